```python
import jax, jax.numpy as jnp
from jax import lax
import numpy as np

D_MODEL = 1024
BATCH = 32
SEQ = 2048
DEPTH = 1
DEC_BATCH = 16
DEC_SEQ = 64
PAST_LEN = 1024

CHUNK = 64
ML_HEADS = 4
ML_DHEAD = D_MODEL // ML_HEADS
ML_WIDTH = ML_HEADS * ML_DHEAD
CONV_W = 4
ATT_HEADS = 16
ATT_DHEAD = D_MODEL // ATT_HEADS
ATT_WIDTH = ATT_HEADS * ATT_DHEAD
BAND_CHUNKS = 8
MAX_REL = 128
D_FF = ((8 * D_MODEL) // 3 + 127) // 128 * 128
ALPHA = (2.0 * DEPTH) ** 0.25
BETA = (8.0 * DEPTH) ** -0.25
LN_EPS = 1e-5
NEG_INF = -1e30
SEG_SIZES = (ML_WIDTH,) * 4 + (ML_HEADS,) * 2 + (ATT_WIDTH,) * 3 + (D_MODEL,) * 2
SPLIT_IDX = [int(s) for s in np.cumsum(SEG_SIZES)[:-1]]
IN_WIDTH = int(sum(SEG_SIZES))

kernel_name = 'hybrid_streaming_mlstm_bandattn_step'


def layer_norm(x, g, b):
    xf = x.astype(jnp.float32)
    mu = jnp.mean(xf, axis=-1, keepdims=True)
    var = jnp.mean(jnp.square(xf - mu), axis=-1, keepdims=True)
    y = (xf - mu) * lax.rsqrt(var + LN_EPS) * g.astype(jnp.float32) + b.astype(jnp.float32)
    return y.astype(x.dtype)


def swiglu_ffn(x, w_gu, w_down):
    gate, up = jnp.split(x @ w_gu, 2, axis=-1)
    return (jax.nn.silu(gate) * up) @ w_down


def causal_dwconv(x_pad, w, b):
    t = x_pad.shape[1] - (CONV_W - 1)
    out = b
    for j in range(CONV_W):
        out = out + x_pad[:, j:j + t] * w[j]
    return out


def mlstm_chunk(carry, inp):
    C, n, m = carry
    q, k, v, lf, li = inp
    L = q.shape[2]
    bcum = jnp.cumsum(lf, axis=-1)
    causal = jnp.tril(jnp.ones((L, L), dtype=bool))
    dmat = jnp.where(causal, bcum[..., :, None] - bcum[..., None, :] + li[..., None, :], -jnp.inf)
    inter = bcum + m[..., None]
    m_t = jnp.maximum(inter, jnp.max(dmat, axis=-1))
    w_inter = jnp.exp(inter - m_t)
    s = jnp.einsum('bhtd,bhsd->bhts', q, k) * jnp.exp(dmat - m_t[..., None])
    num = w_inter[..., None] * jnp.einsum('bhtd,bhde->bhte', q, C) + jnp.einsum('bhts,bhse->bhte', s, v)
    den = w_inter * jnp.einsum('bhtd,bhd->bht', q, n) + jnp.sum(s, axis=-1)
    h = num / jnp.maximum(jnp.abs(den), jnp.exp(-m_t))[..., None]
    b_last = bcum[..., -1]
    g_s = b_last[..., None] - bcum + li
    m_new = jnp.maximum(b_last + m, jnp.max(g_s, axis=-1))
    decay = jnp.exp(b_last + m - m_new)
    w_s = jnp.exp(g_s - m_new[..., None])
    C_new = decay[..., None, None] * C + jnp.einsum('bhs,bhsd,bhse->bhde', w_s, k, v)
    n_new = decay[..., None] * n + jnp.einsum('bhs,bhsd->bhd', w_s, k)
    return (C_new, n_new, m_new), h


def mlstm_run(q, k, v, li, lf, state):
    b, h, t, _ = q.shape
    L = min(t, CHUNK)
    nc = t // L

    def blocks(a):
        return jnp.moveaxis(a.reshape(a.shape[:2] + (nc, L) + a.shape[3:]), 2, 0)

    state, hs = lax.scan(mlstm_chunk, state, (blocks(q), blocks(k), blocks(v), blocks(lf), blocks(li)))
    hs = jnp.moveaxis(hs, 0, 2).reshape(b, h, t, v.shape[-1])
    return hs, state


def rel_bias_block(table, q_pos, k_pos):
    rel = jnp.clip(q_pos[:, None] - k_pos[None, :], -MAX_REL, MAX_REL) + MAX_REL
    return table[:, rel].astype(jnp.float32)


def band_attention_prompt(q, k, v, table):
    b, t, h, d = q.shape
    nc = t // CHUNK
    pad = BAND_CHUNKS * CHUNK
    band = pad + CHUNK
    kp = jnp.pad(k, ((0, 0), (pad, 0), (0, 0), (0, 0)))
    vp = jnp.pad(v, ((0, 0), (pad, 0), (0, 0), (0, 0)))
    bias = rel_bias_block(table, pad + jnp.arange(CHUNK), jnp.arange(band))

    def one_chunk(c):
        qc = lax.dynamic_slice_in_dim(q, c * CHUNK, CHUNK, axis=1)
        kc = lax.dynamic_slice_in_dim(kp, c * CHUNK, band, axis=1)
        vc = lax.dynamic_slice_in_dim(vp, c * CHUNK, band, axis=1)
        s = jnp.einsum('bqhd,bkhd->bhqk', qc, kc).astype(jnp.float32) + bias
        k_valid = (c - BAND_CHUNKS) * CHUNK + jnp.arange(band) >= 0
        p = jax.nn.softmax(jnp.where(k_valid, s, NEG_INF), axis=-1)
        return jnp.einsum('bhqk,bkhd->bqhd', p.astype(vc.dtype), vc)

    o = lax.map(one_chunk, jnp.arange(nc))
    return jnp.moveaxis(o, 0, 1).reshape(b, t, h, d)


def band_attention_sample(q, k_new, v_new, k_past, v_past, table):
    n_past, t = k_past.shape[1], q.shape[1]
    kk = jnp.concatenate([k_past.astype(k_new.dtype), k_new], axis=1)
    vv = jnp.concatenate([v_past.astype(v_new.dtype), v_new], axis=1)
    bias = rel_bias_block(table, n_past + jnp.arange(t), jnp.arange(n_past + t))
    s = jnp.einsum('bqhd,bkhd->bhqk', q, kk).astype(jnp.float32) + bias
    p = jax.nn.softmax(s, axis=-1)
    return jnp.einsum('bhqk,bkhd->bqhd', p.astype(vv.dtype), vv)


def token_mixer(xn, p, conv_prev, ml_state, att_past):
    b, t, _ = xn.shape
    f32 = jnp.float32
    (ml_q, ml_k, ml_v, ml_o, ml_i, ml_f, a_q, a_k, a_v, g_ml, g_att) = jnp.split(xn @ p['w_in'], SPLIT_IDX, axis=-1)
    qk_pad = jnp.concatenate([conv_prev.astype(xn.dtype), jnp.concatenate([ml_q, ml_k], axis=-1)], axis=1)
    qk = jax.nn.silu(causal_dwconv(qk_pad, p['ml_conv_w'], p['ml_conv_b']))
    new_conv = qk_pad[:, qk_pad.shape[1] - (CONV_W - 1):]
    q_c, k_c = jnp.split(qk, 2, axis=-1)

    def ml_heads(a):
        return a.reshape(b, t, ML_HEADS, ML_DHEAD).transpose(0, 2, 1, 3).astype(f32)

    qm = ml_heads(q_c)
    km = ml_heads(k_c) * (ML_DHEAD ** -0.5)
    vm = ml_heads(ml_v)
    li = (ml_i + p['b_ml_i']).astype(f32).transpose(0, 2, 1)
    lf = jax.nn.log_sigmoid((ml_f + p['b_ml_f']).astype(f32)).transpose(0, 2, 1)
    C0, n0, m0 = ml_state
    h_ml, (C1, n1, m1) = mlstm_run(qm, km, vm, li, lf, (C0.astype(f32), n0.astype(f32), m0.astype(f32)))
    h_ml = h_ml.transpose(0, 2, 1, 3)
    mu = jnp.mean(h_ml, axis=-1, keepdims=True)
    var = jnp.mean(jnp.square(h_ml - mu), axis=-1, keepdims=True)
    h_ml = ((h_ml - mu) * lax.rsqrt(var + LN_EPS)).reshape(b, t, ML_WIDTH)
    h_ml = h_ml * p['ml_norm_g'].astype(f32) * jax.nn.sigmoid(ml_o.astype(f32))
    y_ml = h_ml.astype(xn.dtype) @ p['w_ml_proj']
    qa = a_q.reshape(b, t, ATT_HEADS, ATT_DHEAD) * (ATT_DHEAD ** -0.5)
    ka = a_k.reshape(b, t, ATT_HEADS, ATT_DHEAD)
    va = a_v.reshape(b, t, ATT_HEADS, ATT_DHEAD)
    if att_past is None:
        o = band_attention_prompt(qa, ka, va, p['att_rel_bias'])
        keep = min(BAND_CHUNKS * CHUNK, t)
        k_rows, v_rows = ka[:, t - keep:], va[:, t - keep:]
    else:
        o = band_attention_sample(qa, ka, va, att_past[0], att_past[1], p['att_rel_bias'])
        k_rows, v_rows = ka, va
    y_att = o.reshape(b, t, ATT_WIDTH) @ p['w_att_proj']
    merged = jax.nn.sigmoid(g_ml) * y_ml + jax.nn.sigmoid(g_att) * y_att
    new_state = (new_conv, C1.astype(xn.dtype), n1.astype(xn.dtype), m1.astype(xn.dtype), k_rows, v_rows)
    return merged @ p['w_out'], new_state


def encoder_layer(x, p, conv_prev, ml_state, att_past):
    h = layer_norm(ALPHA * x + 0.5 * swiglu_ffn(x, p['ffn1_w_gu'], p['ffn1_w_down']), p['ln1_g'], p['ln1_b'])
    mix, new_state = token_mixer(h, p, conv_prev, ml_state, att_past)
    h = layer_norm(ALPHA * h + mix, p['ln2_g'], p['ln2_b'])
    h = layer_norm(ALPHA * h + 0.5 * swiglu_ffn(h, p['ffn2_w_gu'], p['ffn2_w_down']), p['ln3_g'], p['ln3_b'])
    return h, new_state


def setup_inputs(seed: int = 0) -> dict:
    key = jax.random.key(seed)
    ks = jax.random.split(key, 32)
    f32 = jnp.float32

    def nrm(k, shape, scale):
        return jax.random.normal(k, shape, f32) * scale

    att_past = min(BAND_CHUNKS * CHUNK, PAST_LEN)
    col_scale = np.ones((IN_WIDTH,), np.float32)
    col_scale[SPLIT_IDX[1]:SPLIT_IDX[2]] = BETA
    col_scale[SPLIT_IDX[7]:SPLIT_IDX[8]] = BETA
    return {
        'x_prompt': nrm(ks[0], (BATCH, SEQ, D_MODEL), 1.0),
        'x_sample': nrm(ks[1], (DEC_BATCH, DEC_SEQ, D_MODEL), 1.0),
        'state_ml_conv': nrm(ks[2], (DEPTH, DEC_BATCH, CONV_W - 1, 2 * ML_WIDTH), 1.0),
        'state_ml_C': nrm(ks[3], (DEPTH, DEC_BATCH, ML_HEADS, ML_DHEAD, ML_DHEAD), 0.1),
        'state_ml_n': jnp.abs(nrm(ks[4], (DEPTH, DEC_BATCH, ML_HEADS, ML_DHEAD), 0.1)),
        'state_ml_m': nrm(ks[5], (DEPTH, DEC_BATCH, ML_HEADS), 0.5),
        'cache_att_k': nrm(ks[6], (DEPTH, DEC_BATCH, att_past, ATT_HEADS, ATT_DHEAD), 1.0),
        'cache_att_v': nrm(ks[7], (DEPTH, DEC_BATCH, att_past, ATT_HEADS, ATT_DHEAD), 1.0),
        'w_in': nrm(ks[8], (DEPTH, D_MODEL, IN_WIDTH), D_MODEL ** -0.5) * jnp.asarray(col_scale),
        'b_ml_i': nrm(ks[9], (DEPTH, ML_HEADS), 0.1),
        'b_ml_f': jnp.linspace(3.0, 6.0, ML_HEADS, dtype=f32) + nrm(ks[10], (DEPTH, ML_HEADS), 0.1),
        'ml_conv_w': nrm(ks[11], (DEPTH, CONV_W, 2 * ML_WIDTH), CONV_W ** -0.5),
        'ml_conv_b': nrm(ks[12], (DEPTH, 2 * ML_WIDTH), 0.02),
        'ml_norm_g': 1.0 + nrm(ks[13], (DEPTH, ML_WIDTH), 0.02),
        'att_rel_bias': nrm(ks[14], (DEPTH, ATT_HEADS, 2 * MAX_REL + 1), 0.5),
        'w_ml_proj': nrm(ks[15], (DEPTH, ML_WIDTH, D_MODEL), BETA * ML_WIDTH ** -0.5),
        'w_att_proj': nrm(ks[16], (DEPTH, ATT_WIDTH, D_MODEL), BETA * ATT_WIDTH ** -0.5),
        'w_out': nrm(ks[17], (DEPTH, D_MODEL, D_MODEL), BETA * D_MODEL ** -0.5),
        'ffn1_w_gu': nrm(ks[18], (DEPTH, D_MODEL, 2 * D_FF), D_MODEL ** -0.5),
        'ffn1_w_down': nrm(ks[19], (DEPTH, D_FF, D_MODEL), BETA * D_FF ** -0.5),
        'ffn2_w_gu': nrm(ks[20], (DEPTH, D_MODEL, 2 * D_FF), D_MODEL ** -0.5),
        'ffn2_w_down': nrm(ks[21], (DEPTH, D_FF, D_MODEL), BETA * D_FF ** -0.5),
        'ln1_g': 1.0 + nrm(ks[22], (DEPTH, D_MODEL), 0.02),
        'ln1_b': nrm(ks[23], (DEPTH, D_MODEL), 0.02),
        'ln2_g': 1.0 + nrm(ks[24], (DEPTH, D_MODEL), 0.02),
        'ln2_b': nrm(ks[25], (DEPTH, D_MODEL), 0.02),
        'ln3_g': 1.0 + nrm(ks[26], (DEPTH, D_MODEL), 0.02),
        'ln3_b': nrm(ks[27], (DEPTH, D_MODEL), 0.02),
    }


def reference(x_prompt, x_sample, state_ml_conv, state_ml_C, state_ml_n, state_ml_m, cache_att_k, cache_att_v,
              w_in, b_ml_i, b_ml_f, ml_conv_w, ml_conv_b, ml_norm_g, att_rel_bias, w_ml_proj, w_att_proj, w_out,
              ffn1_w_gu, ffn1_w_down, ffn2_w_gu, ffn2_w_down, ln1_g, ln1_b, ln2_g, ln2_b, ln3_g, ln3_b):
    f32 = jnp.float32
    bp = x_prompt.shape[0]
    y_prompt, y_sample = x_prompt, x_sample
    prompt_states, sample_states = [], []
    for l in range(DEPTH):
        p = {'w_in': w_in[l], 'b_ml_i': b_ml_i[l], 'b_ml_f': b_ml_f[l], 'ml_conv_w': ml_conv_w[l],
             'ml_conv_b': ml_conv_b[l], 'ml_norm_g': ml_norm_g[l], 'att_rel_bias': att_rel_bias[l],
             'w_ml_proj': w_ml_proj[l], 'w_att_proj': w_att_proj[l], 'w_out': w_out[l],
             'ffn1_w_gu': ffn1_w_gu[l], 'ffn1_w_down': ffn1_w_down[l], 'ffn2_w_gu': ffn2_w_gu[l],
             'ffn2_w_down': ffn2_w_down[l], 'ln1_g': ln1_g[l], 'ln1_b': ln1_b[l], 'ln2_g': ln2_g[l],
             'ln2_b': ln2_b[l], 'ln3_g': ln3_g[l], 'ln3_b': ln3_b[l]}
        conv0 = jnp.zeros((bp, CONV_W - 1, 2 * ML_WIDTH), x_prompt.dtype)
        ml0 = (jnp.zeros((bp, ML_HEADS, ML_DHEAD, ML_DHEAD), f32),
               jnp.zeros((bp, ML_HEADS, ML_DHEAD), f32),
               jnp.zeros((bp, ML_HEADS), f32))
        y_prompt, st_p = encoder_layer(y_prompt, p, conv0, ml0, None)
        y_sample, st_s = encoder_layer(y_sample, p, state_ml_conv[l],
                                       (state_ml_C[l], state_ml_n[l], state_ml_m[l]),
                                       (cache_att_k[l], cache_att_v[l]))
        prompt_states.append(st_p)
        sample_states.append(st_s)
    p_conv, p_C, p_n, p_m, p_k, p_v = [jnp.stack(s) for s in zip(*prompt_states)]
    s_conv, s_C, s_n, s_m, s_k, s_v = [jnp.stack(s) for s in zip(*sample_states)]
    return (y_prompt, y_sample, p_conv, s_conv, p_C, s_C, p_n, s_n, p_m, s_m, p_k, s_k, p_v, s_v)
```

```python
import functools

import jax
import jax.numpy as jnp
from jax import lax
from jax.experimental import pallas as pl
from jax.experimental.pallas import tpu as pltpu

F32 = jnp.float32
BF16 = jnp.bfloat16

D_MODEL = 1024
CHUNK = 64
ML_HEADS = 4
ML_DHEAD = D_MODEL // ML_HEADS
ML_WIDTH = ML_HEADS * ML_DHEAD
CONV_W = 4
ATT_HEADS = 16
ATT_DHEAD = D_MODEL // ATT_HEADS
ATT_WIDTH = ATT_HEADS * ATT_DHEAD
BAND_CHUNKS = 8
BAND = BAND_CHUNKS * CHUNK
MAX_REL = 128
D_FF = ((8 * D_MODEL) // 3 + 127) // 128 * 128
ALPHA = 2.0 ** 0.25
LN_EPS = 1e-5
NEG_INF = -1e30

LANES = 128
GATE_LANES = LANES
VMEM_LIMIT = 56 * 1024 * 1024

_O_MLQ, _O_MLK, _O_MLV, _O_MLO = 0, ML_WIDTH, 2 * ML_WIDTH, 3 * ML_WIDTH
_O_MLI = 4 * ML_WIDTH
_O_MLF = _O_MLI + ML_HEADS
_O_AQ = _O_MLF + ML_HEADS
_O_AK = _O_AQ + ATT_WIDTH
_O_AV = _O_AK + ATT_WIDTH
_O_GM = _O_AV + ATT_WIDTH
_O_GA = _O_GM + D_MODEL


def _const_spec(shape):
    nd = len(shape)
    return pl.BlockSpec(shape, lambda *_: (0,) * nd, pipeline_mode=pl.Buffered(1))


def _params(*sem):
    return pltpu.CompilerParams(dimension_semantics=sem, vmem_limit_bytes=VMEM_LIMIT)


def _layer_norm(y, g, b):
    mu = jnp.mean(y, axis=-1, keepdims=True)
    yc = y - mu
    var = jnp.mean(yc * yc, axis=-1, keepdims=True)
    return yc * lax.rsqrt(var + LN_EPS) * g + b


def _sigmoid(x):
    return 1.0 / (1.0 + jnp.exp(-x))


_FF_CHUNKS = ((0, 768), (768, 1792), (1792, D_FF))


def _ffn_ln_kernel(x_ref, wg_ref, wu_ref, wd_ref, g_ref, b_ref, o_ref):
    x = x_ref[...]
    xb = x.astype(BF16)
    acc = None
    for s, e in _FF_CHUNKS:
        gate = jnp.dot(xb, wg_ref[:, s:e], preferred_element_type=F32)
        up = jnp.dot(xb, wu_ref[:, s:e], preferred_element_type=F32)
        hid = (gate * _sigmoid(gate) * up).astype(BF16)
        part = jnp.dot(hid, wd_ref[s:e, :], preferred_element_type=F32)
        acc = part if acc is None else acc + part
    o_ref[...] = _layer_norm(ALPHA * x + 0.5 * acc, g_ref[...], b_ref[...])


def _ffn_ln(x, wg, wu, wd, g, b):
    n = x.shape[0]
    tm = min(512, n)
    row = pl.BlockSpec((tm, D_MODEL), lambda i: (i, 0))
    return pl.pallas_call(
        _ffn_ln_kernel,
        grid=(n // tm,),
        in_specs=[row, _const_spec(wg.shape), _const_spec(wu.shape), _const_spec(wd.shape),
                  _const_spec(g.shape), _const_spec(b.shape)],
        out_specs=row,
        out_shape=jax.ShapeDtypeStruct((n, D_MODEL), F32),
        compiler_params=_params("parallel"),
        name="ffn_ln",
    )(x, wg, wu, wd, g, b)


def _in_proj_kernel(h_ref, cprev_ref, wqk_ref, wv_ref, wo_ref, wif_ref, bif_ref, cw_ref, cb_ref,
                    waq_ref, wak_ref, wav_ref, wgm_ref, wga_ref,
                    q_ref, k_ref, v_ref, so_ref, gt_ref, conv_ref,
                    aq_ref, akt_ref, av_ref, pk_ref, pv_ref, sgm_ref, sga_ref,
                    tail_ref, *, tm):
    t = pl.program_id(1)
    hb = h_ref[0].astype(BF16)

    raw = jnp.dot(hb, wqk_ref[...], preferred_element_type=F32)

    @pl.when(t == 0)
    def _():
        tail_ref[...] = jnp.zeros_like(tail_ref)
        tail_ref[8 - (CONV_W - 1):, :] = cprev_ref[0]

    cat = jnp.concatenate([tail_ref[...], raw], axis=0)
    conv = cb_ref[...] + raw * cw_ref[CONV_W - 1:CONV_W, :]
    for j in range(1, CONV_W):
        shifted = pltpu.roll(cat, j, axis=0)[8:, :]
        conv = conv + shifted * cw_ref[CONV_W - 1 - j:CONV_W - j, :]
    tail_ref[...] = raw[tm - 8:, :]
    conv_ref[0] = raw[tm - (CONV_W - 1):, :]
    qk = conv * _sigmoid(conv)
    q_ref[0] = qk[:, :ML_WIDTH].astype(BF16)
    k_ref[0] = (qk[:, ML_WIDTH:] * (ML_DHEAD ** -0.5)).astype(BF16)

    v_ref[0] = jnp.dot(hb, wv_ref[...], preferred_element_type=F32).astype(BF16)
    so_ref[0] = _sigmoid(jnp.dot(hb, wo_ref[...], preferred_element_type=F32)).astype(BF16)

    zg = jnp.dot(hb, wif_ref[...], preferred_element_type=F32) + bif_ref[...]
    lane = lax.broadcasted_iota(jnp.int32, zg.shape, 1)
    log_sig = jnp.minimum(zg, 0.0) - jnp.log(1.0 + jnp.exp(-jnp.abs(zg)))
    gt_ref[0] = jnp.where(lane < ML_HEADS, zg, log_sig)

    aq_ref[0] = jnp.dot(hb, waq_ref[...], preferred_element_type=F32).astype(BF16)
    ak = jnp.dot(hb, wak_ref[...], preferred_element_type=F32)
    pk_ref[0] = ak
    akt_ref[0] = ak.T.astype(BF16)
    av = jnp.dot(hb, wav_ref[...], preferred_element_type=F32)
    pv_ref[0] = av
    av_ref[0] = av.astype(BF16)
    sgm_ref[0] = _sigmoid(jnp.dot(hb, wgm_ref[...], preferred_element_type=F32)).astype(BF16)
    sga_ref[0] = _sigmoid(jnp.dot(hb, wga_ref[...], preferred_element_type=F32)).astype(BF16)


def _in_proj(h, conv_prev, w):
    b, t, _ = h.shape
    tm = min(256, t)
    nt = t // tm
    keep = min(BAND, t)
    first_kept = (t - keep) // tm

    def tok(width, dtype):
        return (pl.BlockSpec((1, tm, width), lambda i, j: (i, j, 0)),
                jax.ShapeDtypeStruct((b, t, width), dtype))

    def kept(width):
        return (pl.BlockSpec((1, tm, width), lambda i, j: (i, jnp.maximum(j - first_kept, 0), 0)),
                jax.ShapeDtypeStruct((b, keep, width), F32))

    outs = [
        tok(ML_WIDTH, BF16), tok(ML_WIDTH, BF16), tok(ML_WIDTH, BF16), tok(ML_WIDTH, BF16),
        tok(GATE_LANES, F32),
        (pl.BlockSpec((1, CONV_W - 1, 2 * ML_WIDTH), lambda i, j: (i, 0, 0)),
         jax.ShapeDtypeStruct((b, CONV_W - 1, 2 * ML_WIDTH), F32)),
        tok(ATT_WIDTH, BF16),
        (pl.BlockSpec((1, ATT_WIDTH, tm), lambda i, j: (i, 0, j)),
         jax.ShapeDtypeStruct((b, ATT_WIDTH, t), BF16)),
        tok(ATT_WIDTH, BF16),
        kept(ATT_WIDTH), kept(ATT_WIDTH),
        tok(D_MODEL, BF16), tok(D_MODEL, BF16),
    ]
    weights = [w["wqk"], w["wv"], w["wo"], w["wif"], w["bif"], w["conv_w"], w["conv_b"],
               w["waq"], w["wak"], w["wav"], w["wgm"], w["wga"]]
    return pl.pallas_call(
        functools.partial(_in_proj_kernel, tm=tm),
        grid=(b, nt),
        in_specs=[pl.BlockSpec((1, tm, D_MODEL), lambda i, j: (i, j, 0)),
                  pl.BlockSpec((1, CONV_W - 1, 2 * ML_WIDTH), lambda i, j: (i, 0, 0))]
                 + [_const_spec(a.shape) for a in weights],
        out_specs=[o[0] for o in outs],
        out_shape=[o[1] for o in outs],
        scratch_shapes=[pltpu.VMEM((8, 2 * ML_WIDTH), F32)],
        compiler_params=_params("parallel", "arbitrary"),
        name="in_proj",
    )(h, conv_prev, *weights)


def _mlstm_kernel(q_ref, k_ref, v_ref, so_ref, gt_ref, c0_ref, n0_ref, m0_ref, ng_ref,
                  hm_ref, c_ref, n_ref, m_ref, *, blk):
    @pl.when(pl.program_id(1) == 0)
    def _():
        c_ref[...] = c0_ref[...]
        n_ref[...] = n0_ref[...]
        m_ref[...] = m0_ref[...]

    gates = gt_ref[0]
    row = lax.broadcasted_iota(jnp.int32, (blk, blk), 0)
    col = lax.broadcasted_iota(jnp.int32, (blk, blk), 1)
    causal = col <= row
    csum = jnp.dot(causal.astype(F32), gates, preferred_element_type=F32,
                   precision=lax.Precision.HIGHEST)
    bcum = pltpu.roll(csum, LANES - ML_HEADS, axis=1)
    li_rel = gates - bcum
    li_rel_t = li_rel.T
    m_prev = m_ref[0]
    b_last = bcum[blk - 1:blk, :]
    g_s = b_last + li_rel
    m_new = jnp.maximum(b_last + m_prev, jnp.max(g_s, axis=0, keepdims=True))
    decay = jnp.exp(b_last + m_prev - m_new)
    w_s = jnp.exp(g_s - m_new)
    inter = bcum + m_prev

    for j in range(ML_HEADS):
        sl = slice(j * ML_DHEAD, (j + 1) * ML_DHEAD)
        qj, kj, vj = q_ref[0, :, sl], k_ref[0, :, sl], v_ref[0, :, sl]
        dmat = jnp.where(causal, bcum[:, j:j + 1] + li_rel_t[j:j + 1, :], -jnp.inf)
        icol = inter[:, j:j + 1]
        m_t = jnp.maximum(icol, jnp.max(dmat, axis=-1, keepdims=True))
        s = lax.dot_general(qj, kj, (((1,), (1,)), ((), ())), preferred_element_type=F32) * jnp.exp(dmat - m_t)
        w_inter = jnp.exp(icol - m_t)
        c_j = c_ref[0, j]
        n_j = n_ref[0, j:j + 1, :]
        num = (w_inter * jnp.dot(qj, c_j.astype(BF16), preferred_element_type=F32)
               + jnp.dot(s.astype(BF16), vj, preferred_element_type=F32))
        den = (w_inter * jnp.sum(qj.astype(F32) * n_j, axis=-1, keepdims=True)
               + jnp.sum(s, axis=-1, keepdims=True))
        hh = num / jnp.maximum(jnp.abs(den), jnp.exp(-m_t))
        mu = jnp.mean(hh, axis=-1, keepdims=True)
        hc = hh - mu
        var = jnp.mean(hc * hc, axis=-1, keepdims=True)
        hn = hc * lax.rsqrt(var + LN_EPS) * ng_ref[:, sl] * so_ref[0, :, sl].astype(F32)
        hm_ref[0, :, sl] = hn.astype(BF16)

        kw = kj.astype(F32) * w_s[:, j:j + 1]
        d_j = decay[:, j:j + 1]
        c_ref[0, j] = d_j * c_j + lax.dot_general(kw.astype(BF16), vj, (((0,), (0,)), ((), ())),
                                                  preferred_element_type=F32)
        n_ref[0, j:j + 1, :] = d_j * n_j + jnp.sum(kw, axis=0, keepdims=True)
    m_ref[0] = m_new


def _mlstm(q, k, v, so, gates, c0, n0, m0, norm_g):
    b, t, _ = q.shape
    blk = min(256, t)
    tok = pl.BlockSpec((1, blk, ML_WIDTH), lambda i, j: (i, j, 0))
    c_spec = pl.BlockSpec((1, ML_HEADS, ML_DHEAD, ML_DHEAD), lambda i, j: (i, 0, 0, 0))
    n_spec = pl.BlockSpec((1, ML_HEADS, ML_DHEAD), lambda i, j: (i, 0, 0))
    m_spec = pl.BlockSpec((1, 1, LANES), lambda i, j: (i, 0, 0))
    return pl.pallas_call(
        functools.partial(_mlstm_kernel, blk=blk),
        grid=(b, t // blk),
        in_specs=[tok, tok, tok, tok, pl.BlockSpec((1, blk, GATE_LANES), lambda i, j: (i, j, 0)),
                  c_spec, n_spec, m_spec, _const_spec(norm_g.shape)],
        out_specs=[tok, c_spec, n_spec, m_spec],
        out_shape=[jax.ShapeDtypeStruct((b, t, ML_WIDTH), BF16),
                   jax.ShapeDtypeStruct(c0.shape, F32),
                   jax.ShapeDtypeStruct(n0.shape, F32),
                   jax.ShapeDtypeStruct(m0.shape, F32)],
        compiler_params=_params("parallel", "arbitrary"),
        name="mlstm",
    )(q, k, v, so, gates, c0, n0, m0, norm_g)


KEY_BLOCK = 256
N_KEY_BLOCKS = 3


def _band_attn_kernel(q_ref, kt0_ref, kt1_ref, kt2_ref, v0_ref, v1_ref, v2_ref, bias_ref, o_ref,
                      *, lead_blocks, mask_missing):
    kt_refs = (kt0_ref, kt1_ref, kt2_ref)
    v_refs = (v0_ref, v1_ref, v2_ref)
    qb = q_ref.shape[1]
    nkeys = N_KEY_BLOCKS * KEY_BLOCK
    if mask_missing:
        first_valid = (lead_blocks - pl.program_id(1)) * KEY_BLOCK
        kcol = lax.broadcasted_iota(jnp.int32, (1, nkeys), 1)
        missing = jnp.where(kcol >= first_valid, 0.0, NEG_INF)
    lane = lax.broadcasted_iota(jnp.int32, (qb, LANES), 1)
    low = lane < ATT_DHEAD
    for g in range(ATT_HEADS // 2):
        sl = slice(g * LANES, (g + 1) * LANES)
        q2 = q_ref[0, :, sl]
        kt2 = jnp.concatenate([r[0, sl, :] for r in kt_refs], axis=1)
        v2 = jnp.concatenate([r[0, :, sl] for r in v_refs], axis=0)
        halves = []
        for half in range(2):
            keep = low if half == 0 else jnp.logical_not(low)
            qh = jnp.where(keep, q2.astype(F32), 0.0).astype(BF16)
            s = jnp.dot(qh, kt2, preferred_element_type=F32) + bias_ref[2 * g + half]
            if mask_missing:
                s = s + missing
            p = jnp.exp(s - jnp.max(s, axis=-1, keepdims=True))
            l = jnp.sum(p, axis=-1, keepdims=True)
            halves.append(jnp.dot(p.astype(BF16), v2, preferred_element_type=F32) / l)
        o_ref[0, :, sl] = jnp.where(low, halves[0], halves[1]).astype(BF16)


def _band_bias(table, qb):
    nkeys = N_KEY_BLOCKS * KEY_BLOCK
    qi = jnp.arange(qb)
    kj = jnp.arange(nkeys)
    rel = qi[:, None] - (kj[None, :] - (nkeys - qb))
    rel_idx = jnp.clip(rel, -MAX_REL, MAX_REL) + MAX_REL
    chunk_start = (qi // CHUNK) * CHUNK
    kpos = kj - (nkeys - qb)
    in_band = (kpos[None, :] >= chunk_start[:, None] - BAND) & (kpos[None, :] < chunk_start[:, None] + CHUNK)
    return jnp.where(in_band[None], table[:, rel_idx].astype(F32), NEG_INF)


def _band_attn(q, kt, v, bias, *, streaming):
    b, tq, _ = q.shape
    qb = bias.shape[1]
    lead = N_KEY_BLOCKS - qb // KEY_BLOCK if streaming else 0
    if streaming:
        def kidx(s):
            return lambda i, j: jnp.maximum(j - lead + s, 0)
    else:
        def kidx(s):
            return lambda i, j: s
    kt_specs = [pl.BlockSpec((1, ATT_WIDTH, KEY_BLOCK), (lambda f: lambda i, j: (i, 0, f(i, j)))(kidx(s)))
                for s in range(N_KEY_BLOCKS)]
    v_specs = [pl.BlockSpec((1, KEY_BLOCK, ATT_WIDTH), (lambda f: lambda i, j: (i, f(i, j), 0))(kidx(s)))
               for s in range(N_KEY_BLOCKS)]
    tok = pl.BlockSpec((1, qb, ATT_WIDTH), lambda i, j: (i, j, 0))
    return pl.pallas_call(
        functools.partial(_band_attn_kernel, lead_blocks=lead, mask_missing=streaming),
        grid=(b, tq // qb),
        in_specs=[tok] + kt_specs + v_specs + [_const_spec(bias.shape)],
        out_specs=tok,
        out_shape=jax.ShapeDtypeStruct((b, tq, ATT_WIDTH), BF16),
        compiler_params=_params("parallel", "arbitrary"),
        name="band_attn",
    )(q, kt, kt, kt, v, v, v, bias)


def _mix_ln_kernel(h_ref, hm_ref, oa_ref, sgm_ref, sga_ref, wm_ref, wa_ref, wout_ref, g_ref, b_ref, o_ref):
    y_ml = jnp.dot(hm_ref[...], wm_ref[...], preferred_element_type=F32)
    y_att = jnp.dot(oa_ref[...], wa_ref[...], preferred_element_type=F32)
    merged = sgm_ref[...].astype(F32) * y_ml + sga_ref[...].astype(F32) * y_att
    mix = jnp.dot(merged.astype(BF16), wout_ref[...], preferred_element_type=F32)
    o_ref[...] = _layer_norm(ALPHA * h_ref[...] + mix, g_ref[...], b_ref[...])


def _mix_ln(h, hm, oa, sgm, sga, wm, wa, wout, g, b):
    n = h.shape[0]
    tm = min(512, n)
    row = pl.BlockSpec((tm, D_MODEL), lambda i: (i, 0))
    consts = [wm, wa, wout, g, b]
    return pl.pallas_call(
        _mix_ln_kernel,
        grid=(n // tm,),
        in_specs=[row] * 5 + [_const_spec(a.shape) for a in consts],
        out_specs=row,
        out_shape=jax.ShapeDtypeStruct((n, D_MODEL), F32),
        compiler_params=_params("parallel"),
        name="mix_ln",
    )(h, hm, oa, sgm, sga, *consts)


def _prep_weights(p):
    w_in = p["w_in"]

    def cols(start, width):
        return w_in[:, start:start + width]

    pad = jnp.zeros((D_MODEL, GATE_LANES - 2 * ML_HEADS), F32)
    bif = jnp.concatenate([p["b_ml_i"], p["b_ml_f"], jnp.zeros((GATE_LANES - 2 * ML_HEADS,), F32)])
    row = lambda a: a.reshape(1, -1).astype(F32)
    return {
        "wqk": cols(_O_MLQ, 2 * ML_WIDTH).astype(BF16),
        "wv": cols(_O_MLV, ML_WIDTH).astype(BF16),
        "wo": cols(_O_MLO, ML_WIDTH).astype(BF16),
        "wif": jnp.concatenate([cols(_O_MLI, 2 * ML_HEADS), pad], axis=1).astype(BF16),
        "bif": row(bif),
        "conv_w": p["ml_conv_w"].astype(F32),
        "conv_b": row(p["ml_conv_b"]),
        "waq": (cols(_O_AQ, ATT_WIDTH) * (ATT_DHEAD ** -0.5)).astype(BF16),
        "wak": cols(_O_AK, ATT_WIDTH).astype(BF16),
        "wav": cols(_O_AV, ATT_WIDTH).astype(BF16),
        "wgm": cols(_O_GM, D_MODEL).astype(BF16),
        "wga": cols(_O_GA, D_MODEL).astype(BF16),
        "norm_g": row(p["ml_norm_g"]),
        "wm": p["w_ml_proj"].astype(BF16),
        "wa": p["w_att_proj"].astype(BF16),
        "wout": p["w_out"].astype(BF16),
        "ffn1": (p["ffn1_w_gu"][:, :D_FF].astype(BF16), p["ffn1_w_gu"][:, D_FF:].astype(BF16),
                 p["ffn1_w_down"].astype(BF16)),
        "ffn2": (p["ffn2_w_gu"][:, :D_FF].astype(BF16), p["ffn2_w_gu"][:, D_FF:].astype(BF16),
                 p["ffn2_w_down"].astype(BF16)),
        "ln1": (row(p["ln1_g"]), row(p["ln1_b"])),
        "ln2": (row(p["ln2_g"]), row(p["ln2_b"])),
        "ln3": (row(p["ln3_g"]), row(p["ln3_b"])),
        "rel_bias": p["att_rel_bias"],
    }


def _encoder_layer(x, w, conv_prev, ml_state, att_past):
    b, t, _ = x.shape
    n = b * t
    flat = lambda a: a.reshape(n, a.shape[-1])
    h1 = _ffn_ln(flat(x), *w["ffn1"], *w["ln1"])
    (q, k, v, so, gates, new_conv, aq, akt, av, k_rows, v_rows, sgm, sga) = _in_proj(
        h1.reshape(b, t, D_MODEL), conv_prev, w)

    c0, n0, m0 = ml_state
    m0p = jnp.pad(m0.astype(F32), ((0, 0), (0, LANES - ML_HEADS))).reshape(b, 1, LANES)
    hm, c1, n1, m1p = _mlstm(q, k, v, so, gates, c0.astype(F32), n0.astype(F32), m0p, w["norm_g"])
    m1 = m1p[:, 0, :ML_HEADS]

    if att_past is None:
        bias = _band_bias(w["rel_bias"], KEY_BLOCK)
        oa = _band_attn(aq, akt, av, bias, streaming=True)
    else:
        k_past, v_past = att_past
        n_past = k_past.shape[1]
        nkeys = N_KEY_BLOCKS * KEY_BLOCK
        fill = nkeys - n_past - t
        kt_past = k_past.reshape(b, n_past, ATT_WIDTH).transpose(0, 2, 1).astype(BF16)
        kt_all = jnp.concatenate([jnp.zeros((b, ATT_WIDTH, fill), BF16), kt_past, akt], axis=2)
        v_all = jnp.concatenate([jnp.zeros((b, fill, ATT_WIDTH), BF16),
                                 v_past.reshape(b, n_past, ATT_WIDTH).astype(BF16), av], axis=1)
        bias = _band_bias(w["rel_bias"], t)
        oa = _band_attn(aq, kt_all, v_all, bias, streaming=False)

    h2 = _mix_ln(h1, flat(hm), flat(oa), flat(sgm), flat(sga), w["wm"], w["wa"], w["wout"], *w["ln2"])
    y = _ffn_ln(h2, *w["ffn2"], *w["ln3"]).reshape(b, t, D_MODEL)
    keep = k_rows.shape[1]
    state = (new_conv, c1, n1, m1,
             k_rows.reshape(b, keep, ATT_HEADS, ATT_DHEAD), v_rows.reshape(b, keep, ATT_HEADS, ATT_DHEAD))
    return y, state


def kernel(x_prompt, x_sample, state_ml_conv, state_ml_C, state_ml_n, state_ml_m, cache_att_k, cache_att_v,
           w_in, b_ml_i, b_ml_f, ml_conv_w, ml_conv_b, ml_norm_g, att_rel_bias, w_ml_proj, w_att_proj, w_out,
           ffn1_w_gu, ffn1_w_down, ffn2_w_gu, ffn2_w_down, ln1_g, ln1_b, ln2_g, ln2_b, ln3_g, ln3_b):
    depth = w_in.shape[0]
    bp = x_prompt.shape[0]
    y_p, y_s = x_prompt, x_sample
    p_states, s_states = [], []
    for l in range(depth):
        w = _prep_weights({
            "w_in": w_in[l], "b_ml_i": b_ml_i[l], "b_ml_f": b_ml_f[l], "ml_conv_w": ml_conv_w[l],
            "ml_conv_b": ml_conv_b[l], "ml_norm_g": ml_norm_g[l], "att_rel_bias": att_rel_bias[l],
            "w_ml_proj": w_ml_proj[l], "w_att_proj": w_att_proj[l], "w_out": w_out[l],
            "ffn1_w_gu": ffn1_w_gu[l], "ffn1_w_down": ffn1_w_down[l], "ffn2_w_gu": ffn2_w_gu[l],
            "ffn2_w_down": ffn2_w_down[l], "ln1_g": ln1_g[l], "ln1_b": ln1_b[l], "ln2_g": ln2_g[l],
            "ln2_b": ln2_b[l], "ln3_g": ln3_g[l], "ln3_b": ln3_b[l]})
        conv0 = jnp.zeros((bp, CONV_W - 1, 2 * ML_WIDTH), F32)
        ml0 = (jnp.zeros((bp, ML_HEADS, ML_DHEAD, ML_DHEAD), F32),
               jnp.zeros((bp, ML_HEADS, ML_DHEAD), F32),
               jnp.zeros((bp, ML_HEADS), F32))
        y_p, st_p = _encoder_layer(y_p, w, conv0, ml0, None)
        y_s, st_s = _encoder_layer(y_s, w, state_ml_conv[l],
                                   (state_ml_C[l], state_ml_n[l], state_ml_m[l]),
                                   (cache_att_k[l], cache_att_v[l]))
        p_states.append(st_p)
        s_states.append(st_s)
    p_conv, p_c, p_n, p_m, p_k, p_v = [jnp.stack(s) for s in zip(*p_states)]
    s_conv, s_c, s_n, s_m, s_k, s_v = [jnp.stack(s) for s in zip(*s_states)]
    return (y_p, y_s, p_conv, s_conv, p_c, s_c, p_n, s_n, p_m, s_m, p_k, s_k, p_v, s_v)
```

```python
import functools

import jax
import jax.numpy as jnp
import numpy as np
from jax import lax
from jax.experimental import pallas as pl
from jax.experimental.pallas import tpu as pltpu

F32 = jnp.float32
BF16 = jnp.bfloat16

D_MODEL = 1024
CHUNK = 64
ML_HEADS = 4
ML_DHEAD = D_MODEL // ML_HEADS
ML_WIDTH = ML_HEADS * ML_DHEAD
CONV_W = 4
ATT_HEADS = 16
ATT_DHEAD = D_MODEL // ATT_HEADS
ATT_WIDTH = ATT_HEADS * ATT_DHEAD
BAND_CHUNKS = 8
BAND = BAND_CHUNKS * CHUNK
MAX_REL = 128
D_FF = ((8 * D_MODEL) // 3 + 127) // 128 * 128
ALPHA = 2.0 ** 0.25
LN_EPS = 1e-5
NEG_INF = -1e30

LANES = 128
GATE_LANES = LANES
VMEM_LIMIT = 56 * 1024 * 1024

_O_MLQ, _O_MLK, _O_MLV, _O_MLO = 0, ML_WIDTH, 2 * ML_WIDTH, 3 * ML_WIDTH
_O_MLI = 4 * ML_WIDTH
_O_MLF = _O_MLI + ML_HEADS
_O_AQ = _O_MLF + ML_HEADS
_O_AK = _O_AQ + ATT_WIDTH
_O_AV = _O_AK + ATT_WIDTH
_O_GM = _O_AV + ATT_WIDTH
_O_GA = _O_GM + D_MODEL


def _const_spec(shape):
    nd = len(shape)
    return pl.BlockSpec(shape, lambda *_: (0,) * nd, pipeline_mode=pl.Buffered(1))


def _params(*sem):
    return pltpu.CompilerParams(dimension_semantics=sem, vmem_limit_bytes=VMEM_LIMIT)


def _layer_norm(y, g, b):
    mu = jnp.mean(y, axis=-1, keepdims=True)
    yc = y - mu
    var = jnp.mean(yc * yc, axis=-1, keepdims=True)
    return yc * lax.rsqrt(var + LN_EPS) * g + b


def _sigmoid(x):
    return 1.0 / (1.0 + jnp.exp(-x))


_FF_CHUNKS = ((0, 768), (768, 1792), (1792, D_FF))


def _ffn_ln_kernel(x_ref, wg_ref, wu_ref, wd_ref, g_ref, b_ref, o_ref):
    x = x_ref[...]
    xb = x.astype(BF16)
    acc = None
    for s, e in _FF_CHUNKS:
        gate = jnp.dot(xb, wg_ref[:, s:e], preferred_element_type=F32)
        up = jnp.dot(xb, wu_ref[:, s:e], preferred_element_type=F32)
        hid = (gate * _sigmoid(gate) * up).astype(BF16)
        part = jnp.dot(hid, wd_ref[s:e, :], preferred_element_type=F32)
        acc = part if acc is None else acc + part
    o_ref[...] = _layer_norm(ALPHA * x + 0.5 * acc, g_ref[...], b_ref[...])


def _ffn_ln(x, wg, wu, wd, g, b):
    n = x.shape[0]
    tm = min(512, n)
    row = pl.BlockSpec((tm, D_MODEL), lambda i: (i, 0))
    return pl.pallas_call(
        _ffn_ln_kernel,
        grid=(n // tm,),
        in_specs=[row, _const_spec(wg.shape), _const_spec(wu.shape), _const_spec(wd.shape),
                  _const_spec(g.shape), _const_spec(b.shape)],
        out_specs=row,
        out_shape=jax.ShapeDtypeStruct((n, D_MODEL), F32),
        compiler_params=_params("parallel"),
        name="ffn_ln",
    )(x, wg, wu, wd, g, b)


COL_CHUNK = 256
CONV_ROWS = 16


def _in_proj_kernel(h_ref, cprev_ref, wqk_ref, wv_ref, wo_ref, wif_ref, bif_ref, cw_ref, cb_ref,
                    waq_ref, wak_ref, wav_ref, wgm_ref, wga_ref,
                    q_ref, k_ref, v_ref, so_ref, gt_ref, conv_ref,
                    aq_ref, akt_ref, av_ref, pk_ref, pv_ref, sgm_ref, sga_ref,
                    hb_ref, raw_ref, *, tm):
    hb_ref[...] = h_ref[0].astype(BF16)

    @pl.when(pl.program_id(1) == 0)
    def _():
        raw_ref[0:8, :] = jnp.zeros((8, 2 * ML_WIDTH), F32)
        raw_ref[8 - (CONV_W - 1):8, :] = cprev_ref[0]

    def proj(w_ref, c):
        cols = slice(c * COL_CHUNK, (c + 1) * COL_CHUNK)
        return jnp.dot(hb_ref[...], w_ref[:, cols], preferred_element_type=F32), cols

    def ml_qk(c):
        y, cols = proj(wqk_ref, c)
        raw_ref[8:8 + tm, cols] = y
        is_k = c * COL_CHUNK >= ML_WIDTH
        o_ref, o_cols = (k_ref, slice(cols.start - ML_WIDTH, cols.stop - ML_WIDTH)) if is_k else (q_ref, cols)
        for r in range(tm // CONV_ROWS):
            base = 8 + r * CONV_ROWS
            acc = cb_ref[:, cols]
            for j in range(CONV_W):
                acc = acc + raw_ref[base - j:base - j + CONV_ROWS, cols] * cw_ref[CONV_W - 1 - j:CONV_W - j, cols]
            qk = acc * _sigmoid(acc)
            if is_k:
                qk = qk * (ML_DHEAD ** -0.5)
            o_ref[0, r * CONV_ROWS:(r + 1) * CONV_ROWS, o_cols] = qk.astype(BF16)

    def plain(w_ref, o_ref):
        def job(c):
            y, cols = proj(w_ref, c)
            o_ref[0, :, cols] = y.astype(BF16)
        return job

    def gated(w_ref, o_ref):
        def job(c):
            y, cols = proj(w_ref, c)
            o_ref[0, :, cols] = _sigmoid(y).astype(BF16)
        return job

    def att_k(c):
        y, cols = proj(wak_ref, c)
        pk_ref[0, :, cols] = y
        akt_ref[0, cols, :] = y.T.astype(BF16)

    def att_v(c):
        y, cols = proj(wav_ref, c)
        pv_ref[0, :, cols] = y
        av_ref[0, :, cols] = y.astype(BF16)

    n_chunks = D_MODEL // COL_CHUNK
    heavy = [functools.partial(ml_qk, c) for c in range(2 * n_chunks)]
    medium = [functools.partial(job, c)
              for job in (gated(wo_ref, so_ref), gated(wgm_ref, sgm_ref), att_k, gated(wga_ref, sga_ref))
              for c in range(n_chunks)]
    light = [functools.partial(job, c)
             for job in (plain(wv_ref, v_ref), plain(waq_ref, aq_ref), att_v) for c in range(n_chunks)]
    order = []
    for i in range(len(medium)):
        order += heavy[i:i + 1] + light[i:i + 1] + medium[i:i + 1]
    for job in order:
        job()
    conv_ref[0] = raw_ref[8 + tm - (CONV_W - 1):8 + tm, :]
    raw_ref[0:8, :] = raw_ref[tm:tm + 8, :]

    zg = jnp.dot(hb_ref[...], wif_ref[...], preferred_element_type=F32) + bif_ref[...]
    lane = lax.broadcasted_iota(jnp.int32, zg.shape, 1)
    log_sig = jnp.minimum(zg, 0.0) - jnp.log(1.0 + jnp.exp(-jnp.abs(zg)))
    gt_ref[0] = jnp.where(lane < ML_HEADS, zg, log_sig)


def _in_proj(h, conv_prev, w):
    b, t, _ = h.shape
    tm = min(256, t)
    nt = t // tm
    keep = min(BAND, t)
    first_kept = (t - keep) // tm

    def tok(width, dtype):
        return (pl.BlockSpec((1, tm, width), lambda i, j: (i, j, 0)),
                jax.ShapeDtypeStruct((b, t, width), dtype))

    def kept(width):
        return (pl.BlockSpec((1, tm, width), lambda i, j: (i, jnp.maximum(j - first_kept, 0), 0)),
                jax.ShapeDtypeStruct((b, keep, width), F32))

    outs = [
        tok(ML_WIDTH, BF16), tok(ML_WIDTH, BF16), tok(ML_WIDTH, BF16), tok(ML_WIDTH, BF16),
        tok(GATE_LANES, F32),
        (pl.BlockSpec((1, CONV_W - 1, 2 * ML_WIDTH), lambda i, j: (i, 0, 0)),
         jax.ShapeDtypeStruct((b, CONV_W - 1, 2 * ML_WIDTH), F32)),
        tok(ATT_WIDTH, BF16),
        (pl.BlockSpec((1, ATT_WIDTH, tm), lambda i, j: (i, 0, j)),
         jax.ShapeDtypeStruct((b, ATT_WIDTH, t), BF16)),
        tok(ATT_WIDTH, BF16),
        kept(ATT_WIDTH), kept(ATT_WIDTH),
        tok(D_MODEL, BF16), tok(D_MODEL, BF16),
    ]
    weights = [w["wqk"], w["wv"], w["wo"], w["wif"], w["bif"], w["conv_w"], w["conv_b"],
               w["waq"], w["wak"], w["wav"], w["wgm"], w["wga"]]
    return pl.pallas_call(
        functools.partial(_in_proj_kernel, tm=tm),
        grid=(b, nt),
        in_specs=[pl.BlockSpec((1, tm, D_MODEL), lambda i, j: (i, j, 0)),
                  pl.BlockSpec((1, CONV_W - 1, 2 * ML_WIDTH), lambda i, j: (i, 0, 0))]
                 + [_const_spec(a.shape) for a in weights],
        out_specs=[o[0] for o in outs],
        out_shape=[o[1] for o in outs],
        scratch_shapes=[pltpu.VMEM((tm, D_MODEL), BF16), pltpu.VMEM((tm + 8, 2 * ML_WIDTH), F32)],
        compiler_params=_params("parallel", "arbitrary"),
        name="in_proj",
    )(h, conv_prev, *weights)


def _mlstm_kernel(q_ref, k_ref, v_ref, so_ref, gt_ref, c0_ref, n0_ref, m0_ref, ng_ref,
                  hm_ref, c_ref, n_ref, m_ref, *, blk):
    @pl.when(pl.program_id(1) == 0)
    def _():
        c_ref[...] = c0_ref[...]
        n_ref[...] = n0_ref[...]
        m_ref[...] = m0_ref[...]

    gates = gt_ref[0]
    row = lax.broadcasted_iota(jnp.int32, (blk, blk), 0)
    col = lax.broadcasted_iota(jnp.int32, (blk, blk), 1)
    causal = col <= row
    csum = jnp.dot(causal.astype(F32), gates, preferred_element_type=F32,
                   precision=lax.Precision.HIGHEST)
    bcum = pltpu.roll(csum, LANES - ML_HEADS, axis=1)
    li_rel = gates - bcum
    li_rel_t = li_rel.T
    m_prev = m_ref[0]
    b_last = bcum[blk - 1:blk, :]
    g_s = b_last + li_rel
    m_new = jnp.maximum(b_last + m_prev, jnp.max(g_s, axis=0, keepdims=True))
    decay = jnp.exp(b_last + m_prev - m_new)
    w_s = jnp.exp(g_s - m_new)
    inter = bcum + m_prev

    for j in range(ML_HEADS):
        sl = slice(j * ML_DHEAD, (j + 1) * ML_DHEAD)
        qj, kj, vj = q_ref[0, :, sl], k_ref[0, :, sl], v_ref[0, :, sl]
        dmat = jnp.where(causal, bcum[:, j:j + 1] + li_rel_t[j:j + 1, :], -jnp.inf)
        icol = inter[:, j:j + 1]
        m_t = jnp.maximum(icol, jnp.max(dmat, axis=-1, keepdims=True))
        s = lax.dot_general(qj, kj, (((1,), (1,)), ((), ())), preferred_element_type=F32) * jnp.exp(dmat - m_t)
        w_inter = jnp.exp(icol - m_t)
        c_j = c_ref[0, j]
        n_j = n_ref[0, j:j + 1, :]
        num = (w_inter * jnp.dot(qj, c_j.astype(BF16), preferred_element_type=F32)
               + jnp.dot(s.astype(BF16), vj, preferred_element_type=F32))
        den = (w_inter * jnp.sum(qj.astype(F32) * n_j, axis=-1, keepdims=True)
               + jnp.sum(s, axis=-1, keepdims=True))
        hh = num / jnp.maximum(jnp.abs(den), jnp.exp(-m_t))
        mu = jnp.mean(hh, axis=-1, keepdims=True)
        hc = hh - mu
        var = jnp.mean(hc * hc, axis=-1, keepdims=True)
        hn = hc * lax.rsqrt(var + LN_EPS) * ng_ref[:, sl] * so_ref[0, :, sl].astype(F32)
        hm_ref[0, :, sl] = hn.astype(BF16)

        kw = kj.astype(F32) * w_s[:, j:j + 1]
        d_j = decay[:, j:j + 1]
        c_ref[0, j] = d_j * c_j + lax.dot_general(kw.astype(BF16), vj, (((0,), (0,)), ((), ())),
                                                  preferred_element_type=F32)
        n_ref[0, j:j + 1, :] = d_j * n_j + jnp.sum(kw, axis=0, keepdims=True)
    m_ref[0] = m_new


def _mlstm(q, k, v, so, gates, c0, n0, m0, norm_g):
    b, t, _ = q.shape
    blk = min(256, t)
    tok = pl.BlockSpec((1, blk, ML_WIDTH), lambda i, j: (i, j, 0))
    c_spec = pl.BlockSpec((1, ML_HEADS, ML_DHEAD, ML_DHEAD), lambda i, j: (i, 0, 0, 0))
    n_spec = pl.BlockSpec((1, ML_HEADS, ML_DHEAD), lambda i, j: (i, 0, 0))
    m_spec = pl.BlockSpec((1, 1, LANES), lambda i, j: (i, 0, 0))
    return pl.pallas_call(
        functools.partial(_mlstm_kernel, blk=blk),
        grid=(b, t // blk),
        in_specs=[tok, tok, tok, tok, pl.BlockSpec((1, blk, GATE_LANES), lambda i, j: (i, j, 0)),
                  c_spec, n_spec, m_spec, _const_spec(norm_g.shape)],
        out_specs=[tok, c_spec, n_spec, m_spec],
        out_shape=[jax.ShapeDtypeStruct((b, t, ML_WIDTH), BF16),
                   jax.ShapeDtypeStruct(c0.shape, F32),
                   jax.ShapeDtypeStruct(n0.shape, F32),
                   jax.ShapeDtypeStruct(m0.shape, F32)],
        compiler_params=_params("parallel", "arbitrary"),
        name="mlstm",
    )(q, k, v, so, gates, c0, n0, m0, norm_g)


KEY_BLOCK = 256
N_KEY_BLOCKS = 3


def _band_attn_kernel(q_ref, kt0_ref, kt1_ref, kt2_ref, v0_ref, v1_ref, v2_ref, bias_ref, o_ref,
                      *, lead_blocks, mask_missing):
    kt_refs = (kt0_ref, kt1_ref, kt2_ref)
    v_refs = (v0_ref, v1_ref, v2_ref)
    qb = q_ref.shape[1]
    nkeys = N_KEY_BLOCKS * KEY_BLOCK
    if mask_missing:
        first_valid = (lead_blocks - pl.program_id(1)) * KEY_BLOCK
        kcol = lax.broadcasted_iota(jnp.int32, (1, nkeys), 1)
        missing = jnp.where(kcol >= first_valid, 0.0, NEG_INF)
    lane = lax.broadcasted_iota(jnp.int32, (qb, LANES), 1)
    low = lane < ATT_DHEAD
    for g in range(ATT_HEADS // 2):
        sl = slice(g * LANES, (g + 1) * LANES)
        q2 = q_ref[0, :, sl]
        kt2 = jnp.concatenate([r[0, sl, :] for r in kt_refs], axis=1)
        v2 = jnp.concatenate([r[0, :, sl] for r in v_refs], axis=0)
        halves = []
        for half in range(2):
            keep = low if half == 0 else jnp.logical_not(low)
            qh = jnp.where(keep, q2.astype(F32), 0.0).astype(BF16)
            s = jnp.dot(qh, kt2, preferred_element_type=F32) + bias_ref[2 * g + half]
            if mask_missing:
                s = s + missing
            p = jnp.exp(s - jnp.max(s, axis=-1, keepdims=True))
            l = jnp.sum(p, axis=-1, keepdims=True)
            halves.append(jnp.dot(p.astype(BF16), v2, preferred_element_type=F32) / l)
        o_ref[0, :, sl] = jnp.where(low, halves[0], halves[1]).astype(BF16)


def _band_bias(table, qb):
    nkeys = N_KEY_BLOCKS * KEY_BLOCK
    heads = table.shape[0]
    period = nkeys + qb
    shift = np.arange(period)
    shift = np.where(shift < nkeys, shift, shift - period)
    rel_idx = np.clip(nkeys - qb - shift, -MAX_REL, MAX_REL) + MAX_REL
    diag = table[:, rel_idx].astype(F32)
    toep = jnp.tile(diag, (1, qb))[:, :qb * (period - 1)].reshape(heads, qb, period - 1)[:, :, :nkeys]
    qi = np.arange(qb)
    chunk_start = (qi // CHUNK) * CHUNK
    kpos = np.arange(nkeys) - (nkeys - qb)
    in_band = (kpos[None, :] >= chunk_start[:, None] - BAND) & (kpos[None, :] < chunk_start[:, None] + CHUNK)
    return jnp.where(jnp.asarray(in_band)[None], toep, NEG_INF)


def _band_attn(q, kt, v, bias, *, streaming):
    b, tq, _ = q.shape
    qb = bias.shape[1]
    lead = N_KEY_BLOCKS - qb // KEY_BLOCK if streaming else 0
    if streaming:
        def kidx(s):
            return lambda i, j: jnp.maximum(j - lead + s, 0)
    else:
        def kidx(s):
            return lambda i, j: s
    kt_specs = [pl.BlockSpec((1, ATT_WIDTH, KEY_BLOCK), (lambda f: lambda i, j: (i, 0, f(i, j)))(kidx(s)))
                for s in range(N_KEY_BLOCKS)]
    v_specs = [pl.BlockSpec((1, KEY_BLOCK, ATT_WIDTH), (lambda f: lambda i, j: (i, f(i, j), 0))(kidx(s)))
               for s in range(N_KEY_BLOCKS)]
    tok = pl.BlockSpec((1, qb, ATT_WIDTH), lambda i, j: (i, j, 0))
    return pl.pallas_call(
        functools.partial(_band_attn_kernel, lead_blocks=lead, mask_missing=streaming),
        grid=(b, tq // qb),
        in_specs=[tok] + kt_specs + v_specs + [_const_spec(bias.shape)],
        out_specs=tok,
        out_shape=jax.ShapeDtypeStruct((b, tq, ATT_WIDTH), BF16),
        compiler_params=_params("parallel", "arbitrary"),
        name="band_attn",
    )(q, kt, kt, kt, v, v, v, bias)


def _mix_ln_kernel(h_ref, hm_ref, oa_ref, sgm_ref, sga_ref, wm_ref, wa_ref, wout_ref, g_ref, b_ref, o_ref):
    y_ml = jnp.dot(hm_ref[...], wm_ref[...], preferred_element_type=F32)
    y_att = jnp.dot(oa_ref[...], wa_ref[...], preferred_element_type=F32)
    merged = sgm_ref[...].astype(F32) * y_ml + sga_ref[...].astype(F32) * y_att
    mix = jnp.dot(merged.astype(BF16), wout_ref[...], preferred_element_type=F32)
    o_ref[...] = _layer_norm(ALPHA * h_ref[...] + mix, g_ref[...], b_ref[...])


def _mix_ln(h, hm, oa, sgm, sga, wm, wa, wout, g, b):
    n = h.shape[0]
    tm = min(512, n)
    row = pl.BlockSpec((tm, D_MODEL), lambda i: (i, 0))
    consts = [wm, wa, wout, g, b]
    return pl.pallas_call(
        _mix_ln_kernel,
        grid=(n // tm,),
        in_specs=[row] * 5 + [_const_spec(a.shape) for a in consts],
        out_specs=row,
        out_shape=jax.ShapeDtypeStruct((n, D_MODEL), F32),
        compiler_params=_params("parallel"),
        name="mix_ln",
    )(h, hm, oa, sgm, sga, *consts)


def _prep_weights(p):
    w_in = p["w_in"]

    def cols(start, width):
        return w_in[:, start:start + width]

    pad = jnp.zeros((D_MODEL, GATE_LANES - 2 * ML_HEADS), F32)
    bif = jnp.concatenate([p["b_ml_i"], p["b_ml_f"], jnp.zeros((GATE_LANES - 2 * ML_HEADS,), F32)])
    row = lambda a: a.reshape(1, -1).astype(F32)
    return {
        "wqk": cols(_O_MLQ, 2 * ML_WIDTH).astype(BF16),
        "wv": cols(_O_MLV, ML_WIDTH).astype(BF16),
        "wo": cols(_O_MLO, ML_WIDTH).astype(BF16),
        "wif": jnp.concatenate([cols(_O_MLI, 2 * ML_HEADS), pad], axis=1).astype(BF16),
        "bif": row(bif),
        "conv_w": p["ml_conv_w"].astype(F32),
        "conv_b": row(p["ml_conv_b"]),
        "waq": (cols(_O_AQ, ATT_WIDTH) * (ATT_DHEAD ** -0.5)).astype(BF16),
        "wak": cols(_O_AK, ATT_WIDTH).astype(BF16),
        "wav": cols(_O_AV, ATT_WIDTH).astype(BF16),
        "wgm": cols(_O_GM, D_MODEL).astype(BF16),
        "wga": cols(_O_GA, D_MODEL).astype(BF16),
        "norm_g": row(p["ml_norm_g"]),
        "wm": p["w_ml_proj"].astype(BF16),
        "wa": p["w_att_proj"].astype(BF16),
        "wout": p["w_out"].astype(BF16),
        "ffn1": (p["ffn1_w_gu"][:, :D_FF].astype(BF16), p["ffn1_w_gu"][:, D_FF:].astype(BF16),
                 p["ffn1_w_down"].astype(BF16)),
        "ffn2": (p["ffn2_w_gu"][:, :D_FF].astype(BF16), p["ffn2_w_gu"][:, D_FF:].astype(BF16),
                 p["ffn2_w_down"].astype(BF16)),
        "ln1": (row(p["ln1_g"]), row(p["ln1_b"])),
        "ln2": (row(p["ln2_g"]), row(p["ln2_b"])),
        "ln3": (row(p["ln3_g"]), row(p["ln3_b"])),
        "rel_bias": p["att_rel_bias"],
    }


def _encoder_layer(x, w, conv_prev, ml_state, att_past):
    b, t, _ = x.shape
    n = b * t
    flat = lambda a: a.reshape(n, a.shape[-1])
    h1 = _ffn_ln(flat(x), *w["ffn1"], *w["ln1"])
    (q, k, v, so, gates, new_conv, aq, akt, av, k_rows, v_rows, sgm, sga) = _in_proj(
        h1.reshape(b, t, D_MODEL), conv_prev, w)

    c0, n0, m0 = ml_state
    m0p = jnp.pad(m0.astype(F32), ((0, 0), (0, LANES - ML_HEADS))).reshape(b, 1, LANES)
    hm, c1, n1, m1p = _mlstm(q, k, v, so, gates, c0.astype(F32), n0.astype(F32), m0p, w["norm_g"])
    m1 = m1p[:, 0, :ML_HEADS]

    if att_past is None:
        bias = _band_bias(w["rel_bias"], KEY_BLOCK)
        oa = _band_attn(aq, akt, av, bias, streaming=True)
    else:
        k_past, v_past = att_past
        n_past = k_past.shape[1]
        nkeys = N_KEY_BLOCKS * KEY_BLOCK
        fill = nkeys - n_past - t
        kt_past = k_past.reshape(b, n_past, ATT_WIDTH).transpose(0, 2, 1).astype(BF16)
        kt_all = jnp.concatenate([jnp.zeros((b, ATT_WIDTH, fill), BF16), kt_past, akt], axis=2)
        v_all = jnp.concatenate([jnp.zeros((b, fill, ATT_WIDTH), BF16),
                                 v_past.reshape(b, n_past, ATT_WIDTH).astype(BF16), av], axis=1)
        bias = _band_bias(w["rel_bias"], t)
        oa = _band_attn(aq, kt_all, v_all, bias, streaming=False)

    h2 = _mix_ln(h1, flat(hm), flat(oa), flat(sgm), flat(sga), w["wm"], w["wa"], w["wout"], *w["ln2"])
    y = _ffn_ln(h2, *w["ffn2"], *w["ln3"]).reshape(b, t, D_MODEL)
    keep = k_rows.shape[1]
    state = (new_conv, c1, n1, m1,
             k_rows.reshape(b, keep, ATT_HEADS, ATT_DHEAD), v_rows.reshape(b, keep, ATT_HEADS, ATT_DHEAD))
    return y, state


def kernel(x_prompt, x_sample, state_ml_conv, state_ml_C, state_ml_n, state_ml_m, cache_att_k, cache_att_v,
           w_in, b_ml_i, b_ml_f, ml_conv_w, ml_conv_b, ml_norm_g, att_rel_bias, w_ml_proj, w_att_proj, w_out,
           ffn1_w_gu, ffn1_w_down, ffn2_w_gu, ffn2_w_down, ln1_g, ln1_b, ln2_g, ln2_b, ln3_g, ln3_b):
    depth = w_in.shape[0]
    bp = x_prompt.shape[0]
    y_p, y_s = x_prompt, x_sample
    p_states, s_states = [], []
    for l in range(depth):
        w = _prep_weights({
            "w_in": w_in[l], "b_ml_i": b_ml_i[l], "b_ml_f": b_ml_f[l], "ml_conv_w": ml_conv_w[l],
            "ml_conv_b": ml_conv_b[l], "ml_norm_g": ml_norm_g[l], "att_rel_bias": att_rel_bias[l],
            "w_ml_proj": w_ml_proj[l], "w_att_proj": w_att_proj[l], "w_out": w_out[l],
            "ffn1_w_gu": ffn1_w_gu[l], "ffn1_w_down": ffn1_w_down[l], "ffn2_w_gu": ffn2_w_gu[l],
            "ffn2_w_down": ffn2_w_down[l], "ln1_g": ln1_g[l], "ln1_b": ln1_b[l], "ln2_g": ln2_g[l],
            "ln2_b": ln2_b[l], "ln3_g": ln3_g[l], "ln3_b": ln3_b[l]})
        conv0 = jnp.zeros((bp, CONV_W - 1, 2 * ML_WIDTH), F32)
        ml0 = (jnp.zeros((bp, ML_HEADS, ML_DHEAD, ML_DHEAD), F32),
               jnp.zeros((bp, ML_HEADS, ML_DHEAD), F32),
               jnp.zeros((bp, ML_HEADS), F32))
        y_p, st_p = _encoder_layer(y_p, w, conv0, ml0, None)
        y_s, st_s = _encoder_layer(y_s, w, state_ml_conv[l],
                                   (state_ml_C[l], state_ml_n[l], state_ml_m[l]),
                                   (cache_att_k[l], cache_att_v[l]))
        p_states.append(st_p)
        s_states.append(st_s)
    p_conv, p_c, p_n, p_m, p_k, p_v = [jnp.stack(s) for s in zip(*p_states)]
    s_conv, s_c, s_n, s_m, s_k, s_v = [jnp.stack(s) for s in zip(*s_states)]
    return (y_p, y_s, p_conv, s_conv, p_c, s_c, p_n, s_n, p_m, s_m, p_k, s_k, p_v, s_v)
```

```python
import functools

import jax
import jax.numpy as jnp
import numpy as np
from jax import lax
from jax.experimental import pallas as pl
from jax.experimental.pallas import tpu as pltpu

F32 = jnp.float32
BF16 = jnp.bfloat16

D_MODEL = 1024
CHUNK = 64
ML_HEADS = 4
ML_DHEAD = D_MODEL // ML_HEADS
ML_WIDTH = ML_HEADS * ML_DHEAD
CONV_W = 4
ATT_HEADS = 16
ATT_DHEAD = D_MODEL // ATT_HEADS
ATT_WIDTH = ATT_HEADS * ATT_DHEAD
BAND_CHUNKS = 8
BAND = BAND_CHUNKS * CHUNK
MAX_REL = 128
D_FF = ((8 * D_MODEL) // 3 + 127) // 128 * 128
ALPHA = 2.0 ** 0.25
LN_EPS = 1e-5
NEG_INF = -1e30

LANES = 128
GATE_LANES = LANES
VMEM_LIMIT = 56 * 1024 * 1024

_O_MLQ, _O_MLK, _O_MLV, _O_MLO = 0, ML_WIDTH, 2 * ML_WIDTH, 3 * ML_WIDTH
_O_MLI = 4 * ML_WIDTH
_O_MLF = _O_MLI + ML_HEADS
_O_AQ = _O_MLF + ML_HEADS
_O_AK = _O_AQ + ATT_WIDTH
_O_AV = _O_AK + ATT_WIDTH
_O_GM = _O_AV + ATT_WIDTH
_O_GA = _O_GM + D_MODEL


def _const_spec(shape):
    nd = len(shape)
    return pl.BlockSpec(shape, lambda *_: (0,) * nd, pipeline_mode=pl.Buffered(1))


def _params(*sem):
    return pltpu.CompilerParams(dimension_semantics=sem, vmem_limit_bytes=VMEM_LIMIT)


def _layer_norm(y, g, b):
    mu = jnp.mean(y, axis=-1, keepdims=True)
    yc = y - mu
    var = jnp.mean(yc * yc, axis=-1, keepdims=True)
    return yc * lax.rsqrt(var + LN_EPS) * g + b


def _sigmoid(x):
    return 1.0 / (1.0 + jnp.exp(-x))


_FF_CHUNKS = ((0, 768), (768, 1792), (1792, D_FF))


def _ffn_ln_kernel(x_ref, wg_ref, wu_ref, wd_ref, g_ref, b_ref, o_ref):
    x = x_ref[...]
    xb = x.astype(BF16)
    acc = None
    for s, e in _FF_CHUNKS:
        gate = jnp.dot(xb, wg_ref[:, s:e], preferred_element_type=F32)
        up = jnp.dot(xb, wu_ref[:, s:e], preferred_element_type=F32)
        hid = (gate * _sigmoid(gate) * up).astype(BF16)
        part = jnp.dot(hid, wd_ref[s:e, :], preferred_element_type=F32)
        acc = part if acc is None else acc + part
    o_ref[...] = _layer_norm(ALPHA * x + 0.5 * acc, g_ref[...], b_ref[...])


def _ffn_ln(x, wg, wu, wd, g, b):
    n = x.shape[0]
    tm = min(512, n)
    row = pl.BlockSpec((tm, D_MODEL), lambda i: (i, 0))
    return pl.pallas_call(
        _ffn_ln_kernel,
        grid=(n // tm,),
        in_specs=[row, _const_spec(wg.shape), _const_spec(wu.shape), _const_spec(wd.shape),
                  _const_spec(g.shape), _const_spec(b.shape)],
        out_specs=row,
        out_shape=jax.ShapeDtypeStruct((n, D_MODEL), F32),
        compiler_params=_params("parallel"),
        name="ffn_ln",
    )(x, wg, wu, wd, g, b)


COL_CHUNK = 256
CONV_ROWS = 16


def _chunk_jobs(hb_ref):
    def proj(w_ref, c):
        cols = slice(c * COL_CHUNK, (c + 1) * COL_CHUNK)
        return jnp.dot(hb_ref[...], w_ref[:, cols], preferred_element_type=F32), cols

    def plain(w_ref, o_ref):
        def job(c):
            y, cols = proj(w_ref, c)
            o_ref[0, :, cols] = y.astype(BF16)
        return job

    def gated(w_ref, o_ref):
        def job(c):
            y, cols = proj(w_ref, c)
            o_ref[0, :, cols] = _sigmoid(y).astype(BF16)
        return job

    return proj, plain, gated


def _interleave(*queues):
    order = []
    for i in range(max(len(q) for q in queues)):
        for q in queues:
            order += q[i:i + 1]
    return order


def _in_proj_ml_kernel(h_ref, cprev_ref, wqk_ref, wv_ref, wo_ref, wif_ref, bif_ref, cw_ref, cb_ref,
                       q_ref, k_ref, v_ref, so_ref, gt_ref, conv_ref, hb_ref, raw_ref, *, tm):
    hb_ref[...] = h_ref[0].astype(BF16)
    proj, plain, gated = _chunk_jobs(hb_ref)

    @pl.when(pl.program_id(1) == 0)
    def _():
        raw_ref[0:8, :] = jnp.zeros((8, 2 * ML_WIDTH), F32)
        raw_ref[8 - (CONV_W - 1):8, :] = cprev_ref[0]

    def ml_qk(c):
        y, cols = proj(wqk_ref, c)
        raw_ref[8:8 + tm, cols] = y
        is_k = c * COL_CHUNK >= ML_WIDTH
        o_ref, o_cols = (k_ref, slice(cols.start - ML_WIDTH, cols.stop - ML_WIDTH)) if is_k else (q_ref, cols)
        sub = lax.broadcasted_iota(jnp.int32, (8, COL_CHUNK), 0)
        for r in range(tm // CONV_ROWS):
            base = 8 + r * CONV_ROWS
            blocks = [raw_ref[base + 8 * i:base + 8 * i + 8, cols] for i in range(-1, CONV_ROWS // 8)]
            taps = [jnp.concatenate(blocks[1:], axis=0)]
            for j in range(1, CONV_W):
                taps.append(jnp.concatenate(
                    [pltpu.roll(jnp.where(sub < 8 - j, cur, prev), j, axis=0)
                     for prev, cur in zip(blocks[:-1], blocks[1:])], axis=0))
            acc = cb_ref[:, cols]
            for j in range(CONV_W):
                acc = acc + taps[j] * cw_ref[CONV_W - 1 - j, :, cols]
            qk = acc * _sigmoid(acc)
            if is_k:
                qk = qk * (ML_DHEAD ** -0.5)
            o_ref[0, r * CONV_ROWS:(r + 1) * CONV_ROWS, o_cols] = qk.astype(BF16)

    n_chunks = D_MODEL // COL_CHUNK
    heavy = [functools.partial(ml_qk, c) for c in range(2 * n_chunks)]
    light = [functools.partial(job, c) for c in range(n_chunks)
             for job in (plain(wv_ref, v_ref), gated(wo_ref, so_ref))]
    for job in _interleave(heavy, light):
        job()
    conv_ref[0] = raw_ref[8 + tm - (CONV_W - 1):8 + tm, :]
    raw_ref[0:8, :] = raw_ref[tm:tm + 8, :]

    zg = jnp.dot(hb_ref[...], wif_ref[...], preferred_element_type=F32) + bif_ref[...]
    lane = lax.broadcasted_iota(jnp.int32, zg.shape, 1)
    log_sig = jnp.minimum(zg, 0.0) - jnp.log(1.0 + jnp.exp(-jnp.abs(zg)))
    gt_ref[0] = jnp.where(lane < ML_HEADS, zg, log_sig)


def _in_proj_att_kernel(h_ref, waq_ref, wak_ref, wav_ref, wgm_ref, wga_ref,
                        aq_ref, akt_ref, av_ref, pk_ref, pv_ref, sgm_ref, sga_ref, hb_ref):
    hb_ref[...] = h_ref[0].astype(BF16)
    proj, plain, gated = _chunk_jobs(hb_ref)

    def att_k(c):
        y, cols = proj(wak_ref, c)
        pk_ref[0, :, cols] = y
        akt_ref[0, cols, :] = y.astype(BF16).T

    def att_v(c):
        y, cols = proj(wav_ref, c)
        pv_ref[0, :, cols] = y
        av_ref[0, :, cols] = y.astype(BF16)

    n_chunks = D_MODEL // COL_CHUNK
    heavy = [functools.partial(job, c) for c in range(n_chunks)
             for job in (gated(wgm_ref, sgm_ref), att_k, gated(wga_ref, sga_ref))]
    light = [functools.partial(job, c) for c in range(n_chunks) for job in (plain(waq_ref, aq_ref), att_v)]
    for job in _interleave(heavy, light):
        job()


def _in_proj(h, conv_prev, w):
    b, t, _ = h.shape
    tm = min(512, t)
    nt = t // tm
    keep = min(BAND, t)
    first_kept = (t - keep) // tm
    h_spec = pl.BlockSpec((1, tm, D_MODEL), lambda i, j: (i, j, 0))

    def tok(width, dtype):
        return (pl.BlockSpec((1, tm, width), lambda i, j: (i, j, 0)),
                jax.ShapeDtypeStruct((b, t, width), dtype))

    def kept(width):
        return (pl.BlockSpec((1, tm, width), lambda i, j: (i, jnp.maximum(j - first_kept, 0), 0)),
                jax.ShapeDtypeStruct((b, keep, width), F32))

    conv_spec = pl.BlockSpec((1, CONV_W - 1, 2 * ML_WIDTH), lambda i, j: (i, 0, 0))
    ml_outs = [
        tok(ML_WIDTH, BF16), tok(ML_WIDTH, BF16), tok(ML_WIDTH, BF16), tok(ML_WIDTH, BF16),
        tok(GATE_LANES, F32),
        (conv_spec, jax.ShapeDtypeStruct((b, CONV_W - 1, 2 * ML_WIDTH), F32)),
    ]
    ml_weights = [w["wqk"], w["wv"], w["wo"], w["wif"], w["bif"], w["conv_w"], w["conv_b"]]
    ml = pl.pallas_call(
        functools.partial(_in_proj_ml_kernel, tm=tm),
        grid=(b, nt),
        in_specs=[h_spec, conv_spec] + [_const_spec(a.shape) for a in ml_weights],
        out_specs=[o[0] for o in ml_outs],
        out_shape=[o[1] for o in ml_outs],
        scratch_shapes=[pltpu.VMEM((tm, D_MODEL), BF16), pltpu.VMEM((tm + 8, 2 * ML_WIDTH), F32)],
        compiler_params=_params("parallel", "arbitrary"),
        name="in_proj_ml",
    )(h, conv_prev, *ml_weights)

    att_outs = [
        tok(ATT_WIDTH, BF16),
        (pl.BlockSpec((1, ATT_WIDTH, tm), lambda i, j: (i, 0, j)),
         jax.ShapeDtypeStruct((b, ATT_WIDTH, t), BF16)),
        tok(ATT_WIDTH, BF16),
        kept(ATT_WIDTH), kept(ATT_WIDTH),
        tok(D_MODEL, BF16), tok(D_MODEL, BF16),
    ]
    att_weights = [w["waq"], w["wak"], w["wav"], w["wgm"], w["wga"]]
    att = pl.pallas_call(
        _in_proj_att_kernel,
        grid=(b, nt),
        in_specs=[h_spec] + [_const_spec(a.shape) for a in att_weights],
        out_specs=[o[0] for o in att_outs],
        out_shape=[o[1] for o in att_outs],
        scratch_shapes=[pltpu.VMEM((tm, D_MODEL), BF16)],
        compiler_params=_params("parallel", "arbitrary"),
        name="in_proj_att",
    )(h, *att_weights)
    return tuple(ml) + tuple(att)


def _mlstm_kernel(q_ref, k_ref, v_ref, so_ref, gt_ref, c0_ref, n0_ref, m0_ref, ng_ref,
                  hm_ref, c_ref, n_ref, m_ref, *, blk):
    @pl.when(pl.program_id(1) == 0)
    def _():
        c_ref[...] = c0_ref[...]
        n_ref[...] = n0_ref[...]
        m_ref[...] = m0_ref[...]

    gates = gt_ref[0]
    row = lax.broadcasted_iota(jnp.int32, (blk, blk), 0)
    col = lax.broadcasted_iota(jnp.int32, (blk, blk), 1)
    causal = col <= row
    csum = jnp.dot(causal.astype(F32), gates, preferred_element_type=F32,
                   precision=lax.Precision.HIGHEST)
    bcum = pltpu.roll(csum, LANES - ML_HEADS, axis=1)
    li_rel = gates - bcum
    li_rel_t = li_rel.T
    m_prev = m_ref[0]
    b_last = bcum[blk - 1:blk, :]
    g_s = b_last + li_rel
    m_new = jnp.maximum(b_last + m_prev, jnp.max(g_s, axis=0, keepdims=True))
    decay = jnp.exp(b_last + m_prev - m_new)
    w_s = jnp.exp(g_s - m_new)
    inter = bcum + m_prev

    for j in range(ML_HEADS):
        sl = slice(j * ML_DHEAD, (j + 1) * ML_DHEAD)
        qj, kj, vj = q_ref[0, :, sl], k_ref[0, :, sl], v_ref[0, :, sl]
        dmat = jnp.where(causal, bcum[:, j:j + 1] + li_rel_t[j:j + 1, :], -jnp.inf)
        icol = inter[:, j:j + 1]
        m_t = jnp.maximum(icol, jnp.max(dmat, axis=-1, keepdims=True))
        s = lax.dot_general(qj, kj, (((1,), (1,)), ((), ())), preferred_element_type=F32) * jnp.exp(dmat - m_t)
        w_inter = jnp.exp(icol - m_t)
        c_j = c_ref[0, j]
        n_j = n_ref[0, j:j + 1, :]
        num = (w_inter * jnp.dot(qj, c_j.astype(BF16), preferred_element_type=F32)
               + jnp.dot(s.astype(BF16), vj, preferred_element_type=F32))
        den = (w_inter * jnp.sum(qj.astype(F32) * n_j, axis=-1, keepdims=True)
               + jnp.sum(s, axis=-1, keepdims=True))
        hh = num / jnp.maximum(jnp.abs(den), jnp.exp(-m_t))
        mu = jnp.mean(hh, axis=-1, keepdims=True)
        hc = hh - mu
        var = jnp.mean(hc * hc, axis=-1, keepdims=True)
        hn = hc * lax.rsqrt(var + LN_EPS) * ng_ref[:, sl] * so_ref[0, :, sl].astype(F32)
        hm_ref[0, :, sl] = hn.astype(BF16)

        kw = kj.astype(F32) * w_s[:, j:j + 1]
        d_j = decay[:, j:j + 1]
        c_ref[0, j] = d_j * c_j + lax.dot_general(kw.astype(BF16), vj, (((0,), (0,)), ((), ())),
                                                  preferred_element_type=F32)
        n_ref[0, j:j + 1, :] = d_j * n_j + jnp.sum(kw, axis=0, keepdims=True)
    m_ref[0] = m_new


def _mlstm(q, k, v, so, gates, c0, n0, m0, norm_g):
    b, t, _ = q.shape
    blk = min(256, t)
    tok = pl.BlockSpec((1, blk, ML_WIDTH), lambda i, j: (i, j, 0))
    c_spec = pl.BlockSpec((1, ML_HEADS, ML_DHEAD, ML_DHEAD), lambda i, j: (i, 0, 0, 0))
    n_spec = pl.BlockSpec((1, ML_HEADS, ML_DHEAD), lambda i, j: (i, 0, 0))
    m_spec = pl.BlockSpec((1, 1, LANES), lambda i, j: (i, 0, 0))
    return pl.pallas_call(
        functools.partial(_mlstm_kernel, blk=blk),
        grid=(b, t // blk),
        in_specs=[tok, tok, tok, tok, pl.BlockSpec((1, blk, GATE_LANES), lambda i, j: (i, j, 0)),
                  c_spec, n_spec, m_spec, _const_spec(norm_g.shape)],
        out_specs=[tok, c_spec, n_spec, m_spec],
        out_shape=[jax.ShapeDtypeStruct((b, t, ML_WIDTH), BF16),
                   jax.ShapeDtypeStruct(c0.shape, F32),
                   jax.ShapeDtypeStruct(n0.shape, F32),
                   jax.ShapeDtypeStruct(m0.shape, F32)],
        compiler_params=_params("parallel", "arbitrary"),
        name="mlstm",
    )(q, k, v, so, gates, c0, n0, m0, norm_g)


KEY_BLOCK = 256
N_KEY_BLOCKS = 3
N_STAGE = 4
ROW_BLOCK = 64


def _band_attn_kernel(q_ref, kt0_ref, kt1_ref, kt2_ref, v0_ref, v1_ref, v2_ref, bias_ref, o_ref,
                      s_ref, p_ref, mx_ref, linv_ref, even_ref, *, lead_blocks):
    kt_refs = (kt0_ref, kt1_ref, kt2_ref)
    v_refs = (v0_ref, v1_ref, v2_ref)
    qb = q_ref.shape[1]
    lane = lax.broadcasted_iota(jnp.int32, (qb, LANES), 1)
    low = lane < ATT_DHEAD

    def attend(first_slot):
        slots = range(first_slot, N_KEY_BLOCKS)
        k_lo = first_slot * KEY_BLOCK

        def scores(head):
            sl = slice(head // 2 * LANES, (head // 2 + 1) * LANES)
            q2 = q_ref[0, :, sl].astype(F32)
            qh = jnp.where(low if head % 2 == 0 else jnp.logical_not(low), q2, 0.0).astype(BF16)
            row_max = None
            for c in slots:
                cols = slice(c * KEY_BLOCK, (c + 1) * KEY_BLOCK)
                s = jnp.dot(qh, kt_refs[c][0, sl, :], preferred_element_type=F32) + bias_ref[head, :, cols]
                s_ref[head % N_STAGE, :, cols] = s
                part = jnp.maximum(s[:, :LANES], s[:, LANES:])
                row_max = part if row_max is None else jnp.maximum(row_max, part)
            mx_ref[head % N_STAGE] = row_max

        def weighted_values(head):
            sl = slice(head // 2 * LANES, (head // 2 + 1) * LANES)
            buf = head % N_STAGE
            rb = min(ROW_BLOCK, qb)
            for r in range(qb // rb):
                rows = slice(r * rb, (r + 1) * rb)
                m = jnp.max(mx_ref[buf, rows, :], axis=-1, keepdims=True)
                row_sum = None
                for c in slots:
                    cols = slice(c * KEY_BLOCK, (c + 1) * KEY_BLOCK)
                    p = jnp.exp(s_ref[buf, rows, cols] - m)
                    p_ref[buf, rows, cols] = p.astype(BF16)
                    part = p[:, :LANES] + p[:, LANES:]
                    row_sum = part if row_sum is None else row_sum + part
                l = jnp.sum(row_sum, axis=-1, keepdims=True)
                linv_ref[buf, rows, :] = jnp.broadcast_to(1.0 / l, (rb, LANES))
            v2 = jnp.concatenate([v_refs[c][0, :, sl] for c in slots], axis=0)
            return jnp.dot(p_ref[buf, :, k_lo:], v2, preferred_element_type=F32) * linv_ref[buf]

        scores(0)
        for head in range(ATT_HEADS):
            if head + 1 < ATT_HEADS:
                scores(head + 1)
            out = weighted_values(head)
            if head % 2 == 0:
                even_ref[...] = out
            else:
                sl = slice(head // 2 * LANES, (head // 2 + 1) * LANES)
                o_ref[0, :, sl] = jnp.where(low, even_ref[...], out).astype(BF16)

    if lead_blocks == 0:
        attend(0)
    else:
        j = pl.program_id(1)
        for missing in range(lead_blocks, 0, -1):
            pl.when(j == lead_blocks - missing)(functools.partial(attend, missing))
        pl.when(j >= lead_blocks)(functools.partial(attend, 0))


def _band_bias(table, qb):
    nkeys = N_KEY_BLOCKS * KEY_BLOCK
    heads = table.shape[0]
    period = nkeys + qb
    shift = np.arange(period)
    shift = np.where(shift < nkeys, shift, shift - period)
    rel_idx = np.clip(nkeys - qb - shift, -MAX_REL, MAX_REL) + MAX_REL
    diag = table[:, rel_idx].astype(F32)
    toep = jnp.tile(diag, (1, qb))[:, :qb * (period - 1)].reshape(heads, qb, period - 1)[:, :, :nkeys]
    qi = np.arange(qb)
    chunk_start = (qi // CHUNK) * CHUNK
    kpos = np.arange(nkeys) - (nkeys - qb)
    in_band = (kpos[None, :] >= chunk_start[:, None] - BAND) & (kpos[None, :] < chunk_start[:, None] + CHUNK)
    return jnp.where(jnp.asarray(in_band)[None], toep, NEG_INF)


def _band_attn(q, kt, v, bias, *, streaming):
    b, tq, _ = q.shape
    qb = bias.shape[1]
    lead = N_KEY_BLOCKS - qb // KEY_BLOCK if streaming else 0
    if streaming:
        def kidx(s):
            return lambda i, j: jnp.maximum(j - lead + s, 0)
    else:
        def kidx(s):
            return lambda i, j: s
    kt_specs = [pl.BlockSpec((1, ATT_WIDTH, KEY_BLOCK), (lambda f: lambda i, j: (i, 0, f(i, j)))(kidx(s)))
                for s in range(N_KEY_BLOCKS)]
    v_specs = [pl.BlockSpec((1, KEY_BLOCK, ATT_WIDTH), (lambda f: lambda i, j: (i, f(i, j), 0))(kidx(s)))
               for s in range(N_KEY_BLOCKS)]
    tok = pl.BlockSpec((1, qb, ATT_WIDTH), lambda i, j: (i, j, 0))
    return pl.pallas_call(
        functools.partial(_band_attn_kernel, lead_blocks=lead),
        grid=(b, tq // qb),
        in_specs=[tok] + kt_specs + v_specs + [_const_spec(bias.shape)],
        out_specs=tok,
        out_shape=jax.ShapeDtypeStruct((b, tq, ATT_WIDTH), BF16),
        scratch_shapes=[pltpu.VMEM((N_STAGE, qb, N_KEY_BLOCKS * KEY_BLOCK), F32),
                        pltpu.VMEM((N_STAGE, qb, N_KEY_BLOCKS * KEY_BLOCK), BF16),
                        pltpu.VMEM((N_STAGE, qb, LANES), F32),
                        pltpu.VMEM((N_STAGE, qb, LANES), F32),
                        pltpu.VMEM((qb, LANES), F32)],
        compiler_params=_params("parallel", "arbitrary"),
        name="band_attn",
    )(q, kt, kt, kt, v, v, v, bias)


def _mix_ln_kernel(h_ref, hm_ref, oa_ref, sgm_ref, sga_ref, wm_ref, wa_ref, wout_ref, g_ref, b_ref, o_ref):
    y_ml = jnp.dot(hm_ref[...], wm_ref[...], preferred_element_type=F32)
    y_att = jnp.dot(oa_ref[...], wa_ref[...], preferred_element_type=F32)
    merged = sgm_ref[...].astype(F32) * y_ml + sga_ref[...].astype(F32) * y_att
    mix = jnp.dot(merged.astype(BF16), wout_ref[...], preferred_element_type=F32)
    o_ref[...] = _layer_norm(ALPHA * h_ref[...] + mix, g_ref[...], b_ref[...])


def _mix_ln(h, hm, oa, sgm, sga, wm, wa, wout, g, b):
    n = h.shape[0]
    tm = min(512, n)
    row = pl.BlockSpec((tm, D_MODEL), lambda i: (i, 0))
    consts = [wm, wa, wout, g, b]
    return pl.pallas_call(
        _mix_ln_kernel,
        grid=(n // tm,),
        in_specs=[row] * 5 + [_const_spec(a.shape) for a in consts],
        out_specs=row,
        out_shape=jax.ShapeDtypeStruct((n, D_MODEL), F32),
        compiler_params=_params("parallel"),
        name="mix_ln",
    )(h, hm, oa, sgm, sga, *consts)


def _prep_weights(p):
    w_in = p["w_in"]

    def cols(start, width):
        return w_in[:, start:start + width]

    pad = jnp.zeros((D_MODEL, GATE_LANES - 2 * ML_HEADS), F32)
    bif = jnp.concatenate([p["b_ml_i"], p["b_ml_f"], jnp.zeros((GATE_LANES - 2 * ML_HEADS,), F32)])
    row = lambda a: a.reshape(1, -1).astype(F32)
    return {
        "wqk": cols(_O_MLQ, 2 * ML_WIDTH).astype(BF16),
        "wv": cols(_O_MLV, ML_WIDTH).astype(BF16),
        "wo": cols(_O_MLO, ML_WIDTH).astype(BF16),
        "wif": jnp.concatenate([cols(_O_MLI, 2 * ML_HEADS), pad], axis=1).astype(BF16),
        "bif": row(bif),
        "conv_w": jnp.broadcast_to(p["ml_conv_w"].astype(F32)[:, None, :], (CONV_W, CONV_ROWS, 2 * ML_WIDTH)),
        "conv_b": jnp.broadcast_to(row(p["ml_conv_b"]), (CONV_ROWS, 2 * ML_WIDTH)),
        "waq": (cols(_O_AQ, ATT_WIDTH) * (ATT_DHEAD ** -0.5)).astype(BF16),
        "wak": cols(_O_AK, ATT_WIDTH).astype(BF16),
        "wav": cols(_O_AV, ATT_WIDTH).astype(BF16),
        "wgm": cols(_O_GM, D_MODEL).astype(BF16),
        "wga": cols(_O_GA, D_MODEL).astype(BF16),
        "norm_g": row(p["ml_norm_g"]),
        "wm": p["w_ml_proj"].astype(BF16),
        "wa": p["w_att_proj"].astype(BF16),
        "wout": p["w_out"].astype(BF16),
        "ffn1": (p["ffn1_w_gu"][:, :D_FF].astype(BF16), p["ffn1_w_gu"][:, D_FF:].astype(BF16),
                 p["ffn1_w_down"].astype(BF16)),
        "ffn2": (p["ffn2_w_gu"][:, :D_FF].astype(BF16), p["ffn2_w_gu"][:, D_FF:].astype(BF16),
                 p["ffn2_w_down"].astype(BF16)),
        "ln1": (row(p["ln1_g"]), row(p["ln1_b"])),
        "ln2": (row(p["ln2_g"]), row(p["ln2_b"])),
        "ln3": (row(p["ln3_g"]), row(p["ln3_b"])),
        "rel_bias": p["att_rel_bias"],
    }


def _encoder_layer(x, w, conv_prev, ml_state, att_past):
    b, t, _ = x.shape
    n = b * t
    flat = lambda a: a.reshape(n, a.shape[-1])
    h1 = _ffn_ln(flat(x), *w["ffn1"], *w["ln1"])
    (q, k, v, so, gates, new_conv, aq, akt, av, k_rows, v_rows, sgm, sga) = _in_proj(
        h1.reshape(b, t, D_MODEL), conv_prev, w)

    c0, n0, m0 = ml_state
    m0p = jnp.pad(m0.astype(F32), ((0, 0), (0, LANES - ML_HEADS))).reshape(b, 1, LANES)
    hm, c1, n1, m1p = _mlstm(q, k, v, so, gates, c0.astype(F32), n0.astype(F32), m0p, w["norm_g"])
    m1 = m1p[:, 0, :ML_HEADS]

    if att_past is None:
        bias = _band_bias(w["rel_bias"], KEY_BLOCK)
        oa = _band_attn(aq, akt, av, bias, streaming=True)
    else:
        k_past, v_past = att_past
        n_past = k_past.shape[1]
        nkeys = N_KEY_BLOCKS * KEY_BLOCK
        fill = nkeys - n_past - t
        kt_past = k_past.reshape(b, n_past, ATT_WIDTH).transpose(0, 2, 1).astype(BF16)
        kt_all = jnp.concatenate([jnp.zeros((b, ATT_WIDTH, fill), BF16), kt_past, akt], axis=2)
        v_all = jnp.concatenate([jnp.zeros((b, fill, ATT_WIDTH), BF16),
                                 v_past.reshape(b, n_past, ATT_WIDTH).astype(BF16), av], axis=1)
        bias = _band_bias(w["rel_bias"], t)
        oa = _band_attn(aq, kt_all, v_all, bias, streaming=False)

    h2 = _mix_ln(h1, flat(hm), flat(oa), flat(sgm), flat(sga), w["wm"], w["wa"], w["wout"], *w["ln2"])
    y = _ffn_ln(h2, *w["ffn2"], *w["ln3"]).reshape(b, t, D_MODEL)
    keep = k_rows.shape[1]
    state = (new_conv, c1, n1, m1,
             k_rows.reshape(b, keep, ATT_HEADS, ATT_DHEAD), v_rows.reshape(b, keep, ATT_HEADS, ATT_DHEAD))
    return y, state


def kernel(x_prompt, x_sample, state_ml_conv, state_ml_C, state_ml_n, state_ml_m, cache_att_k, cache_att_v,
           w_in, b_ml_i, b_ml_f, ml_conv_w, ml_conv_b, ml_norm_g, att_rel_bias, w_ml_proj, w_att_proj, w_out,
           ffn1_w_gu, ffn1_w_down, ffn2_w_gu, ffn2_w_down, ln1_g, ln1_b, ln2_g, ln2_b, ln3_g, ln3_b):
    depth = w_in.shape[0]
    bp = x_prompt.shape[0]
    y_p, y_s = x_prompt, x_sample
    p_states, s_states = [], []
    for l in range(depth):
        w = _prep_weights({
            "w_in": w_in[l], "b_ml_i": b_ml_i[l], "b_ml_f": b_ml_f[l], "ml_conv_w": ml_conv_w[l],
            "ml_conv_b": ml_conv_b[l], "ml_norm_g": ml_norm_g[l], "att_rel_bias": att_rel_bias[l],
            "w_ml_proj": w_ml_proj[l], "w_att_proj": w_att_proj[l], "w_out": w_out[l],
            "ffn1_w_gu": ffn1_w_gu[l], "ffn1_w_down": ffn1_w_down[l], "ffn2_w_gu": ffn2_w_gu[l],
            "ffn2_w_down": ffn2_w_down[l], "ln1_g": ln1_g[l], "ln1_b": ln1_b[l], "ln2_g": ln2_g[l],
            "ln2_b": ln2_b[l], "ln3_g": ln3_g[l], "ln3_b": ln3_b[l]})
        conv0 = jnp.zeros((bp, CONV_W - 1, 2 * ML_WIDTH), F32)
        ml0 = (jnp.zeros((bp, ML_HEADS, ML_DHEAD, ML_DHEAD), F32),
               jnp.zeros((bp, ML_HEADS, ML_DHEAD), F32),
               jnp.zeros((bp, ML_HEADS), F32))
        y_p, st_p = _encoder_layer(y_p, w, conv0, ml0, None)
        y_s, st_s = _encoder_layer(y_s, w, state_ml_conv[l],
                                   (state_ml_C[l], state_ml_n[l], state_ml_m[l]),
                                   (cache_att_k[l], cache_att_v[l]))
        p_states.append(st_p)
        s_states.append(st_s)
    p_conv, p_c, p_n, p_m, p_k, p_v = [jnp.stack(s) for s in zip(*p_states)]
    s_conv, s_c, s_n, s_m, s_k, s_v = [jnp.stack(s) for s in zip(*s_states)]
    return (y_p, y_s, p_conv, s_conv, p_c, s_c, p_n, s_n, p_m, s_m, p_k, s_k, p_v, s_v)
```

```python
import functools

import jax
import jax.numpy as jnp
import numpy as np
from jax import lax
from jax.experimental import pallas as pl
from jax.experimental.pallas import tpu as pltpu

F32 = jnp.float32
BF16 = jnp.bfloat16

D_MODEL = 1024
CHUNK = 64
ML_HEADS = 4
ML_DHEAD = D_MODEL // ML_HEADS
ML_WIDTH = ML_HEADS * ML_DHEAD
CONV_W = 4
ATT_HEADS = 16
ATT_DHEAD = D_MODEL // ATT_HEADS
ATT_WIDTH = ATT_HEADS * ATT_DHEAD
BAND_CHUNKS = 8
BAND = BAND_CHUNKS * CHUNK
MAX_REL = 128
D_FF = ((8 * D_MODEL) // 3 + 127) // 128 * 128
ALPHA = 2.0 ** 0.25
LN_EPS = 1e-5
NEG_INF = -1e30

LANES = 128
GATE_LANES = LANES
VMEM_LIMIT = 56 * 1024 * 1024

_O_MLQ, _O_MLK, _O_MLV, _O_MLO = 0, ML_WIDTH, 2 * ML_WIDTH, 3 * ML_WIDTH
_O_MLI = 4 * ML_WIDTH
_O_MLF = _O_MLI + ML_HEADS
_O_AQ = _O_MLF + ML_HEADS
_O_AK = _O_AQ + ATT_WIDTH
_O_AV = _O_AK + ATT_WIDTH
_O_GM = _O_AV + ATT_WIDTH
_O_GA = _O_GM + D_MODEL


def _const_spec(shape):
    nd = len(shape)
    return pl.BlockSpec(shape, lambda *_: (0,) * nd, pipeline_mode=pl.Buffered(1))


def _params(*sem):
    return pltpu.CompilerParams(dimension_semantics=sem, vmem_limit_bytes=VMEM_LIMIT)


def _layer_norm(y, g, b):
    mu = jnp.mean(y, axis=-1, keepdims=True)
    yc = y - mu
    var = jnp.mean(yc * yc, axis=-1, keepdims=True)
    return yc * lax.rsqrt(var + LN_EPS) * g + b


def _sigmoid(x):
    return 1.0 / (1.0 + jnp.exp(-x))


_FF_CHUNKS = ((0, 768), (768, 1792), (1792, D_FF))


def _ffn_ln_kernel(x_ref, wg_ref, wu_ref, wd_ref, g_ref, b_ref, o_ref):
    x = x_ref[...]
    xb = x.astype(BF16)
    acc = None
    for s, e in _FF_CHUNKS:
        gate = jnp.dot(xb, wg_ref[:, s:e], preferred_element_type=F32)
        up = jnp.dot(xb, wu_ref[:, s:e], preferred_element_type=F32)
        hid = (gate * _sigmoid(gate) * up).astype(BF16)
        part = jnp.dot(hid, wd_ref[s:e, :], preferred_element_type=F32)
        acc = part if acc is None else acc + part
    o_ref[...] = _layer_norm(ALPHA * x + 0.5 * acc, g_ref[...], b_ref[...])


def _ffn_ln(x, wg, wu, wd, g, b):
    n = x.shape[0]
    tm = min(512, n)
    row = pl.BlockSpec((tm, D_MODEL), lambda i: (i, 0))
    return pl.pallas_call(
        _ffn_ln_kernel,
        grid=(n // tm,),
        in_specs=[row, _const_spec(wg.shape), _const_spec(wu.shape), _const_spec(wd.shape),
                  _const_spec(g.shape), _const_spec(b.shape)],
        out_specs=row,
        out_shape=jax.ShapeDtypeStruct((n, D_MODEL), F32),
        compiler_params=_params("parallel"),
        name="ffn_ln",
    )(x, wg, wu, wd, g, b)


COL_CHUNK = 256
CONV_ROWS = 16


def _chunk_jobs(hb_ref):
    def proj(w_ref, c):
        cols = slice(c * COL_CHUNK, (c + 1) * COL_CHUNK)
        return jnp.dot(hb_ref[...], w_ref[:, cols], preferred_element_type=F32), cols

    def plain(w_ref, o_ref):
        def job(c):
            y, cols = proj(w_ref, c)
            o_ref[0, :, cols] = y.astype(BF16)
        return job

    def gated(w_ref, o_ref):
        def job(c):
            y, cols = proj(w_ref, c)
            o_ref[0, :, cols] = _sigmoid(y).astype(BF16)
        return job

    return proj, plain, gated


def _interleave(*queues):
    order = []
    for i in range(max(len(q) for q in queues)):
        for q in queues:
            order += q[i:i + 1]
    return order


def _in_proj_ml_kernel(h_ref, cprev_ref, wqk_ref, wv_ref, wo_ref, wif_ref, bif_ref, cw_ref, cb_ref,
                       q_ref, kt_ref, v_ref, so_ref, gt_ref, conv_ref, hb_ref, raw_ref, kc_ref, *, tm):
    hb_ref[...] = h_ref[0].astype(BF16)
    proj, plain, gated = _chunk_jobs(hb_ref)

    @pl.when(pl.program_id(1) == 0)
    def _():
        raw_ref[0:8, :] = jnp.zeros((8, 2 * ML_WIDTH), F32)
        raw_ref[8 - (CONV_W - 1):8, :] = cprev_ref[0]

    def ml_qk(c):
        y, cols = proj(wqk_ref, c)
        raw_ref[8:8 + tm, cols] = y
        is_k = c * COL_CHUNK >= ML_WIDTH
        sub = lax.broadcasted_iota(jnp.int32, (8, COL_CHUNK), 0)
        for r in range(tm // CONV_ROWS):
            base = 8 + r * CONV_ROWS
            blocks = [raw_ref[base + 8 * i:base + 8 * i + 8, cols] for i in range(-1, CONV_ROWS // 8)]
            taps = [jnp.concatenate(blocks[1:], axis=0)]
            for j in range(1, CONV_W):
                taps.append(jnp.concatenate(
                    [pltpu.roll(jnp.where(sub < 8 - j, cur, prev), j, axis=0)
                     for prev, cur in zip(blocks[:-1], blocks[1:])], axis=0))
            acc = cb_ref[:, cols]
            for j in range(CONV_W):
                acc = acc + taps[j] * cw_ref[CONV_W - 1 - j, :, cols]
            qk = acc * _sigmoid(acc)
            rows = slice(r * CONV_ROWS, (r + 1) * CONV_ROWS)
            if is_k:
                kc_ref[rows, :] = qk * (ML_DHEAD ** -0.5)
            else:
                q_ref[0, rows, cols] = qk.astype(BF16)
        if is_k:
            kt_ref[0, cols.start - ML_WIDTH:cols.stop - ML_WIDTH, :] = kc_ref[...].astype(BF16).T

    n_chunks = D_MODEL // COL_CHUNK
    heavy = [functools.partial(ml_qk, c) for c in range(2 * n_chunks)]
    light = [functools.partial(job, c) for c in range(n_chunks)
             for job in (plain(wv_ref, v_ref), gated(wo_ref, so_ref))]
    for job in _interleave(heavy, light):
        job()
    conv_ref[0] = raw_ref[8 + tm - (CONV_W - 1):8 + tm, :]
    raw_ref[0:8, :] = raw_ref[tm:tm + 8, :]

    zg = jnp.dot(hb_ref[...], wif_ref[...], preferred_element_type=F32) + bif_ref[...]
    lane = lax.broadcasted_iota(jnp.int32, zg.shape, 1)
    log_sig = jnp.minimum(zg, 0.0) - jnp.log(1.0 + jnp.exp(-jnp.abs(zg)))
    gt_ref[0] = jnp.where(lane < ML_HEADS, zg, log_sig)


def _in_proj_att_kernel(h_ref, waq_ref, wak_ref, wav_ref, wgm_ref, wga_ref,
                        aq_ref, akt_ref, av_ref, pk_ref, pv_ref, sgm_ref, sga_ref, hb_ref):
    hb_ref[...] = h_ref[0].astype(BF16)
    proj, plain, gated = _chunk_jobs(hb_ref)

    def att_k(c):
        y, cols = proj(wak_ref, c)
        pk_ref[0, :, cols] = y
        akt_ref[0, cols, :] = y.astype(BF16).T

    def att_v(c):
        y, cols = proj(wav_ref, c)
        pv_ref[0, :, cols] = y
        av_ref[0, :, cols] = y.astype(BF16)

    n_chunks = D_MODEL // COL_CHUNK
    heavy = [functools.partial(job, c) for c in range(n_chunks)
             for job in (gated(wgm_ref, sgm_ref), att_k, gated(wga_ref, sga_ref))]
    light = [functools.partial(job, c) for c in range(n_chunks) for job in (plain(waq_ref, aq_ref), att_v)]
    for job in _interleave(heavy, light):
        job()


def _in_proj(h, conv_prev, w):
    b, t, _ = h.shape
    tm = min(512, t)
    nt = t // tm
    keep = min(BAND, t)
    first_kept = (t - keep) // tm
    h_spec = pl.BlockSpec((1, tm, D_MODEL), lambda i, j: (i, j, 0))

    def tok(width, dtype):
        return (pl.BlockSpec((1, tm, width), lambda i, j: (i, j, 0)),
                jax.ShapeDtypeStruct((b, t, width), dtype))

    def kept(width):
        return (pl.BlockSpec((1, tm, width), lambda i, j: (i, jnp.maximum(j - first_kept, 0), 0)),
                jax.ShapeDtypeStruct((b, keep, width), F32))

    conv_spec = pl.BlockSpec((1, CONV_W - 1, 2 * ML_WIDTH), lambda i, j: (i, 0, 0))
    feature_major = (pl.BlockSpec((1, ATT_WIDTH, tm), lambda i, j: (i, 0, j)),
                     jax.ShapeDtypeStruct((b, ATT_WIDTH, t), BF16))
    ml_outs = [
        tok(ML_WIDTH, BF16), feature_major, tok(ML_WIDTH, BF16), tok(ML_WIDTH, BF16),
        tok(GATE_LANES, F32),
        (conv_spec, jax.ShapeDtypeStruct((b, CONV_W - 1, 2 * ML_WIDTH), F32)),
    ]
    ml_weights = [w["wqk"], w["wv"], w["wo"], w["wif"], w["bif"], w["conv_w"], w["conv_b"]]
    ml = pl.pallas_call(
        functools.partial(_in_proj_ml_kernel, tm=tm),
        grid=(b, nt),
        in_specs=[h_spec, conv_spec] + [_const_spec(a.shape) for a in ml_weights],
        out_specs=[o[0] for o in ml_outs],
        out_shape=[o[1] for o in ml_outs],
        scratch_shapes=[pltpu.VMEM((tm, D_MODEL), BF16), pltpu.VMEM((tm + 8, 2 * ML_WIDTH), F32),
                        pltpu.VMEM((tm, COL_CHUNK), F32)],
        compiler_params=_params("parallel", "arbitrary"),
        name="in_proj_ml",
    )(h, conv_prev, *ml_weights)

    att_outs = [
        tok(ATT_WIDTH, BF16),
        feature_major,
        tok(ATT_WIDTH, BF16),
        kept(ATT_WIDTH), kept(ATT_WIDTH),
        tok(D_MODEL, BF16), tok(D_MODEL, BF16),
    ]
    att_weights = [w["waq"], w["wak"], w["wav"], w["wgm"], w["wga"]]
    att = pl.pallas_call(
        _in_proj_att_kernel,
        grid=(b, nt),
        in_specs=[h_spec] + [_const_spec(a.shape) for a in att_weights],
        out_specs=[o[0] for o in att_outs],
        out_shape=[o[1] for o in att_outs],
        scratch_shapes=[pltpu.VMEM((tm, D_MODEL), BF16)],
        compiler_params=_params("parallel", "arbitrary"),
        name="in_proj_att",
    )(h, *att_weights)
    return tuple(ml) + tuple(att)


def _split3(x):
    hi = x.astype(BF16)
    rest = x - hi.astype(F32)
    mid = rest.astype(BF16)
    return hi, mid, (rest - mid.astype(F32)).astype(BF16)


def _mlstm_kernel(q_ref, kt_ref, v_ref, so_ref, gt_ref, c0_ref, n0_ref, m0_ref, ng_ref,
                  hm_ref, c_ref, n_ref, m_ref,
                  ncol_ref, d_ref, s_ref, hh_ref, wi_ref, emt_ref, rs_ref, *, blk):
    heads = range(ML_HEADS)
    rb = min(ROW_BLOCK, blk)
    row_blocks = [slice(r * rb, (r + 1) * rb) for r in range(blk // rb)]
    head_cols = [slice(j * ML_DHEAD, (j + 1) * ML_DHEAD) for j in heads]

    @pl.when(pl.program_id(1) == 0)
    def _():
        c_ref[...] = c0_ref[...]
        ncol_ref[...] = n0_ref[0]
        m_ref[...] = m0_ref[...]

    gates = gt_ref[0]
    tril = (lax.broadcasted_iota(jnp.int32, (blk, blk), 1)
            <= lax.broadcasted_iota(jnp.int32, (blk, blk), 0)).astype(BF16)
    csum = sum(jnp.dot(tril, part, preferred_element_type=F32) for part in _split3(gates))
    bcum = pltpu.roll(csum, LANES - ML_HEADS, axis=1)
    li_rel = gates - bcum
    li_rel_t = li_rel.T
    m_prev = m_ref[0]
    b_last = bcum[blk - 1:blk, :]
    g_s = b_last + li_rel
    m_new = jnp.maximum(b_last + m_prev, jnp.max(g_s, axis=0, keepdims=True))
    decay = jnp.exp(b_last + m_prev - m_new)
    w_s_t = jnp.exp(g_s - m_new).T
    inter = bcum + m_prev

    def lane_of(x, j):
        lane = lax.broadcasted_iota(jnp.int32, x.shape, 1)
        return jnp.sum(jnp.where(lane == j, x, 0.0), axis=-1, keepdims=True)

    def spread(x, width):
        return x[:, :width] if width <= LANES else jnp.concatenate([x] * (width // LANES), axis=1)

    units = [(j, rows) for j in heads for rows in row_blocks]

    for j, rows in units:
        wi_ref[j, rows, :] = jnp.broadcast_to(lane_of(bcum[rows, :], j), (rb, LANES))
    m_prev_j = [lane_of(m_prev, j) for j in heads]
    for j, rows in units:
        causal = (lax.broadcasted_iota(jnp.int32, (rb, blk), 1)
                  <= lax.broadcasted_iota(jnp.int32, (rb, blk), 0) + rows.start)
        bcol = wi_ref[j, rows, :]
        dmat = jnp.where(causal, spread(bcol, blk) + li_rel_t[j:j + 1, :], -jnp.inf)
        icol = bcol + m_prev_j[j]
        m_t = jnp.maximum(icol, jnp.max(dmat, axis=-1, keepdims=True))
        d_ref[j, rows, :] = jnp.exp(dmat - spread(m_t, blk))
        wi_ref[j, rows, :] = jnp.exp(icol - m_t)
        emt_ref[j, rows, :] = jnp.exp(-m_t)

    for j in heads:
        qk = jnp.dot(q_ref[0, :, head_cols[j]], kt_ref[0, head_cols[j], :], preferred_element_type=F32)
        for rows in row_blocks:
            s = qk[rows, :] * d_ref[j, rows, :]
            rs_ref[j, rows, :] = jnp.broadcast_to(jnp.sum(s, axis=-1, keepdims=True), (rb, LANES))
            s_ref[j, rows, :] = s.astype(BF16)

    for j in heads:
        qj, vj = q_ref[0, :, head_cols[j]], v_ref[0, :, head_cols[j]]
        q_c = jnp.dot(qj, c_ref[0, j].astype(BF16), preferred_element_type=F32)
        s_v = jnp.dot(s_ref[j], vj, preferred_element_type=F32)
        q_n = jnp.dot(qj, ncol_ref[j].astype(BF16), preferred_element_type=F32)
        for rows in row_blocks:
            w_inter = wi_ref[j, rows, :]
            den = w_inter * q_n[rows, :] + rs_ref[j, rows, :]
            scale = 1.0 / jnp.maximum(jnp.abs(den), emt_ref[j, rows, :])
            hh = (spread(w_inter, ML_DHEAD) * q_c[rows, :] + s_v[rows, :]) * spread(scale, ML_DHEAD)
            hh_ref[j, rows, :] = hh
            wi_ref[j, rows, :] = jnp.broadcast_to(jnp.mean(hh, axis=-1, keepdims=True), (rb, LANES))
    for j, rows in units:
        hc = hh_ref[j, rows, :] - spread(wi_ref[j, rows, :], ML_DHEAD)
        var = jnp.mean(hc * hc, axis=-1, keepdims=True)
        hn = hc * lax.rsqrt(var + LN_EPS) * ng_ref[:, head_cols[j]] * so_ref[0, rows, head_cols[j]].astype(F32)
        hm_ref[0, rows, head_cols[j]] = hn.astype(BF16)

    ones = jnp.ones((blk, LANES), BF16)
    for j in heads:
        kw = (kt_ref[0, head_cols[j], :].astype(F32) * w_s_t[j:j + 1, :]).astype(BF16)
        d_j = lane_of(decay, j)
        c_ref[0, j] = d_j * c_ref[0, j] + jnp.dot(kw, v_ref[0, :, head_cols[j]], preferred_element_type=F32)
        ncol_ref[j] = d_j * ncol_ref[j] + jnp.dot(kw, ones, preferred_element_type=F32)
    m_ref[0] = m_new

    @pl.when(pl.program_id(1) == pl.num_programs(1) - 1)
    def _():
        for j in heads:
            n_ref[0, j:j + 1, :] = ncol_ref[j].T[0:1, :]


def _mlstm(q, kt, v, so, gates, c0, n0, m0, norm_g):
    b, t, _ = q.shape
    blk = min(256, t)
    tok = pl.BlockSpec((1, blk, ML_WIDTH), lambda i, j: (i, j, 0))
    c_spec = pl.BlockSpec((1, ML_HEADS, ML_DHEAD, ML_DHEAD), lambda i, j: (i, 0, 0, 0))
    n_spec = pl.BlockSpec((1, ML_HEADS, ML_DHEAD), lambda i, j: (i, 0, 0))
    m_spec = pl.BlockSpec((1, 1, LANES), lambda i, j: (i, 0, 0))
    per_head = lambda width, dtype: pltpu.VMEM((ML_HEADS, blk, width), dtype)
    return pl.pallas_call(
        functools.partial(_mlstm_kernel, blk=blk),
        grid=(b, t // blk),
        in_specs=[tok, pl.BlockSpec((1, ML_WIDTH, blk), lambda i, j: (i, 0, j)), tok, tok,
                  pl.BlockSpec((1, blk, GATE_LANES), lambda i, j: (i, j, 0)),
                  c_spec, pl.BlockSpec((1, ML_HEADS, ML_DHEAD, LANES), lambda i, j: (i, 0, 0, 0)), m_spec,
                  _const_spec(norm_g.shape)],
        out_specs=[tok, c_spec, n_spec, m_spec],
        out_shape=[jax.ShapeDtypeStruct((b, t, ML_WIDTH), BF16),
                   jax.ShapeDtypeStruct(c0.shape, F32),
                   jax.ShapeDtypeStruct((b, ML_HEADS, ML_DHEAD), F32),
                   jax.ShapeDtypeStruct(m0.shape, F32)],
        scratch_shapes=[pltpu.VMEM((ML_HEADS, ML_DHEAD, LANES), F32),
                        per_head(blk, F32), per_head(blk, BF16), per_head(ML_DHEAD, F32),
                        per_head(LANES, F32), per_head(LANES, F32), per_head(LANES, F32)],
        compiler_params=_params("parallel", "arbitrary"),
        name="mlstm",
    )(q, kt, v, so, gates, c0, n0, m0, norm_g)


KEY_BLOCK = 256
N_KEY_BLOCKS = 3
N_STAGE = 4
ROW_BLOCK = 64


def _band_attn_kernel(q_ref, kt0_ref, kt1_ref, kt2_ref, v0_ref, v1_ref, v2_ref, bias_ref, o_ref,
                      s_ref, p_ref, mx_ref, linv_ref, even_ref, *, lead_blocks):
    kt_refs = (kt0_ref, kt1_ref, kt2_ref)
    v_refs = (v0_ref, v1_ref, v2_ref)
    qb = q_ref.shape[1]
    lane = lax.broadcasted_iota(jnp.int32, (qb, LANES), 1)
    low = lane < ATT_DHEAD

    def attend(first_slot):
        slots = range(first_slot, N_KEY_BLOCKS)
        k_lo = first_slot * KEY_BLOCK

        def scores(head):
            sl = slice(head // 2 * LANES, (head // 2 + 1) * LANES)
            q2 = q_ref[0, :, sl].astype(F32)
            qh = jnp.where(low if head % 2 == 0 else jnp.logical_not(low), q2, 0.0).astype(BF16)
            row_max = None
            for c in slots:
                cols = slice(c * KEY_BLOCK, (c + 1) * KEY_BLOCK)
                s = jnp.dot(qh, kt_refs[c][0, sl, :], preferred_element_type=F32) + bias_ref[head, :, cols]
                s_ref[head % N_STAGE, :, cols] = s
                part = jnp.maximum(s[:, :LANES], s[:, LANES:])
                row_max = part if row_max is None else jnp.maximum(row_max, part)
            mx_ref[head % N_STAGE] = row_max

        def weighted_values(head):
            sl = slice(head // 2 * LANES, (head // 2 + 1) * LANES)
            buf = head % N_STAGE
            rb = min(ROW_BLOCK, qb)
            for r in range(qb // rb):
                rows = slice(r * rb, (r + 1) * rb)
                m = jnp.max(mx_ref[buf, rows, :], axis=-1, keepdims=True)
                row_sum = None
                for c in slots:
                    cols = slice(c * KEY_BLOCK, (c + 1) * KEY_BLOCK)
                    p = jnp.exp(s_ref[buf, rows, cols] - m)
                    p_ref[buf, rows, cols] = p.astype(BF16)
                    part = p[:, :LANES] + p[:, LANES:]
                    row_sum = part if row_sum is None else row_sum + part
                l = jnp.sum(row_sum, axis=-1, keepdims=True)
                linv_ref[buf, rows, :] = jnp.broadcast_to(1.0 / l, (rb, LANES))
            v2 = jnp.concatenate([v_refs[c][0, :, sl] for c in slots], axis=0)
            return jnp.dot(p_ref[buf, :, k_lo:], v2, preferred_element_type=F32) * linv_ref[buf]

        scores(0)
        for head in range(ATT_HEADS):
            if head + 1 < ATT_HEADS:
                scores(head + 1)
            out = weighted_values(head)
            if head % 2 == 0:
                even_ref[...] = out
            else:
                sl = slice(head // 2 * LANES, (head // 2 + 1) * LANES)
                o_ref[0, :, sl] = jnp.where(low, even_ref[...], out).astype(BF16)

    if lead_blocks == 0:
        attend(0)
    else:
        j = pl.program_id(1)
        for missing in range(lead_blocks, 0, -1):
            pl.when(j == lead_blocks - missing)(functools.partial(attend, missing))
        pl.when(j >= lead_blocks)(functools.partial(attend, 0))


def _band_bias(table, qb):
    nkeys = N_KEY_BLOCKS * KEY_BLOCK
    heads = table.shape[0]
    period = nkeys + qb
    shift = np.arange(period)
    shift = np.where(shift < nkeys, shift, shift - period)
    rel_idx = np.clip(nkeys - qb - shift, -MAX_REL, MAX_REL) + MAX_REL
    diag = table[:, rel_idx].astype(F32)
    toep = jnp.tile(diag, (1, qb))[:, :qb * (period - 1)].reshape(heads, qb, period - 1)[:, :, :nkeys]
    qi = np.arange(qb)
    chunk_start = (qi // CHUNK) * CHUNK
    kpos = np.arange(nkeys) - (nkeys - qb)
    in_band = (kpos[None, :] >= chunk_start[:, None] - BAND) & (kpos[None, :] < chunk_start[:, None] + CHUNK)
    return jnp.where(jnp.asarray(in_band)[None], toep, NEG_INF)


def _band_attn(q, kt, v, bias, *, streaming):
    b, tq, _ = q.shape
    qb = bias.shape[1]
    lead = N_KEY_BLOCKS - qb // KEY_BLOCK if streaming else 0
    if streaming:
        def kidx(s):
            return lambda i, j: jnp.maximum(j - lead + s, 0)
    else:
        def kidx(s):
            return lambda i, j: s
    kt_specs = [pl.BlockSpec((1, ATT_WIDTH, KEY_BLOCK), (lambda f: lambda i, j: (i, 0, f(i, j)))(kidx(s)))
                for s in range(N_KEY_BLOCKS)]
    v_specs = [pl.BlockSpec((1, KEY_BLOCK, ATT_WIDTH), (lambda f: lambda i, j: (i, f(i, j), 0))(kidx(s)))
               for s in range(N_KEY_BLOCKS)]
    tok = pl.BlockSpec((1, qb, ATT_WIDTH), lambda i, j: (i, j, 0))
    return pl.pallas_call(
        functools.partial(_band_attn_kernel, lead_blocks=lead),
        grid=(b, tq // qb),
        in_specs=[tok] + kt_specs + v_specs + [_const_spec(bias.shape)],
        out_specs=tok,
        out_shape=jax.ShapeDtypeStruct((b, tq, ATT_WIDTH), BF16),
        scratch_shapes=[pltpu.VMEM((N_STAGE, qb, N_KEY_BLOCKS * KEY_BLOCK), F32),
                        pltpu.VMEM((N_STAGE, qb, N_KEY_BLOCKS * KEY_BLOCK), BF16),
                        pltpu.VMEM((N_STAGE, qb, LANES), F32),
                        pltpu.VMEM((N_STAGE, qb, LANES), F32),
                        pltpu.VMEM((qb, LANES), F32)],
        compiler_params=_params("parallel", "arbitrary"),
        name="band_attn",
    )(q, kt, kt, kt, v, v, v, bias)


def _mix_ln_kernel(h_ref, hm_ref, oa_ref, sgm_ref, sga_ref, wm_ref, wa_ref, wout_ref, g_ref, b_ref, o_ref):
    y_ml = jnp.dot(hm_ref[...], wm_ref[...], preferred_element_type=F32)
    y_att = jnp.dot(oa_ref[...], wa_ref[...], preferred_element_type=F32)
    merged = sgm_ref[...].astype(F32) * y_ml + sga_ref[...].astype(F32) * y_att
    mix = jnp.dot(merged.astype(BF16), wout_ref[...], preferred_element_type=F32)
    o_ref[...] = _layer_norm(ALPHA * h_ref[...] + mix, g_ref[...], b_ref[...])


def _mix_ln(h, hm, oa, sgm, sga, wm, wa, wout, g, b):
    n = h.shape[0]
    tm = min(512, n)
    row = pl.BlockSpec((tm, D_MODEL), lambda i: (i, 0))
    consts = [wm, wa, wout, g, b]
    return pl.pallas_call(
        _mix_ln_kernel,
        grid=(n // tm,),
        in_specs=[row] * 5 + [_const_spec(a.shape) for a in consts],
        out_specs=row,
        out_shape=jax.ShapeDtypeStruct((n, D_MODEL), F32),
        compiler_params=_params("parallel"),
        name="mix_ln",
    )(h, hm, oa, sgm, sga, *consts)


def _prep_weights(p):
    w_in = p["w_in"]

    def cols(start, width):
        return w_in[:, start:start + width]

    pad = jnp.zeros((D_MODEL, GATE_LANES - 2 * ML_HEADS), F32)
    bif = jnp.concatenate([p["b_ml_i"], p["b_ml_f"], jnp.zeros((GATE_LANES - 2 * ML_HEADS,), F32)])
    row = lambda a: a.reshape(1, -1).astype(F32)
    return {
        "wqk": cols(_O_MLQ, 2 * ML_WIDTH).astype(BF16),
        "wv": cols(_O_MLV, ML_WIDTH).astype(BF16),
        "wo": cols(_O_MLO, ML_WIDTH).astype(BF16),
        "wif": jnp.concatenate([cols(_O_MLI, 2 * ML_HEADS), pad], axis=1).astype(BF16),
        "bif": row(bif),
        "conv_w": jnp.broadcast_to(p["ml_conv_w"].astype(F32)[:, None, :], (CONV_W, CONV_ROWS, 2 * ML_WIDTH)),
        "conv_b": jnp.broadcast_to(row(p["ml_conv_b"]), (CONV_ROWS, 2 * ML_WIDTH)),
        "waq": (cols(_O_AQ, ATT_WIDTH) * (ATT_DHEAD ** -0.5)).astype(BF16),
        "wak": cols(_O_AK, ATT_WIDTH).astype(BF16),
        "wav": cols(_O_AV, ATT_WIDTH).astype(BF16),
        "wgm": cols(_O_GM, D_MODEL).astype(BF16),
        "wga": cols(_O_GA, D_MODEL).astype(BF16),
        "norm_g": row(p["ml_norm_g"]),
        "wm": p["w_ml_proj"].astype(BF16),
        "wa": p["w_att_proj"].astype(BF16),
        "wout": p["w_out"].astype(BF16),
        "ffn1": (p["ffn1_w_gu"][:, :D_FF].astype(BF16), p["ffn1_w_gu"][:, D_FF:].astype(BF16),
                 p["ffn1_w_down"].astype(BF16)),
        "ffn2": (p["ffn2_w_gu"][:, :D_FF].astype(BF16), p["ffn2_w_gu"][:, D_FF:].astype(BF16),
                 p["ffn2_w_down"].astype(BF16)),
        "ln1": (row(p["ln1_g"]), row(p["ln1_b"])),
        "ln2": (row(p["ln2_g"]), row(p["ln2_b"])),
        "ln3": (row(p["ln3_g"]), row(p["ln3_b"])),
        "rel_bias": p["att_rel_bias"],
    }


def _encoder_layer(x, w, conv_prev, ml_state, att_past):
    b, t, _ = x.shape
    n = b * t
    flat = lambda a: a.reshape(n, a.shape[-1])
    h1 = _ffn_ln(flat(x), *w["ffn1"], *w["ln1"])
    (q, k, v, so, gates, new_conv, aq, akt, av, k_rows, v_rows, sgm, sga) = _in_proj(
        h1.reshape(b, t, D_MODEL), conv_prev, w)

    c0, n0, m0 = ml_state
    m0p = jnp.pad(m0.astype(F32), ((0, 0), (0, LANES - ML_HEADS))).reshape(b, 1, LANES)
    n0_rep = jnp.broadcast_to(n0.astype(F32)[..., None], n0.shape + (LANES,))
    hm, c1, n1, m1p = _mlstm(q, k, v, so, gates, c0.astype(F32), n0_rep, m0p, w["norm_g"])
    m1 = m1p[:, 0, :ML_HEADS]

    if att_past is None:
        bias = _band_bias(w["rel_bias"], KEY_BLOCK)
        oa = _band_attn(aq, akt, av, bias, streaming=True)
    else:
        k_past, v_past = att_past
        n_past = k_past.shape[1]
        nkeys = N_KEY_BLOCKS * KEY_BLOCK
        fill = nkeys - n_past - t
        kt_past = k_past.reshape(b, n_past, ATT_WIDTH).transpose(0, 2, 1).astype(BF16)
        kt_all = jnp.concatenate([jnp.zeros((b, ATT_WIDTH, fill), BF16), kt_past, akt], axis=2)
        v_all = jnp.concatenate([jnp.zeros((b, fill, ATT_WIDTH), BF16),
                                 v_past.reshape(b, n_past, ATT_WIDTH).astype(BF16), av], axis=1)
        bias = _band_bias(w["rel_bias"], t)
        oa = _band_attn(aq, kt_all, v_all, bias, streaming=False)

    h2 = _mix_ln(h1, flat(hm), flat(oa), flat(sgm), flat(sga), w["wm"], w["wa"], w["wout"], *w["ln2"])
    y = _ffn_ln(h2, *w["ffn2"], *w["ln3"]).reshape(b, t, D_MODEL)
    keep = k_rows.shape[1]
    state = (new_conv, c1, n1, m1,
             k_rows.reshape(b, keep, ATT_HEADS, ATT_DHEAD), v_rows.reshape(b, keep, ATT_HEADS, ATT_DHEAD))
    return y, state


def kernel(x_prompt, x_sample, state_ml_conv, state_ml_C, state_ml_n, state_ml_m, cache_att_k, cache_att_v,
           w_in, b_ml_i, b_ml_f, ml_conv_w, ml_conv_b, ml_norm_g, att_rel_bias, w_ml_proj, w_att_proj, w_out,
           ffn1_w_gu, ffn1_w_down, ffn2_w_gu, ffn2_w_down, ln1_g, ln1_b, ln2_g, ln2_b, ln3_g, ln3_b):
    depth = w_in.shape[0]
    bp = x_prompt.shape[0]
    y_p, y_s = x_prompt, x_sample
    p_states, s_states = [], []
    for l in range(depth):
        w = _prep_weights({
            "w_in": w_in[l], "b_ml_i": b_ml_i[l], "b_ml_f": b_ml_f[l], "ml_conv_w": ml_conv_w[l],
            "ml_conv_b": ml_conv_b[l], "ml_norm_g": ml_norm_g[l], "att_rel_bias": att_rel_bias[l],
            "w_ml_proj": w_ml_proj[l], "w_att_proj": w_att_proj[l], "w_out": w_out[l],
            "ffn1_w_gu": ffn1_w_gu[l], "ffn1_w_down": ffn1_w_down[l], "ffn2_w_gu": ffn2_w_gu[l],
            "ffn2_w_down": ffn2_w_down[l], "ln1_g": ln1_g[l], "ln1_b": ln1_b[l], "ln2_g": ln2_g[l],
            "ln2_b": ln2_b[l], "ln3_g": ln3_g[l], "ln3_b": ln3_b[l]})
        conv0 = jnp.zeros((bp, CONV_W - 1, 2 * ML_WIDTH), F32)
        ml0 = (jnp.zeros((bp, ML_HEADS, ML_DHEAD, ML_DHEAD), F32),
               jnp.zeros((bp, ML_HEADS, ML_DHEAD), F32),
               jnp.zeros((bp, ML_HEADS), F32))
        y_p, st_p = _encoder_layer(y_p, w, conv0, ml0, None)
        y_s, st_s = _encoder_layer(y_s, w, state_ml_conv[l],
                                   (state_ml_C[l], state_ml_n[l], state_ml_m[l]),
                                   (cache_att_k[l], cache_att_v[l]))
        p_states.append(st_p)
        s_states.append(st_s)
    p_conv, p_c, p_n, p_m, p_k, p_v = [jnp.stack(s) for s in zip(*p_states)]
    s_conv, s_c, s_n, s_m, s_k, s_v = [jnp.stack(s) for s in zip(*s_states)]
    return (y_p, y_s, p_conv, s_conv, p_c, s_c, p_n, s_n, p_m, s_m, p_k, s_k, p_v, s_v)
```

```python
import functools

import jax
import jax.numpy as jnp
import numpy as np
from jax import lax
from jax.experimental import pallas as pl
from jax.experimental.pallas import tpu as pltpu

F32 = jnp.float32
BF16 = jnp.bfloat16

D_MODEL = 1024
CHUNK = 64
ML_HEADS = 4
ML_DHEAD = D_MODEL // ML_HEADS
ML_WIDTH = ML_HEADS * ML_DHEAD
CONV_W = 4
ATT_HEADS = 16
ATT_DHEAD = D_MODEL // ATT_HEADS
ATT_WIDTH = ATT_HEADS * ATT_DHEAD
BAND_CHUNKS = 8
BAND = BAND_CHUNKS * CHUNK
MAX_REL = 128
D_FF = ((8 * D_MODEL) // 3 + 127) // 128 * 128
ALPHA = 2.0 ** 0.25
LN_EPS = 1e-5
NEG_INF = -1e30
LOG2_E = 1.4426950408889634

LANES = 128
GATE_LANES = LANES
VMEM_LIMIT = 56 * 1024 * 1024

_O_MLI = 4 * ML_WIDTH
_O_AQ = _O_MLI + 2 * ML_HEADS
_W_V, _W_O, _W_AQ, _W_AK, _W_AV, _W_GM, _W_GA, _W_IF = 2, 3, 4, 5, 6, 7, 8, 9


def _const_spec(shape, col_block=0):
    index = (0,) * (len(shape) - 1) + (col_block,)
    return pl.BlockSpec(shape, lambda *_: index, pipeline_mode=pl.Buffered(1))


def _params(*sem):
    return pltpu.CompilerParams(dimension_semantics=sem, vmem_limit_bytes=VMEM_LIMIT)


def _layer_norm(y, g, b):
    mu = jnp.mean(y, axis=-1, keepdims=True)
    yc = y - mu
    var = jnp.mean(yc * yc, axis=-1, keepdims=True)
    return yc * lax.rsqrt(var + LN_EPS) * g + b


def _sigmoid(x):
    return 1.0 / (1.0 + jnp.exp(-x))


_FF_CHUNKS = ((0, 768), (768, 1792), (1792, D_FF))


def _ffn_ln_kernel(x_ref, wg_ref, wu_ref, wd_ref, g_ref, b_ref, o_ref):
    x = x_ref[...]
    xb = x.astype(BF16)
    acc = None
    for s, e in _FF_CHUNKS:
        gate = jnp.dot(xb, wg_ref[:, s:e], preferred_element_type=F32)
        up = jnp.dot(xb, wu_ref[:, s:e], preferred_element_type=F32)
        hid = (gate * _sigmoid(gate) * up).astype(BF16)
        part = jnp.dot(hid, wd_ref[s:e, :], preferred_element_type=F32)
        acc = part if acc is None else acc + part
    o_ref[...] = _layer_norm(ALPHA * x + 0.5 * acc, g_ref[...], b_ref[...])


def _ffn_ln(x, wgu, wd, g, b):
    n = x.shape[0]
    tm = min(512, n)
    row = pl.BlockSpec((tm, D_MODEL), lambda i: (i, 0))
    return pl.pallas_call(
        _ffn_ln_kernel,
        grid=(n // tm,),
        in_specs=[row, _const_spec((D_MODEL, D_FF), 0), _const_spec((D_MODEL, D_FF), 1), _const_spec(wd.shape),
                  _const_spec(g.shape), _const_spec(b.shape)],
        out_specs=row,
        out_shape=jax.ShapeDtypeStruct((n, D_MODEL), F32),
        compiler_params=_params("parallel"),
        name="ffn_ln",
    )(x, wgu, wgu, wd, g, b)


COL_CHUNK = 256
CONV_ROWS = 16


def _chunk_jobs(hb_ref):
    def proj(w_ref, c):
        cols = slice(c * COL_CHUNK, (c + 1) * COL_CHUNK)
        return jnp.dot(hb_ref[...], w_ref[:, cols], preferred_element_type=F32), cols

    def plain(w_ref, o_ref):
        def job(c):
            y, cols = proj(w_ref, c)
            o_ref[0, :, cols] = y.astype(BF16)
        return job

    def gated(w_ref, o_ref):
        def job(c):
            y, cols = proj(w_ref, c)
            o_ref[0, :, cols] = _sigmoid(y).astype(BF16)
        return job

    return proj, plain, gated


def _interleave(*queues):
    order = []
    for i in range(max(len(q) for q in queues)):
        for q in queues:
            order += q[i:i + 1]
    return order


def _in_proj_ml_kernel(h_ref, *refs, tm, fresh):
    cprev_ref = None if fresh else refs[0]
    (wqk_ref, wv_ref, wo_ref, wif_ref, bif_ref, cw_ref, cb_ref,
     q_ref, kt_ref, v_ref, so_ref, gt_ref, conv_ref, hb_ref, raw_ref, kc_ref) = refs[0 if fresh else 1:]
    hb_ref[...] = h_ref[0].astype(BF16)
    proj, plain, gated = _chunk_jobs(hb_ref)

    @pl.when(pl.program_id(1) == 0)
    def _():
        raw_ref[0:8, :] = jnp.zeros((8, 2 * ML_WIDTH), F32)
        if not fresh:
            raw_ref[8 - (CONV_W - 1):8, :] = cprev_ref[0]

    def ml_qk(c):
        y, cols = proj(wqk_ref, c)
        raw_ref[8:8 + tm, cols] = y
        is_k = c * COL_CHUNK >= ML_WIDTH
        sub = lax.broadcasted_iota(jnp.int32, (8, COL_CHUNK), 0)
        for r in range(tm // CONV_ROWS):
            base = 8 + r * CONV_ROWS
            blocks = [raw_ref[base + 8 * i:base + 8 * i + 8, cols] for i in range(-1, CONV_ROWS // 8)]
            taps = [jnp.concatenate(blocks[1:], axis=0)]
            for j in range(1, CONV_W):
                taps.append(jnp.concatenate(
                    [pltpu.roll(jnp.where(sub < 8 - j, cur, prev), j, axis=0)
                     for prev, cur in zip(blocks[:-1], blocks[1:])], axis=0))
            acc = cb_ref[:, cols]
            for j in range(CONV_W):
                acc = acc + taps[j] * cw_ref[CONV_W - 1 - j, :, cols]
            qk = acc * _sigmoid(acc)
            rows = slice(r * CONV_ROWS, (r + 1) * CONV_ROWS)
            if is_k:
                kc_ref[rows, :] = qk * (ML_DHEAD ** -0.5)
            else:
                q_ref[0, rows, cols] = qk.astype(BF16)
        if is_k:
            kt_ref[0, cols.start - ML_WIDTH:cols.stop - ML_WIDTH, :] = kc_ref[...].astype(BF16).T

    n_chunks = D_MODEL // COL_CHUNK
    heavy = [functools.partial(ml_qk, c) for c in range(2 * n_chunks)]
    light = [functools.partial(job, c) for c in range(n_chunks)
             for job in (plain(wv_ref, v_ref), gated(wo_ref, so_ref))]
    for job in _interleave(heavy, light):
        job()
    conv_ref[0] = raw_ref[8 + tm - (CONV_W - 1):8 + tm, :]
    raw_ref[0:8, :] = raw_ref[tm:tm + 8, :]

    zg = jnp.dot(hb_ref[...], wif_ref[...], preferred_element_type=F32) + bif_ref[...]
    lane = lax.broadcasted_iota(jnp.int32, zg.shape, 1)
    log_sig = jnp.minimum(zg, 0.0) - jnp.log(1.0 + jnp.exp(-jnp.abs(zg)))
    gt_ref[0] = jnp.where(lane < ML_HEADS, zg, log_sig)


def _in_proj_att_kernel(h_ref, waq_ref, wak_ref, wav_ref, wgm_ref, wga_ref,
                        aq_ref, akt_ref, av_ref, pk_ref, pv_ref, sgm_ref, sga_ref, hb_ref):
    hb_ref[...] = h_ref[0].astype(BF16)
    proj, plain, gated = _chunk_jobs(hb_ref)

    def att_k(c):
        y, cols = proj(wak_ref, c)
        pk_ref[0, :, cols] = y
        akt_ref[0, cols, :] = y.astype(BF16).T

    def att_v(c):
        y, cols = proj(wav_ref, c)
        pv_ref[0, :, cols] = y
        av_ref[0, :, cols] = y.astype(BF16)

    n_chunks = D_MODEL // COL_CHUNK
    heavy = [functools.partial(job, c) for c in range(n_chunks)
             for job in (gated(wgm_ref, sgm_ref), att_k, gated(wga_ref, sga_ref))]
    light = [functools.partial(job, c) for c in range(n_chunks) for job in (plain(waq_ref, aq_ref), att_v)]
    for job in _interleave(heavy, light):
        job()


def _in_proj(h, conv_prev, w):
    b, t, _ = h.shape
    tm = min(512, t)
    nt = t // tm
    keep = min(BAND, t)
    first_kept = (t - keep) // tm
    h_spec = pl.BlockSpec((1, tm, D_MODEL), lambda i, j: (i, j, 0))

    def tok(width, dtype):
        return (pl.BlockSpec((1, tm, width), lambda i, j: (i, j, 0)),
                jax.ShapeDtypeStruct((b, t, width), dtype))

    def kept(width):
        return (pl.BlockSpec((1, tm, width), lambda i, j: (i, jnp.maximum(j - first_kept, 0), 0)),
                jax.ShapeDtypeStruct((b, keep, width), F32))

    conv_spec = pl.BlockSpec((1, CONV_W - 1, 2 * ML_WIDTH), lambda i, j: (i, 0, 0))
    feature_major = (pl.BlockSpec((1, ATT_WIDTH, tm), lambda i, j: (i, 0, j)),
                     jax.ShapeDtypeStruct((b, ATT_WIDTH, t), BF16))
    ml_outs = [
        tok(ML_WIDTH, BF16), feature_major, tok(ML_WIDTH, BF16), tok(ML_WIDTH, BF16),
        tok(GATE_LANES, F32),
        (conv_spec, jax.ShapeDtypeStruct((b, CONV_W - 1, 2 * ML_WIDTH), F32)),
    ]
    w_in = w["w_in"]
    square = (D_MODEL, D_MODEL)
    ml_weights = [w_in, w_in, w_in, w_in, w["bif"], w["conv_w"], w["conv_b"]]
    fresh = conv_prev is None
    carried = [] if fresh else [conv_prev]
    ml = pl.pallas_call(
        functools.partial(_in_proj_ml_kernel, tm=tm, fresh=fresh),
        grid=(b, nt),
        in_specs=[h_spec] + [conv_spec] * len(carried)
                 + [_const_spec((D_MODEL, 2 * ML_WIDTH), 0), _const_spec(square, _W_V), _const_spec(square, _W_O),
                    _const_spec((D_MODEL, GATE_LANES), _W_IF * (D_MODEL // GATE_LANES))]
                 + [_const_spec(a.shape) for a in ml_weights[4:]],
        out_specs=[o[0] for o in ml_outs],
        out_shape=[o[1] for o in ml_outs],
        scratch_shapes=[pltpu.VMEM((tm, D_MODEL), BF16), pltpu.VMEM((tm + 8, 2 * ML_WIDTH), F32),
                        pltpu.VMEM((tm, COL_CHUNK), F32)],
        compiler_params=_params("parallel", "arbitrary"),
        name="in_proj_ml",
    )(h, *carried, *ml_weights)

    att_outs = [
        tok(ATT_WIDTH, BF16),
        feature_major,
        tok(ATT_WIDTH, BF16),
        kept(ATT_WIDTH), kept(ATT_WIDTH),
        tok(D_MODEL, BF16), tok(D_MODEL, BF16),
    ]
    att_weights = [w_in] * 5
    att = pl.pallas_call(
        _in_proj_att_kernel,
        grid=(b, nt),
        in_specs=[h_spec] + [_const_spec(square, c) for c in (_W_AQ, _W_AK, _W_AV, _W_GM, _W_GA)],
        out_specs=[o[0] for o in att_outs],
        out_shape=[o[1] for o in att_outs],
        scratch_shapes=[pltpu.VMEM((tm, D_MODEL), BF16)],
        compiler_params=_params("parallel", "arbitrary"),
        name="in_proj_att",
    )(h, *att_weights)
    return tuple(ml) + tuple(att)


def _split3(x):
    hi = x.astype(BF16)
    rest = x - hi.astype(F32)
    mid = rest.astype(BF16)
    return hi, mid, (rest - mid.astype(F32)).astype(BF16)


def _mlstm_kernel(q_ref, kt_ref, v_ref, so_ref, gt_ref, *refs, blk, fresh):
    c0_ref, n0_ref, m0_ref = (None, None, None) if fresh else refs[:3]
    (ng_ref, hm_ref, c_ref, n_ref, m_ref,
     ncol_ref, d_ref, s_ref, hh_ref, wi_ref, emt_ref, rs_ref) = refs[0 if fresh else 3:]
    heads = range(ML_HEADS)
    rb = min(ROW_BLOCK, blk)
    row_blocks = [slice(r * rb, (r + 1) * rb) for r in range(blk // rb)]
    head_cols = [slice(j * ML_DHEAD, (j + 1) * ML_DHEAD) for j in heads]

    @pl.when(pl.program_id(1) == 0)
    def _():
        if fresh:
            c_ref[...] = jnp.zeros_like(c_ref)
            ncol_ref[...] = jnp.zeros_like(ncol_ref)
            m_ref[...] = jnp.zeros_like(m_ref)
        else:
            c_ref[...] = c0_ref[...]
            ncol_ref[...] = n0_ref[0]
            m_ref[...] = m0_ref[...]

    gates = gt_ref[0]
    tril = (lax.broadcasted_iota(jnp.int32, (blk, blk), 1)
            <= lax.broadcasted_iota(jnp.int32, (blk, blk), 0)).astype(BF16)
    csum = sum(jnp.dot(tril, part, preferred_element_type=F32) for part in _split3(gates))
    bcum = pltpu.roll(csum, LANES - ML_HEADS, axis=1)
    li_rel = gates - bcum
    li_rel_t = li_rel.T
    m_prev = m_ref[0]
    b_last = bcum[blk - 1:blk, :]
    g_s = b_last + li_rel
    m_new = jnp.maximum(b_last + m_prev, jnp.max(g_s, axis=0, keepdims=True))
    decay = jnp.exp(b_last + m_prev - m_new)
    w_s_t = jnp.exp(g_s - m_new).T
    inter = bcum + m_prev

    def lane_of(x, j):
        lane = lax.broadcasted_iota(jnp.int32, x.shape, 1)
        return jnp.sum(jnp.where(lane == j, x, 0.0), axis=-1, keepdims=True)

    def spread(x, width):
        return x[:, :width] if width <= LANES else jnp.concatenate([x] * (width // LANES), axis=1)

    units = [(j, rows) for j in heads for rows in row_blocks]

    for j, rows in units:
        wi_ref[j, rows, :] = jnp.broadcast_to(lane_of(bcum[rows, :], j), (rb, LANES))
    m_prev_j = [lane_of(m_prev, j) for j in heads]
    for j, rows in units:
        causal = (lax.broadcasted_iota(jnp.int32, (rb, blk), 1)
                  <= lax.broadcasted_iota(jnp.int32, (rb, blk), 0) + rows.start)
        bcol = wi_ref[j, rows, :]
        dmat = jnp.where(causal, spread(bcol, blk) + li_rel_t[j:j + 1, :], -jnp.inf)
        icol = bcol + m_prev_j[j]
        m_t = jnp.maximum(icol, jnp.max(dmat, axis=-1, keepdims=True))
        d_ref[j, rows, :] = jnp.exp(dmat - spread(m_t, blk))
        wi_ref[j, rows, :] = jnp.exp(icol - m_t)
        emt_ref[j, rows, :] = jnp.exp(-m_t)

    for j in heads:
        qk = jnp.dot(q_ref[0, :, head_cols[j]], kt_ref[0, head_cols[j], :], preferred_element_type=F32)
        for rows in row_blocks:
            s = qk[rows, :] * d_ref[j, rows, :]
            rs_ref[j, rows, :] = jnp.broadcast_to(jnp.sum(s, axis=-1, keepdims=True), (rb, LANES))
            s_ref[j, rows, :] = s.astype(BF16)

    for j in heads:
        qj, vj = q_ref[0, :, head_cols[j]], v_ref[0, :, head_cols[j]]
        q_c = jnp.dot(qj, c_ref[0, j].astype(BF16), preferred_element_type=F32)
        s_v = jnp.dot(s_ref[j], vj, preferred_element_type=F32)
        q_n = jnp.dot(qj, ncol_ref[j].astype(BF16), preferred_element_type=F32)
        for rows in row_blocks:
            w_inter = wi_ref[j, rows, :]
            den = w_inter * q_n[rows, :] + rs_ref[j, rows, :]
            scale = 1.0 / jnp.maximum(jnp.abs(den), emt_ref[j, rows, :])
            hh = (spread(w_inter, ML_DHEAD) * q_c[rows, :] + s_v[rows, :]) * spread(scale, ML_DHEAD)
            hh_ref[j, rows, :] = hh
            wi_ref[j, rows, :] = jnp.broadcast_to(jnp.mean(hh, axis=-1, keepdims=True), (rb, LANES))
    for j, rows in units:
        hc = hh_ref[j, rows, :] - spread(wi_ref[j, rows, :], ML_DHEAD)
        var = jnp.mean(hc * hc, axis=-1, keepdims=True)
        hn = hc * lax.rsqrt(var + LN_EPS) * ng_ref[:, head_cols[j]] * so_ref[0, rows, head_cols[j]].astype(F32)
        hm_ref[0, rows, head_cols[j]] = hn.astype(BF16)

    ones = jnp.ones((blk, LANES), BF16)
    for j in heads:
        kw = (kt_ref[0, head_cols[j], :].astype(F32) * w_s_t[j:j + 1, :]).astype(BF16)
        d_j = lane_of(decay, j)
        c_ref[0, j] = d_j * c_ref[0, j] + jnp.dot(kw, v_ref[0, :, head_cols[j]], preferred_element_type=F32)
        ncol_ref[j] = d_j * ncol_ref[j] + jnp.dot(kw, ones, preferred_element_type=F32)
    m_ref[0] = m_new

    @pl.when(pl.program_id(1) == pl.num_programs(1) - 1)
    def _():
        for j in heads:
            n_ref[0, j:j + 1, :] = ncol_ref[j].T[0:1, :]


def _mlstm(q, kt, v, so, gates, state, norm_g):
    b, t, _ = q.shape
    blk = min(256, t)
    carried = [] if state is None else list(state)
    tok = pl.BlockSpec((1, blk, ML_WIDTH), lambda i, j: (i, j, 0))
    c_spec = pl.BlockSpec((1, ML_HEADS, ML_DHEAD, ML_DHEAD), lambda i, j: (i, 0, 0, 0))
    n_spec = pl.BlockSpec((1, ML_HEADS, ML_DHEAD), lambda i, j: (i, 0, 0))
    m_spec = pl.BlockSpec((1, 1, LANES), lambda i, j: (i, 0, 0))
    per_head = lambda width, dtype: pltpu.VMEM((ML_HEADS, blk, width), dtype)
    return pl.pallas_call(
        functools.partial(_mlstm_kernel, blk=blk, fresh=state is None),
        grid=(b, t // blk),
        in_specs=[tok, pl.BlockSpec((1, ML_WIDTH, blk), lambda i, j: (i, 0, j)), tok, tok,
                  pl.BlockSpec((1, blk, GATE_LANES), lambda i, j: (i, j, 0))]
                 + [c_spec, pl.BlockSpec((1, ML_HEADS, ML_DHEAD, LANES), lambda i, j: (i, 0, 0, 0)),
                    m_spec][:len(carried)]
                 + [_const_spec(norm_g.shape)],
        out_specs=[tok, c_spec, n_spec, m_spec],
        out_shape=[jax.ShapeDtypeStruct((b, t, ML_WIDTH), BF16),
                   jax.ShapeDtypeStruct((b, ML_HEADS, ML_DHEAD, ML_DHEAD), F32),
                   jax.ShapeDtypeStruct((b, ML_HEADS, ML_DHEAD), F32),
                   jax.ShapeDtypeStruct((b, 1, LANES), F32)],
        scratch_shapes=[pltpu.VMEM((ML_HEADS, ML_DHEAD, LANES), F32),
                        per_head(blk, F32), per_head(blk, BF16), per_head(ML_DHEAD, F32),
                        per_head(LANES, F32), per_head(LANES, F32), per_head(LANES, F32)],
        compiler_params=_params("parallel", "arbitrary"),
        name="mlstm",
    )(q, kt, v, so, gates, *carried, norm_g)


KEY_BLOCK = 256
N_KEY_BLOCKS = 3
N_STAGE = 4
ROW_BLOCK = 64


def _band_attn_kernel(q_ref, kt0_ref, kt1_ref, kt2_ref, v0_ref, v1_ref, v2_ref, bias_ref, o_ref,
                      s_ref, p_ref, mx_ref, linv_ref, even_ref, *, lead_blocks):
    kt_refs = (kt0_ref, kt1_ref, kt2_ref)
    v_refs = (v0_ref, v1_ref, v2_ref)
    qb = q_ref.shape[1]
    lane = lax.broadcasted_iota(jnp.int32, (qb, LANES), 1)
    low = lane < ATT_DHEAD

    def attend(first_slot):
        slots = range(first_slot, N_KEY_BLOCKS)
        k_lo = first_slot * KEY_BLOCK

        def scores(head):
            sl = slice(head // 2 * LANES, (head // 2 + 1) * LANES)
            q2 = q_ref[0, :, sl].astype(F32)
            qh = jnp.where(low if head % 2 == 0 else jnp.logical_not(low), q2, 0.0).astype(BF16)
            row_max = None
            for c in slots:
                cols = slice(c * KEY_BLOCK, (c + 1) * KEY_BLOCK)
                s = jnp.dot(qh, kt_refs[c][0, sl, :], preferred_element_type=F32) + bias_ref[head, :, cols]
                s_ref[head % N_STAGE, :, cols] = s
                part = jnp.maximum(s[:, :LANES], s[:, LANES:])
                row_max = part if row_max is None else jnp.maximum(row_max, part)
            mx_ref[head % N_STAGE] = row_max

        def weighted_values(head):
            sl = slice(head // 2 * LANES, (head // 2 + 1) * LANES)
            buf = head % N_STAGE
            rb = min(ROW_BLOCK, qb)
            for r in range(qb // rb):
                rows = slice(r * rb, (r + 1) * rb)
                m = jnp.max(mx_ref[buf, rows, :], axis=-1, keepdims=True)
                row_sum = None
                for c in slots:
                    cols = slice(c * KEY_BLOCK, (c + 1) * KEY_BLOCK)
                    p = jnp.exp2(s_ref[buf, rows, cols] - m)
                    p_ref[buf, rows, cols] = p.astype(BF16)
                    part = p[:, :LANES] + p[:, LANES:]
                    row_sum = part if row_sum is None else row_sum + part
                l = jnp.sum(row_sum, axis=-1, keepdims=True)
                linv_ref[buf, rows, :] = jnp.broadcast_to(1.0 / l, (rb, LANES))
            v2 = jnp.concatenate([v_refs[c][0, :, sl] for c in slots], axis=0)
            return jnp.dot(p_ref[buf, :, k_lo:], v2, preferred_element_type=F32) * linv_ref[buf]

        scores(0)
        for head in range(ATT_HEADS):
            if head + 1 < ATT_HEADS:
                scores(head + 1)
            out = weighted_values(head)
            if head % 2 == 0:
                even_ref[...] = out
            else:
                sl = slice(head // 2 * LANES, (head // 2 + 1) * LANES)
                o_ref[0, :, sl] = jnp.where(low, even_ref[...], out).astype(BF16)

    if lead_blocks == 0:
        attend(0)
    else:
        j = pl.program_id(1)
        for missing in range(lead_blocks, 0, -1):
            pl.when(j == lead_blocks - missing)(functools.partial(attend, missing))
        pl.when(j >= lead_blocks)(functools.partial(attend, 0))


def _band_bias(table, qb):
    nkeys = N_KEY_BLOCKS * KEY_BLOCK
    heads = table.shape[0]
    period = nkeys + qb
    shift = np.arange(period)
    shift = np.where(shift < nkeys, shift, shift - period)
    rel_idx = np.clip(nkeys - qb - shift, -MAX_REL, MAX_REL) + MAX_REL
    diag = table[:, rel_idx].astype(F32)
    toep = jnp.tile(diag, (1, qb))[:, :qb * (period - 1)].reshape(heads, qb, period - 1)[:, :, :nkeys]
    qi = np.arange(qb)
    chunk_start = (qi // CHUNK) * CHUNK
    kpos = np.arange(nkeys) - (nkeys - qb)
    in_band = (kpos[None, :] >= chunk_start[:, None] - BAND) & (kpos[None, :] < chunk_start[:, None] + CHUNK)
    return jnp.where(jnp.asarray(in_band)[None], toep * LOG2_E, NEG_INF)


def _band_attn(q, kt, v, bias, *, streaming):
    b, tq, _ = q.shape
    qb = bias.shape[1]
    lead = N_KEY_BLOCKS - qb // KEY_BLOCK if streaming else 0
    if streaming:
        def kidx(s):
            return lambda i, j: jnp.maximum(j - lead + s, 0)
    else:
        def kidx(s):
            return lambda i, j: s
    kt_specs = [pl.BlockSpec((1, ATT_WIDTH, KEY_BLOCK), (lambda f: lambda i, j: (i, 0, f(i, j)))(kidx(s)))
                for s in range(N_KEY_BLOCKS)]
    v_specs = [pl.BlockSpec((1, KEY_BLOCK, ATT_WIDTH), (lambda f: lambda i, j: (i, f(i, j), 0))(kidx(s)))
               for s in range(N_KEY_BLOCKS)]
    tok = pl.BlockSpec((1, qb, ATT_WIDTH), lambda i, j: (i, j, 0))
    return pl.pallas_call(
        functools.partial(_band_attn_kernel, lead_blocks=lead),
        grid=(b, tq // qb),
        in_specs=[tok] + kt_specs + v_specs + [_const_spec(bias.shape)],
        out_specs=tok,
        out_shape=jax.ShapeDtypeStruct((b, tq, ATT_WIDTH), BF16),
        scratch_shapes=[pltpu.VMEM((N_STAGE, qb, N_KEY_BLOCKS * KEY_BLOCK), F32),
                        pltpu.VMEM((N_STAGE, qb, N_KEY_BLOCKS * KEY_BLOCK), BF16),
                        pltpu.VMEM((N_STAGE, qb, LANES), F32),
                        pltpu.VMEM((N_STAGE, qb, LANES), F32),
                        pltpu.VMEM((qb, LANES), F32)],
        compiler_params=_params("parallel", "arbitrary"),
        name="band_attn",
    )(q, kt, kt, kt, v, v, v, bias)


def _mix_ln_kernel(h_ref, hm_ref, oa_ref, sgm_ref, sga_ref, wm_ref, wa_ref, wout_ref, g_ref, b_ref, o_ref):
    y_ml = jnp.dot(hm_ref[...], wm_ref[...], preferred_element_type=F32)
    y_att = jnp.dot(oa_ref[...], wa_ref[...], preferred_element_type=F32)
    merged = sgm_ref[...].astype(F32) * y_ml + sga_ref[...].astype(F32) * y_att
    mix = jnp.dot(merged.astype(BF16), wout_ref[...], preferred_element_type=F32)
    o_ref[...] = _layer_norm(ALPHA * h_ref[...] + mix, g_ref[...], b_ref[...])


def _mix_ln(h, hm, oa, sgm, sga, wm, wa, wout, g, b):
    n = h.shape[0]
    tm = min(512, n)
    row = pl.BlockSpec((tm, D_MODEL), lambda i: (i, 0))
    consts = [wm, wa, wout, g, b]
    return pl.pallas_call(
        _mix_ln_kernel,
        grid=(n // tm,),
        in_specs=[row] * 5 + [_const_spec(a.shape) for a in consts],
        out_specs=row,
        out_shape=jax.ShapeDtypeStruct((n, D_MODEL), F32),
        compiler_params=_params("parallel"),
        name="mix_ln",
    )(h, hm, oa, sgm, sga, *consts)


def _prep_weights(p):
    w_in = p["w_in"]
    w_att = w_in[:, _O_AQ:]
    col_scale = np.ones((w_att.shape[1],), np.float32)
    col_scale[:ATT_WIDTH] = ATT_DHEAD ** -0.5 * LOG2_E
    pad = jnp.zeros((D_MODEL, GATE_LANES - 2 * ML_HEADS), F32)
    w_all = jnp.concatenate([w_in[:, :_O_MLI], w_att * col_scale, w_in[:, _O_MLI:_O_AQ], pad], axis=1).astype(BF16)
    bif = jnp.concatenate([p["b_ml_i"], p["b_ml_f"], jnp.zeros((GATE_LANES - 2 * ML_HEADS,), F32)])
    row = lambda a: a.reshape(1, -1).astype(F32)
    return {
        "w_in": w_all,
        "bif": row(bif),
        "conv_w": jnp.broadcast_to(p["ml_conv_w"].astype(F32)[:, None, :], (CONV_W, CONV_ROWS, 2 * ML_WIDTH)),
        "conv_b": jnp.broadcast_to(row(p["ml_conv_b"]), (CONV_ROWS, 2 * ML_WIDTH)),
        "norm_g": row(p["ml_norm_g"]),
        "wm": p["w_ml_proj"].astype(BF16),
        "wa": p["w_att_proj"].astype(BF16),
        "wout": p["w_out"].astype(BF16),
        "ffn1": (p["ffn1_w_gu"].astype(BF16), p["ffn1_w_down"].astype(BF16)),
        "ffn2": (p["ffn2_w_gu"].astype(BF16), p["ffn2_w_down"].astype(BF16)),
        "ln1": (row(p["ln1_g"]), row(p["ln1_b"])),
        "ln2": (row(p["ln2_g"]), row(p["ln2_b"])),
        "ln3": (row(p["ln3_g"]), row(p["ln3_b"])),
        "rel_bias": p["att_rel_bias"],
    }


def _encoder_layer(x, w, conv_prev, ml_state, att_past):
    b, t, _ = x.shape
    n = b * t
    flat = lambda a: a.reshape(n, a.shape[-1])
    h1 = _ffn_ln(flat(x), *w["ffn1"], *w["ln1"])
    (q, k, v, so, gates, new_conv, aq, akt, av, k_rows, v_rows, sgm, sga) = _in_proj(
        h1.reshape(b, t, D_MODEL), conv_prev, w)

    if ml_state is not None:
        c0, n0, m0 = ml_state
        m0p = jnp.pad(m0.astype(F32), ((0, 0), (0, LANES - ML_HEADS))).reshape(b, 1, LANES)
        n0_rep = jnp.broadcast_to(n0.astype(F32)[..., None], n0.shape + (LANES,))
        ml_state = (c0.astype(F32), n0_rep, m0p)
    hm, c1, n1, m1p = _mlstm(q, k, v, so, gates, ml_state, w["norm_g"])
    m1 = m1p[:, 0, :ML_HEADS]

    if att_past is None:
        bias = _band_bias(w["rel_bias"], KEY_BLOCK)
        oa = _band_attn(aq, akt, av, bias, streaming=True)
    else:
        k_past, v_past = att_past
        n_past = k_past.shape[1]
        nkeys = N_KEY_BLOCKS * KEY_BLOCK
        fill = nkeys - n_past - t
        kt_past = k_past.reshape(b, n_past, ATT_WIDTH).transpose(0, 2, 1).astype(BF16)
        kt_all = jnp.concatenate([jnp.zeros((b, ATT_WIDTH, fill), BF16), kt_past, akt], axis=2)
        v_all = jnp.concatenate([jnp.zeros((b, fill, ATT_WIDTH), BF16),
                                 v_past.reshape(b, n_past, ATT_WIDTH).astype(BF16), av], axis=1)
        bias = _band_bias(w["rel_bias"], t)
        oa = _band_attn(aq, kt_all, v_all, bias, streaming=False)

    h2 = _mix_ln(h1, flat(hm), flat(oa), flat(sgm), flat(sga), w["wm"], w["wa"], w["wout"], *w["ln2"])
    y = _ffn_ln(h2, *w["ffn2"], *w["ln3"]).reshape(b, t, D_MODEL)
    keep = k_rows.shape[1]
    state = (new_conv, c1, n1, m1,
             k_rows.reshape(b, keep, ATT_HEADS, ATT_DHEAD), v_rows.reshape(b, keep, ATT_HEADS, ATT_DHEAD))
    return y, state


def kernel(x_prompt, x_sample, state_ml_conv, state_ml_C, state_ml_n, state_ml_m, cache_att_k, cache_att_v,
           w_in, b_ml_i, b_ml_f, ml_conv_w, ml_conv_b, ml_norm_g, att_rel_bias, w_ml_proj, w_att_proj, w_out,
           ffn1_w_gu, ffn1_w_down, ffn2_w_gu, ffn2_w_down, ln1_g, ln1_b, ln2_g, ln2_b, ln3_g, ln3_b):
    depth = w_in.shape[0]
    y_p, y_s = x_prompt, x_sample
    p_states, s_states = [], []
    for l in range(depth):
        w = _prep_weights({
            "w_in": w_in[l], "b_ml_i": b_ml_i[l], "b_ml_f": b_ml_f[l], "ml_conv_w": ml_conv_w[l],
            "ml_conv_b": ml_conv_b[l], "ml_norm_g": ml_norm_g[l], "att_rel_bias": att_rel_bias[l],
            "w_ml_proj": w_ml_proj[l], "w_att_proj": w_att_proj[l], "w_out": w_out[l],
            "ffn1_w_gu": ffn1_w_gu[l], "ffn1_w_down": ffn1_w_down[l], "ffn2_w_gu": ffn2_w_gu[l],
            "ffn2_w_down": ffn2_w_down[l], "ln1_g": ln1_g[l], "ln1_b": ln1_b[l], "ln2_g": ln2_g[l],
            "ln2_b": ln2_b[l], "ln3_g": ln3_g[l], "ln3_b": ln3_b[l]})
        y_p, st_p = _encoder_layer(y_p, w, None, None, None)
        y_s, st_s = _encoder_layer(y_s, w, state_ml_conv[l],
                                   (state_ml_C[l], state_ml_n[l], state_ml_m[l]),
                                   (cache_att_k[l], cache_att_v[l]))
        p_states.append(st_p)
        s_states.append(st_s)
    p_conv, p_c, p_n, p_m, p_k, p_v = [jnp.stack(s) for s in zip(*p_states)]
    s_conv, s_c, s_n, s_m, s_k, s_v = [jnp.stack(s) for s in zip(*s_states)]
    return (y_p, y_s, p_conv, s_conv, p_c, s_c, p_n, s_n, p_m, s_m, p_k, s_k, p_v, s_v)
```

```python
import functools

import jax
import jax.numpy as jnp
import numpy as np
from jax import lax
from jax.experimental import pallas as pl
from jax.experimental.pallas import tpu as pltpu

F32 = jnp.float32
BF16 = jnp.bfloat16

D_MODEL = 1024
CHUNK = 64
ML_HEADS = 4
ML_DHEAD = D_MODEL // ML_HEADS
ML_WIDTH = ML_HEADS * ML_DHEAD
CONV_W = 4
ATT_HEADS = 16
ATT_DHEAD = D_MODEL // ATT_HEADS
ATT_WIDTH = ATT_HEADS * ATT_DHEAD
BAND_CHUNKS = 8
BAND = BAND_CHUNKS * CHUNK
MAX_REL = 128
D_FF = ((8 * D_MODEL) // 3 + 127) // 128 * 128
ALPHA = 2.0 ** 0.25
LN_EPS = 1e-5
NEG_INF = -1e30
LOG2_E = 1.4426950408889634

LANES = 128
GATE_LANES = LANES
VMEM_LIMIT = 56 * 1024 * 1024

_O_MLI = 4 * ML_WIDTH
_O_AQ = _O_MLI + 2 * ML_HEADS
_W_V, _W_O, _W_AQ, _W_AK, _W_AV, _W_GM, _W_GA, _W_IF = 2, 3, 4, 5, 6, 7, 8, 9


def _const_spec(shape, col_block=0):
    index = (0,) * (len(shape) - 1) + (col_block,)
    return pl.BlockSpec(shape, lambda *_: index, pipeline_mode=pl.Buffered(1))


def _params(*sem):
    return pltpu.CompilerParams(dimension_semantics=sem, vmem_limit_bytes=VMEM_LIMIT)


def _layer_norm(y, g, b):
    mu = jnp.mean(y, axis=-1, keepdims=True)
    yc = y - mu
    var = jnp.mean(yc * yc, axis=-1, keepdims=True)
    return yc * lax.rsqrt(var + LN_EPS) * g + b


def _sigmoid(x):
    return 1.0 / (1.0 + jnp.exp(-x))


_FF_CHUNKS = ((0, 768), (768, 1792), (1792, D_FF))


def _ffn_ln_kernel(x_ref, wg_ref, wu_ref, wd_ref, g_ref, b_ref, o_ref):
    x = x_ref[...]
    xb = x.astype(BF16)
    acc = None
    for s, e in _FF_CHUNKS:
        gate = jnp.dot(xb, wg_ref[:, s:e], preferred_element_type=F32)
        up = jnp.dot(xb, wu_ref[:, s:e], preferred_element_type=F32)
        hid = (gate * _sigmoid(gate) * up).astype(BF16)
        part = jnp.dot(hid, wd_ref[s:e, :], preferred_element_type=F32)
        acc = part if acc is None else acc + part
    o_ref[...] = _layer_norm(ALPHA * x + 0.5 * acc, g_ref[...], b_ref[...])


def _ffn_ln(x, wgu, wd, g, b):
    n = x.shape[0]
    tm = min(512, n)
    row = pl.BlockSpec((tm, D_MODEL), lambda i: (i, 0))
    return pl.pallas_call(
        _ffn_ln_kernel,
        grid=(n // tm,),
        in_specs=[row, _const_spec((D_MODEL, D_FF), 0), _const_spec((D_MODEL, D_FF), 1), _const_spec(wd.shape),
                  _const_spec(g.shape), _const_spec(b.shape)],
        out_specs=row,
        out_shape=jax.ShapeDtypeStruct((n, D_MODEL), F32),
        compiler_params=_params("parallel"),
        name="ffn_ln",
    )(x, wgu, wgu, wd, g, b)


COL_CHUNK = 256
CONV_ROWS = 16


def _chunk_jobs(hb_ref):
    def proj(w_ref, c):
        cols = slice(c * COL_CHUNK, (c + 1) * COL_CHUNK)
        return jnp.dot(hb_ref[...], w_ref[:, cols], preferred_element_type=F32), cols

    def plain(w_ref, o_ref):
        def job(c):
            y, cols = proj(w_ref, c)
            o_ref[0, :, cols] = y.astype(BF16)
        return job

    def gated(w_ref, o_ref):
        def job(c):
            y, cols = proj(w_ref, c)
            o_ref[0, :, cols] = _sigmoid(y).astype(BF16)
        return job

    return proj, plain, gated


def _interleave(*queues):
    order = []
    for i in range(max(len(q) for q in queues)):
        for q in queues:
            order += q[i:i + 1]
    return order


def _in_proj_ml_kernel(h_ref, *refs, tm, fresh):
    cprev_ref = None if fresh else refs[0]
    (wqk_ref, wv_ref, wo_ref, wif_ref, bif_ref, cw_ref, cb_ref,
     q_ref, kt_ref, v_ref, so_ref, gt_ref, conv_ref, hb_ref, raw_ref, kc_ref) = refs[0 if fresh else 1:]
    hb_ref[...] = h_ref[0].astype(BF16)
    proj, plain, gated = _chunk_jobs(hb_ref)

    @pl.when(pl.program_id(1) == 0)
    def _():
        raw_ref[0:8, :] = jnp.zeros((8, 2 * ML_WIDTH), F32)
        if not fresh:
            raw_ref[8 - (CONV_W - 1):8, :] = cprev_ref[0]

    def ml_qk(c):
        y, cols = proj(wqk_ref, c)
        raw_ref[8:8 + tm, cols] = y
        is_k = c * COL_CHUNK >= ML_WIDTH
        sub = lax.broadcasted_iota(jnp.int32, (8, COL_CHUNK), 0)
        for r in range(tm // CONV_ROWS):
            base = 8 + r * CONV_ROWS
            blocks = [raw_ref[base + 8 * i:base + 8 * i + 8, cols] for i in range(-1, CONV_ROWS // 8)]
            taps = [jnp.concatenate(blocks[1:], axis=0)]
            for j in range(1, CONV_W):
                taps.append(jnp.concatenate(
                    [pltpu.roll(jnp.where(sub < 8 - j, cur, prev), j, axis=0)
                     for prev, cur in zip(blocks[:-1], blocks[1:])], axis=0))
            acc = cb_ref[:, cols]
            for j in range(CONV_W):
                acc = acc + taps[j] * cw_ref[CONV_W - 1 - j, :, cols]
            qk = acc * _sigmoid(acc)
            rows = slice(r * CONV_ROWS, (r + 1) * CONV_ROWS)
            if is_k:
                kc_ref[rows, :] = qk * (ML_DHEAD ** -0.5)
            else:
                q_ref[0, rows, cols] = qk.astype(BF16)
        if is_k:
            kt_ref[0, cols.start - ML_WIDTH:cols.stop - ML_WIDTH, :] = kc_ref[...].astype(BF16).T

    n_chunks = D_MODEL // COL_CHUNK
    heavy = [functools.partial(ml_qk, c) for c in range(2 * n_chunks)]
    light = [functools.partial(job, c) for c in range(n_chunks)
             for job in (plain(wv_ref, v_ref), gated(wo_ref, so_ref))]
    for job in _interleave(heavy, light):
        job()
    conv_ref[0] = raw_ref[8 + tm - (CONV_W - 1):8 + tm, :]
    raw_ref[0:8, :] = raw_ref[tm:tm + 8, :]

    zg = jnp.dot(hb_ref[...], wif_ref[...], preferred_element_type=F32) + bif_ref[...]
    lane = lax.broadcasted_iota(jnp.int32, zg.shape, 1)
    log_sig = jnp.minimum(zg, 0.0) - jnp.log(1.0 + jnp.exp(-jnp.abs(zg)))
    gt_ref[0] = jnp.where(lane < ML_HEADS, zg, log_sig)


def _in_proj_att_kernel(h_ref, waq_ref, wak_ref, wav_ref, wgm_ref, wga_ref,
                        aq_ref, akt_ref, av_ref, pk_ref, pv_ref, sgm_ref, sga_ref, hb_ref):
    hb_ref[...] = h_ref[0].astype(BF16)
    proj, plain, gated = _chunk_jobs(hb_ref)

    def att_k(c):
        y, cols = proj(wak_ref, c)
        pk_ref[0, :, cols] = y
        akt_ref[0, cols, :] = y.astype(BF16).T

    def att_v(c):
        y, cols = proj(wav_ref, c)
        pv_ref[0, :, cols] = y
        av_ref[0, :, cols] = y.astype(BF16)

    n_chunks = D_MODEL // COL_CHUNK
    heavy = [functools.partial(job, c) for c in range(n_chunks)
             for job in (gated(wgm_ref, sgm_ref), att_k, gated(wga_ref, sga_ref))]
    light = [functools.partial(job, c) for c in range(n_chunks) for job in (plain(waq_ref, aq_ref), att_v)]
    for job in _interleave(heavy, light):
        job()


def _in_proj(h, conv_prev, w):
    b, t, _ = h.shape
    tm = min(512, t)
    nt = t // tm
    keep = min(BAND, t)
    first_kept = (t - keep) // tm
    h_spec = pl.BlockSpec((1, tm, D_MODEL), lambda i, j: (i, j, 0))

    def tok(width, dtype):
        return (pl.BlockSpec((1, tm, width), lambda i, j: (i, j, 0)),
                jax.ShapeDtypeStruct((b, t, width), dtype))

    def kept(width):
        return (pl.BlockSpec((1, tm, width), lambda i, j: (i, jnp.maximum(j - first_kept, 0), 0)),
                jax.ShapeDtypeStruct((b, keep, width), F32))

    conv_spec = pl.BlockSpec((1, CONV_W - 1, 2 * ML_WIDTH), lambda i, j: (i, 0, 0))
    feature_major = (pl.BlockSpec((1, ATT_WIDTH, tm), lambda i, j: (i, 0, j)),
                     jax.ShapeDtypeStruct((b, ATT_WIDTH, t), BF16))
    ml_outs = [
        tok(ML_WIDTH, BF16), feature_major, tok(ML_WIDTH, BF16), tok(ML_WIDTH, BF16),
        tok(GATE_LANES, F32),
        (conv_spec, jax.ShapeDtypeStruct((b, CONV_W - 1, 2 * ML_WIDTH), F32)),
    ]
    w_in = w["w_in"]
    square = (D_MODEL, D_MODEL)
    ml_weights = [w_in, w_in, w_in, w_in, w["bif"], w["conv_w"], w["conv_b"]]
    fresh = conv_prev is None
    carried = [] if fresh else [conv_prev]
    ml = pl.pallas_call(
        functools.partial(_in_proj_ml_kernel, tm=tm, fresh=fresh),
        grid=(b, nt),
        in_specs=[h_spec] + [conv_spec] * len(carried)
                 + [_const_spec((D_MODEL, 2 * ML_WIDTH), 0), _const_spec(square, _W_V), _const_spec(square, _W_O),
                    _const_spec((D_MODEL, GATE_LANES), _W_IF * (D_MODEL // GATE_LANES))]
                 + [_const_spec(a.shape) for a in ml_weights[4:]],
        out_specs=[o[0] for o in ml_outs],
        out_shape=[o[1] for o in ml_outs],
        scratch_shapes=[pltpu.VMEM((tm, D_MODEL), BF16), pltpu.VMEM((tm + 8, 2 * ML_WIDTH), F32),
                        pltpu.VMEM((tm, COL_CHUNK), F32)],
        compiler_params=_params("parallel", "arbitrary"),
        name="in_proj_ml",
    )(h, *carried, *ml_weights)

    att_outs = [
        tok(ATT_WIDTH, BF16),
        feature_major,
        tok(ATT_WIDTH, BF16),
        kept(ATT_WIDTH), kept(ATT_WIDTH),
        tok(D_MODEL, BF16), tok(D_MODEL, BF16),
    ]
    att_weights = [w_in] * 5
    att = pl.pallas_call(
        _in_proj_att_kernel,
        grid=(b, nt),
        in_specs=[h_spec] + [_const_spec(square, c) for c in (_W_AQ, _W_AK, _W_AV, _W_GM, _W_GA)],
        out_specs=[o[0] for o in att_outs],
        out_shape=[o[1] for o in att_outs],
        scratch_shapes=[pltpu.VMEM((tm, D_MODEL), BF16)],
        compiler_params=_params("parallel", "arbitrary"),
        name="in_proj_att",
    )(h, *att_weights)
    return tuple(ml) + tuple(att)


def _split3(x):
    hi = x.astype(BF16)
    rest = x - hi.astype(F32)
    mid = rest.astype(BF16)
    return hi, mid, (rest - mid.astype(F32)).astype(BF16)


def _mlstm_kernel(q_ref, kt_ref, v_ref, so_ref, gt_ref, *refs, blk, fresh):
    c0_ref, n0_ref, m0_ref = (None, None, None) if fresh else refs[:3]
    (ng_ref, hm_ref, c_ref, n_ref, m_ref,
     ncol_ref, d_ref, s_ref, hh_ref, wi_ref, emt_ref, rs_ref) = refs[0 if fresh else 3:]
    heads = range(ML_HEADS)
    rb = min(ROW_BLOCK, blk)
    row_blocks = [slice(r * rb, (r + 1) * rb) for r in range(blk // rb)]
    head_cols = [slice(j * ML_DHEAD, (j + 1) * ML_DHEAD) for j in heads]

    @pl.when(pl.program_id(1) == 0)
    def _():
        if fresh:
            c_ref[...] = jnp.zeros_like(c_ref)
            ncol_ref[...] = jnp.zeros_like(ncol_ref)
            m_ref[...] = jnp.zeros_like(m_ref)
        else:
            c_ref[...] = c0_ref[...]
            ncol_ref[...] = n0_ref[0]
            m_ref[...] = m0_ref[...]

    gates = gt_ref[0]
    tril = (lax.broadcasted_iota(jnp.int32, (blk, blk), 1)
            <= lax.broadcasted_iota(jnp.int32, (blk, blk), 0)).astype(BF16)
    csum = sum(jnp.dot(tril, part, preferred_element_type=F32) for part in _split3(gates))
    bcum = pltpu.roll(csum, LANES - ML_HEADS, axis=1)
    li_rel = gates - bcum
    li_rel_t = li_rel.T
    m_prev = m_ref[0]
    b_last = bcum[blk - 1:blk, :]
    g_s = b_last + li_rel
    m_new = jnp.maximum(b_last + m_prev, jnp.max(g_s, axis=0, keepdims=True))
    decay = jnp.exp(b_last + m_prev - m_new)
    w_s_t = jnp.exp(g_s - m_new).T
    inter = bcum + m_prev

    def lane_of(x, j):
        lane = lax.broadcasted_iota(jnp.int32, x.shape, 1)
        return jnp.sum(jnp.where(lane == j, x, 0.0), axis=-1, keepdims=True)

    def spread(x, width):
        return x[:, :width] if width <= LANES else jnp.concatenate([x] * (width // LANES), axis=1)

    units = [(j, rows) for j in heads for rows in row_blocks]

    for j, rows in units:
        wi_ref[j, rows, :] = jnp.broadcast_to(lane_of(bcum[rows, :], j), (rb, LANES))
    m_prev_j = [lane_of(m_prev, j) for j in heads]
    for j, rows in units:
        causal = (lax.broadcasted_iota(jnp.int32, (rb, blk), 1)
                  <= lax.broadcasted_iota(jnp.int32, (rb, blk), 0) + rows.start)
        bcol = wi_ref[j, rows, :]
        dmat = jnp.where(causal, spread(bcol, blk) + li_rel_t[j:j + 1, :], -jnp.inf)
        icol = bcol + m_prev_j[j]
        m_t = jnp.maximum(icol, jnp.max(dmat, axis=-1, keepdims=True))
        d_ref[j, rows, :] = jnp.exp(dmat - spread(m_t, blk))
        wi_ref[j, rows, :] = jnp.exp(icol - m_t)
        emt_ref[j, rows, :] = jnp.exp(-m_t)

    for j in heads:
        qk = jnp.dot(q_ref[0, :, head_cols[j]], kt_ref[0, head_cols[j], :], preferred_element_type=F32)
        for rows in row_blocks:
            s = qk[rows, :] * d_ref[j, rows, :]
            rs_ref[j, rows, :] = jnp.broadcast_to(jnp.sum(s, axis=-1, keepdims=True), (rb, LANES))
            s_ref[j, rows, :] = s.astype(BF16)

    for j in heads:
        qj, vj = q_ref[0, :, head_cols[j]], v_ref[0, :, head_cols[j]]
        q_c = jnp.dot(qj, c_ref[0, j].astype(BF16), preferred_element_type=F32)
        s_v = jnp.dot(s_ref[j], vj, preferred_element_type=F32)
        q_n = jnp.dot(qj, ncol_ref[j].astype(BF16), preferred_element_type=F32)
        for rows in row_blocks:
            w_inter = wi_ref[j, rows, :]
            den = w_inter * q_n[rows, :] + rs_ref[j, rows, :]
            scale = 1.0 / jnp.maximum(jnp.abs(den), emt_ref[j, rows, :])
            hh = (spread(w_inter, ML_DHEAD) * q_c[rows, :] + s_v[rows, :]) * spread(scale, ML_DHEAD)
            hh_ref[j, rows, :] = hh
            wi_ref[j, rows, :] = jnp.broadcast_to(jnp.mean(hh, axis=-1, keepdims=True), (rb, LANES))
    for j, rows in units:
        hc = hh_ref[j, rows, :] - spread(wi_ref[j, rows, :], ML_DHEAD)
        var = jnp.mean(hc * hc, axis=-1, keepdims=True)
        hn = hc * lax.rsqrt(var + LN_EPS) * ng_ref[:, head_cols[j]] * so_ref[0, rows, head_cols[j]].astype(F32)
        hm_ref[0, rows, head_cols[j]] = hn.astype(BF16)

    ones = jnp.ones((blk, LANES), BF16)
    for j in heads:
        kw = (kt_ref[0, head_cols[j], :].astype(F32) * w_s_t[j:j + 1, :]).astype(BF16)
        d_j = lane_of(decay, j)
        c_ref[0, j] = d_j * c_ref[0, j] + jnp.dot(kw, v_ref[0, :, head_cols[j]], preferred_element_type=F32)
        ncol_ref[j] = d_j * ncol_ref[j] + jnp.dot(kw, ones, preferred_element_type=F32)
    m_ref[0] = m_new

    @pl.when(pl.program_id(1) == pl.num_programs(1) - 1)
    def _():
        for j in heads:
            n_ref[0, j:j + 1, :] = ncol_ref[j].T[0:1, :]


def _mlstm(q, kt, v, so, gates, state, norm_g):
    b, t, _ = q.shape
    blk = min(256, t)
    carried = [] if state is None else list(state)
    tok = pl.BlockSpec((1, blk, ML_WIDTH), lambda i, j: (i, j, 0))
    c_spec = pl.BlockSpec((1, ML_HEADS, ML_DHEAD, ML_DHEAD), lambda i, j: (i, 0, 0, 0))
    n_spec = pl.BlockSpec((1, ML_HEADS, ML_DHEAD), lambda i, j: (i, 0, 0))
    m_spec = pl.BlockSpec((1, 1, LANES), lambda i, j: (i, 0, 0))
    per_head = lambda width, dtype: pltpu.VMEM((ML_HEADS, blk, width), dtype)
    return pl.pallas_call(
        functools.partial(_mlstm_kernel, blk=blk, fresh=state is None),
        grid=(b, t // blk),
        in_specs=[tok, pl.BlockSpec((1, ML_WIDTH, blk), lambda i, j: (i, 0, j)), tok, tok,
                  pl.BlockSpec((1, blk, GATE_LANES), lambda i, j: (i, j, 0))]
                 + [c_spec, pl.BlockSpec((1, ML_HEADS, ML_DHEAD, LANES), lambda i, j: (i, 0, 0, 0)),
                    m_spec][:len(carried)]
                 + [_const_spec(norm_g.shape)],
        out_specs=[tok, c_spec, n_spec, m_spec],
        out_shape=[jax.ShapeDtypeStruct((b, t, ML_WIDTH), BF16),
                   jax.ShapeDtypeStruct((b, ML_HEADS, ML_DHEAD, ML_DHEAD), F32),
                   jax.ShapeDtypeStruct((b, ML_HEADS, ML_DHEAD), F32),
                   jax.ShapeDtypeStruct((b, 1, LANES), F32)],
        scratch_shapes=[pltpu.VMEM((ML_HEADS, ML_DHEAD, LANES), F32),
                        per_head(blk, F32), per_head(blk, BF16), per_head(ML_DHEAD, F32),
                        per_head(LANES, F32), per_head(LANES, F32), per_head(LANES, F32)],
        compiler_params=_params("parallel", "arbitrary"),
        name="mlstm",
    )(q, kt, v, so, gates, *carried, norm_g)


KEY_BLOCK = 256
N_KEY_BLOCKS = 3
N_STAGE = 4
ROW_BLOCK = 64


def _band_attn_kernel(q_ref, kt0_ref, kt1_ref, kt2_ref, v0_ref, v1_ref, v2_ref, bias_ref, o_ref,
                      s_ref, p_ref, mx_ref, linv_ref, even_ref, *, lead_blocks):
    kt_refs = (kt0_ref, kt1_ref, kt2_ref)
    v_refs = (v0_ref, v1_ref, v2_ref)
    qb = q_ref.shape[1]
    lane = lax.broadcasted_iota(jnp.int32, (qb, LANES), 1)
    low = lane < ATT_DHEAD

    def attend(first_slot):
        slots = range(first_slot, N_KEY_BLOCKS)
        k_lo = first_slot * KEY_BLOCK

        def scores(head):
            sl = slice(head // 2 * LANES, (head // 2 + 1) * LANES)
            q2 = q_ref[0, :, sl].astype(F32)
            qh = jnp.where(low if head % 2 == 0 else jnp.logical_not(low), q2, 0.0).astype(BF16)
            row_max = None
            for c in slots:
                cols = slice(c * KEY_BLOCK, (c + 1) * KEY_BLOCK)
                s = jnp.dot(qh, kt_refs[c][0, sl, :], preferred_element_type=F32) + bias_ref[head, :, cols]
                s_ref[head % N_STAGE, :, cols] = s
                part = jnp.maximum(s[:, :LANES], s[:, LANES:])
                row_max = part if row_max is None else jnp.maximum(row_max, part)
            mx_ref[head % N_STAGE] = row_max

        def weighted_values(head):
            sl = slice(head // 2 * LANES, (head // 2 + 1) * LANES)
            buf = head % N_STAGE
            rb = min(ROW_BLOCK, qb)
            for r in range(qb // rb):
                rows = slice(r * rb, (r + 1) * rb)
                m = jnp.max(mx_ref[buf, rows, :], axis=-1, keepdims=True)
                row_sum = None
                for c in slots:
                    cols = slice(c * KEY_BLOCK, (c + 1) * KEY_BLOCK)
                    p = jnp.exp2(s_ref[buf, rows, cols] - m)
                    p_ref[buf, rows, cols] = p.astype(BF16)
                    part = p[:, :LANES] + p[:, LANES:]
                    row_sum = part if row_sum is None else row_sum + part
                l = jnp.sum(row_sum, axis=-1, keepdims=True)
                linv_ref[buf, rows, :] = jnp.broadcast_to(1.0 / l, (rb, LANES))
            v2 = jnp.concatenate([v_refs[c][0, :, sl] for c in slots], axis=0)
            return jnp.dot(p_ref[buf, :, k_lo:], v2, preferred_element_type=F32) * linv_ref[buf]

        scores(0)
        for head in range(ATT_HEADS):
            if head + 1 < ATT_HEADS:
                scores(head + 1)
            out = weighted_values(head)
            if head % 2 == 0:
                even_ref[...] = out
            else:
                sl = slice(head // 2 * LANES, (head // 2 + 1) * LANES)
                o_ref[0, :, sl] = jnp.where(low, even_ref[...], out).astype(BF16)

    j = pl.program_id(1)
    for missing in range(lead_blocks, 0, -1):
        pl.when(j == lead_blocks - missing)(functools.partial(attend, missing))
    pl.when(j >= lead_blocks)(functools.partial(attend, 0))


BIAS_PERIOD = 1024


def _band_bias_kernel(diag_ref, o_ref, *, qb):
    nkeys = N_KEY_BLOCKS * KEY_BLOCK
    toep = pltpu.roll(jnp.broadcast_to(diag_ref[0], (qb, BIAS_PERIOD)), 0, 1, stride=1, stride_axis=0)
    row = lax.broadcasted_iota(jnp.int32, (qb, nkeys), 0)
    kpos = lax.broadcasted_iota(jnp.int32, (qb, nkeys), 1) - (nkeys - qb)
    chunk_start = (row // CHUNK) * CHUNK
    in_band = jnp.logical_and(kpos >= chunk_start - BAND, kpos < chunk_start + CHUNK)
    o_ref[0] = jnp.where(in_band, toep[:, :nkeys] * LOG2_E, NEG_INF)


def _band_bias(table, qb):
    nkeys = N_KEY_BLOCKS * KEY_BLOCK
    heads = table.shape[0]
    shift = np.arange(BIAS_PERIOD)
    shift = np.where(shift < nkeys, shift, shift - BIAS_PERIOD)
    rel_idx = np.clip(nkeys - qb - shift, -MAX_REL, MAX_REL) + MAX_REL
    diag = table[:, rel_idx].astype(F32).reshape(heads, 1, BIAS_PERIOD)
    return pl.pallas_call(
        functools.partial(_band_bias_kernel, qb=qb),
        grid=(heads,),
        in_specs=[pl.BlockSpec((1, 1, BIAS_PERIOD), lambda h: (h, 0, 0))],
        out_specs=pl.BlockSpec((1, qb, nkeys), lambda h: (h, 0, 0)),
        out_shape=jax.ShapeDtypeStruct((heads, qb, nkeys), F32),
        compiler_params=_params("parallel"),
        name="band_bias",
    )(diag)


def _cached_attn_kernel(q_ref, ktn_ref, vn_ref, kp_ref, vp_ref, bias_ref, o_ref):
    t, n_past = q_ref.shape[1], kp_ref.shape[1]
    low = lax.broadcasted_iota(jnp.int32, (t, LANES), 1) < ATT_DHEAD
    for g in range(ATT_HEADS // 2):
        sl = slice(g * LANES, (g + 1) * LANES)
        q2 = q_ref[0, :, sl].astype(F32)
        kp2, vp2 = kp_ref[0, :, sl].astype(BF16), vp_ref[0, :, sl].astype(BF16)
        halves = []
        for half in range(2):
            head = 2 * g + half
            qh = jnp.where(low if half == 0 else jnp.logical_not(low), q2, 0.0).astype(BF16)
            s_past = (lax.dot_general(qh, kp2, (((1,), (1,)), ((), ())), preferred_element_type=F32)
                      + bias_ref[head, :, :n_past])
            s_new = jnp.dot(qh, ktn_ref[0, sl, :], preferred_element_type=F32) + bias_ref[head, :, n_past:]
            m = jnp.maximum(jnp.max(s_past, axis=-1, keepdims=True), jnp.max(s_new, axis=-1, keepdims=True))
            p_past, p_new = jnp.exp2(s_past - m), jnp.exp2(s_new - m)
            l = jnp.sum(p_past, axis=-1, keepdims=True) + jnp.sum(p_new, axis=-1, keepdims=True)
            pv = (jnp.dot(p_past.astype(BF16), vp2, preferred_element_type=F32)
                  + jnp.dot(p_new.astype(BF16), vn_ref[0, :, sl], preferred_element_type=F32))
            halves.append(pv / l)
        o_ref[0, :, sl] = jnp.where(low, halves[0], halves[1]).astype(BF16)


def _cached_attn(q, kt_new, v_new, k_past, v_past, bias):
    b, t, _ = q.shape
    n_past = k_past.shape[1]
    tok = pl.BlockSpec((1, t, ATT_WIDTH), lambda i: (i, 0, 0))
    past = pl.BlockSpec((1, n_past, ATT_WIDTH), lambda i: (i, 0, 0))
    return pl.pallas_call(
        _cached_attn_kernel,
        grid=(b,),
        in_specs=[tok, pl.BlockSpec((1, ATT_WIDTH, t), lambda i: (i, 0, 0)), tok, past, past,
                  _const_spec(bias.shape)],
        out_specs=tok,
        out_shape=jax.ShapeDtypeStruct((b, t, ATT_WIDTH), BF16),
        compiler_params=_params("parallel"),
        name="cached_attn",
    )(q, kt_new, v_new, k_past, v_past, bias)


def _attention(q, kt, v, att_past, rel_bias):
    if att_past is None:
        return _band_attn(q, kt, v, _band_bias(rel_bias, KEY_BLOCK))
    k_past, v_past = att_past
    b, t, _ = q.shape
    n_past = k_past.shape[1]
    bias = _band_bias(rel_bias, t)[:, :, N_KEY_BLOCKS * KEY_BLOCK - (n_past + t):]
    return _cached_attn(q, kt, v, k_past.reshape(b, n_past, ATT_WIDTH), v_past.reshape(b, n_past, ATT_WIDTH), bias)


def _band_attn(q, kt, v, bias):
    b, tq, _ = q.shape
    qb = bias.shape[1]
    lead = N_KEY_BLOCKS - qb // KEY_BLOCK

    def kidx(s):
        return lambda i, j: jnp.maximum(j - lead + s, 0)

    kt_specs = [pl.BlockSpec((1, ATT_WIDTH, KEY_BLOCK), (lambda f: lambda i, j: (i, 0, f(i, j)))(kidx(s)))
                for s in range(N_KEY_BLOCKS)]
    v_specs = [pl.BlockSpec((1, KEY_BLOCK, ATT_WIDTH), (lambda f: lambda i, j: (i, f(i, j), 0))(kidx(s)))
               for s in range(N_KEY_BLOCKS)]
    tok = pl.BlockSpec((1, qb, ATT_WIDTH), lambda i, j: (i, j, 0))
    return pl.pallas_call(
        functools.partial(_band_attn_kernel, lead_blocks=lead),
        grid=(b, tq // qb),
        in_specs=[tok] + kt_specs + v_specs + [_const_spec(bias.shape)],
        out_specs=tok,
        out_shape=jax.ShapeDtypeStruct((b, tq, ATT_WIDTH), BF16),
        scratch_shapes=[pltpu.VMEM((N_STAGE, qb, N_KEY_BLOCKS * KEY_BLOCK), F32),
                        pltpu.VMEM((N_STAGE, qb, N_KEY_BLOCKS * KEY_BLOCK), BF16),
                        pltpu.VMEM((N_STAGE, qb, LANES), F32),
                        pltpu.VMEM((N_STAGE, qb, LANES), F32),
                        pltpu.VMEM((qb, LANES), F32)],
        compiler_params=_params("parallel", "arbitrary"),
        name="band_attn",
    )(q, kt, kt, kt, v, v, v, bias)


def _mix_ln_kernel(h_ref, hm_ref, oa_ref, sgm_ref, sga_ref, wm_ref, wa_ref, wout_ref, g_ref, b_ref, o_ref):
    y_ml = jnp.dot(hm_ref[...], wm_ref[...], preferred_element_type=F32)
    y_att = jnp.dot(oa_ref[...], wa_ref[...], preferred_element_type=F32)
    merged = sgm_ref[...].astype(F32) * y_ml + sga_ref[...].astype(F32) * y_att
    mix = jnp.dot(merged.astype(BF16), wout_ref[...], preferred_element_type=F32)
    o_ref[...] = _layer_norm(ALPHA * h_ref[...] + mix, g_ref[...], b_ref[...])


def _mix_ln(h, hm, oa, sgm, sga, wm, wa, wout, g, b):
    n = h.shape[0]
    tm = min(512, n)
    row = pl.BlockSpec((tm, D_MODEL), lambda i: (i, 0))
    consts = [wm, wa, wout, g, b]
    return pl.pallas_call(
        _mix_ln_kernel,
        grid=(n // tm,),
        in_specs=[row] * 5 + [_const_spec(a.shape) for a in consts],
        out_specs=row,
        out_shape=jax.ShapeDtypeStruct((n, D_MODEL), F32),
        compiler_params=_params("parallel"),
        name="mix_ln",
    )(h, hm, oa, sgm, sga, *consts)


def _prep_weights(p):
    w_in = p["w_in"]
    w_att = w_in[:, _O_AQ:]
    col_scale = np.ones((w_att.shape[1],), np.float32)
    col_scale[:ATT_WIDTH] = ATT_DHEAD ** -0.5 * LOG2_E
    pad = jnp.zeros((D_MODEL, GATE_LANES - 2 * ML_HEADS), F32)
    w_all = jnp.concatenate([w_in[:, :_O_MLI], w_att * col_scale, w_in[:, _O_MLI:_O_AQ], pad], axis=1).astype(BF16)
    bif = jnp.concatenate([p["b_ml_i"], p["b_ml_f"], jnp.zeros((GATE_LANES - 2 * ML_HEADS,), F32)])
    row = lambda a: a.reshape(1, -1).astype(F32)
    return {
        "w_in": w_all,
        "bif": row(bif),
        "conv_w": jnp.broadcast_to(p["ml_conv_w"].astype(F32)[:, None, :], (CONV_W, CONV_ROWS, 2 * ML_WIDTH)),
        "conv_b": jnp.broadcast_to(row(p["ml_conv_b"]), (CONV_ROWS, 2 * ML_WIDTH)),
        "norm_g": row(p["ml_norm_g"]),
        "wm": p["w_ml_proj"].astype(BF16),
        "wa": p["w_att_proj"].astype(BF16),
        "wout": p["w_out"].astype(BF16),
        "ffn1": (p["ffn1_w_gu"].astype(BF16), p["ffn1_w_down"].astype(BF16)),
        "ffn2": (p["ffn2_w_gu"].astype(BF16), p["ffn2_w_down"].astype(BF16)),
        "ln1": (row(p["ln1_g"]), row(p["ln1_b"])),
        "ln2": (row(p["ln2_g"]), row(p["ln2_b"])),
        "ln3": (row(p["ln3_g"]), row(p["ln3_b"])),
        "rel_bias": p["att_rel_bias"],
    }


def _encoder_layer(x, w, conv_prev, ml_state, att_past):
    b, t, _ = x.shape
    n = b * t
    flat = lambda a: a.reshape(n, a.shape[-1])
    h1 = _ffn_ln(flat(x), *w["ffn1"], *w["ln1"])
    (q, k, v, so, gates, new_conv, aq, akt, av, k_rows, v_rows, sgm, sga) = _in_proj(
        h1.reshape(b, t, D_MODEL), conv_prev, w)

    if ml_state is not None:
        c0, n0, m0 = ml_state
        m0p = jnp.pad(m0.astype(F32), ((0, 0), (0, LANES - ML_HEADS))).reshape(b, 1, LANES)
        n0_rep = jnp.broadcast_to(n0.astype(F32)[..., None], n0.shape + (LANES,))
        ml_state = (c0.astype(F32), n0_rep, m0p)
    hm, c1, n1, m1p = _mlstm(q, k, v, so, gates, ml_state, w["norm_g"])
    m1 = m1p[:, 0, :ML_HEADS]

    oa = _attention(aq, akt, av, att_past, w["rel_bias"])
    h2 = _mix_ln(h1, flat(hm), flat(oa), flat(sgm), flat(sga), w["wm"], w["wa"], w["wout"], *w["ln2"])
    y = _ffn_ln(h2, *w["ffn2"], *w["ln3"]).reshape(b, t, D_MODEL)
    keep = k_rows.shape[1]
    state = (new_conv, c1, n1, m1,
             k_rows.reshape(b, keep, ATT_HEADS, ATT_DHEAD), v_rows.reshape(b, keep, ATT_HEADS, ATT_DHEAD))
    return y, state


def kernel(x_prompt, x_sample, state_ml_conv, state_ml_C, state_ml_n, state_ml_m, cache_att_k, cache_att_v,
           w_in, b_ml_i, b_ml_f, ml_conv_w, ml_conv_b, ml_norm_g, att_rel_bias, w_ml_proj, w_att_proj, w_out,
           ffn1_w_gu, ffn1_w_down, ffn2_w_gu, ffn2_w_down, ln1_g, ln1_b, ln2_g, ln2_b, ln3_g, ln3_b):
    depth = w_in.shape[0]
    y_p, y_s = x_prompt, x_sample
    p_states, s_states = [], []
    for l in range(depth):
        w = _prep_weights({
            "w_in": w_in[l], "b_ml_i": b_ml_i[l], "b_ml_f": b_ml_f[l], "ml_conv_w": ml_conv_w[l],
            "ml_conv_b": ml_conv_b[l], "ml_norm_g": ml_norm_g[l], "att_rel_bias": att_rel_bias[l],
            "w_ml_proj": w_ml_proj[l], "w_att_proj": w_att_proj[l], "w_out": w_out[l],
            "ffn1_w_gu": ffn1_w_gu[l], "ffn1_w_down": ffn1_w_down[l], "ffn2_w_gu": ffn2_w_gu[l],
            "ffn2_w_down": ffn2_w_down[l], "ln1_g": ln1_g[l], "ln1_b": ln1_b[l], "ln2_g": ln2_g[l],
            "ln2_b": ln2_b[l], "ln3_g": ln3_g[l], "ln3_b": ln3_b[l]})
        y_p, st_p = _encoder_layer(y_p, w, None, None, None)
        y_s, st_s = _encoder_layer(y_s, w, state_ml_conv[l],
                                   (state_ml_C[l], state_ml_n[l], state_ml_m[l]),
                                   (cache_att_k[l], cache_att_v[l]))
        p_states.append(st_p)
        s_states.append(st_s)
    p_conv, p_c, p_n, p_m, p_k, p_v = [jnp.stack(s) for s in zip(*p_states)]
    s_conv, s_c, s_n, s_m, s_k, s_v = [jnp.stack(s) for s in zip(*s_states)]
    return (y_p, y_s, p_conv, s_conv, p_c, s_c, p_n, s_n, p_m, s_m, p_k, s_k, p_v, s_v)
```

```python
import functools

import jax
import jax.numpy as jnp
import numpy as np
from jax import lax
from jax.experimental import pallas as pl
from jax.experimental.pallas import tpu as pltpu

F32 = jnp.float32
BF16 = jnp.bfloat16

D_MODEL = 1024
CHUNK = 64
ML_HEADS = 4
ML_DHEAD = D_MODEL // ML_HEADS
ML_WIDTH = ML_HEADS * ML_DHEAD
CONV_W = 4
ATT_HEADS = 16
ATT_DHEAD = D_MODEL // ATT_HEADS
ATT_WIDTH = ATT_HEADS * ATT_DHEAD
BAND_CHUNKS = 8
BAND = BAND_CHUNKS * CHUNK
MAX_REL = 128
D_FF = ((8 * D_MODEL) // 3 + 127) // 128 * 128
ALPHA = 2.0 ** 0.25
LN_EPS = 1e-5
NEG_INF = -1e30
LOG2_E = 1.4426950408889634

LANES = 128
GATE_LANES = LANES
VMEM_LIMIT = 56 * 1024 * 1024

_O_MLI = 4 * ML_WIDTH
_O_AQ = _O_MLI + 2 * ML_HEADS
_W_V, _W_O, _W_AQ, _W_AK, _W_AV, _W_GM, _W_GA, _W_IF = 2, 3, 4, 5, 6, 7, 8, 9


def _const_spec(shape, col_block=0):
    index = (0,) * (len(shape) - 1) + (col_block,)
    return pl.BlockSpec(shape, lambda *_: index, pipeline_mode=pl.Buffered(1))


def _params(*sem):
    return pltpu.CompilerParams(dimension_semantics=sem, vmem_limit_bytes=VMEM_LIMIT)


def _layer_norm(y, g, b):
    mu = jnp.mean(y, axis=-1, keepdims=True)
    yc = y - mu
    var = jnp.mean(yc * yc, axis=-1, keepdims=True)
    return yc * lax.rsqrt(var + LN_EPS) * g + b


def _sigmoid(x):
    return 1.0 / (1.0 + jnp.exp(-x))


_FF_CHUNKS = ((0, 768), (768, 1792), (1792, D_FF))
FFN_SUB_ROWS = 256
FFN_TILE_ROWS = 1024


def _ffn_ln_kernel(x_ref, wg_ref, wu_ref, wd_ref, g_ref, b_ref, o_ref):
    pending = None
    for rows in (slice(r, r + FFN_SUB_ROWS) for r in range(0, x_ref.shape[0], FFN_SUB_ROWS)):
        x = x_ref[rows, :]
        xb = x.astype(BF16)
        acc = None
        for s, e in _FF_CHUNKS:
            gate = jnp.dot(xb, wg_ref[:, s:e], preferred_element_type=F32)
            up = jnp.dot(xb, wu_ref[:, s:e], preferred_element_type=F32)
            hid = (gate * _sigmoid(gate) * up).astype(BF16)
            part = jnp.dot(hid, wd_ref[s:e, :], preferred_element_type=F32)
            acc = part if acc is None else acc + part
        if pending is not None:
            p_rows, p_y = pending
            o_ref[p_rows, :] = _layer_norm(p_y, g_ref[...], b_ref[...])
        pending = (rows, ALPHA * x + 0.5 * acc)
    p_rows, p_y = pending
    o_ref[p_rows, :] = _layer_norm(p_y, g_ref[...], b_ref[...])


def _ffn_ln(x, wgu, wd, g, b):
    n = x.shape[0]
    tm = min(FFN_TILE_ROWS, n)
    row = pl.BlockSpec((tm, D_MODEL), lambda i: (i, 0))
    return pl.pallas_call(
        _ffn_ln_kernel,
        grid=(n // tm,),
        in_specs=[row, _const_spec((D_MODEL, D_FF), 0), _const_spec((D_MODEL, D_FF), 1), _const_spec(wd.shape),
                  _const_spec(g.shape), _const_spec(b.shape)],
        out_specs=row,
        out_shape=jax.ShapeDtypeStruct((n, D_MODEL), F32),
        compiler_params=_params("parallel"),
        name="ffn_ln",
    )(x, wgu, wgu, wd, g, b)


COL_CHUNK = 256
CONV_ROWS = 16


def _chunk_jobs(hb_ref):
    def proj(w_ref, c):
        cols = slice(c * COL_CHUNK, (c + 1) * COL_CHUNK)
        return jnp.dot(hb_ref[...], w_ref[:, cols], preferred_element_type=F32), cols

    def plain(w_ref, o_ref):
        def job(c):
            y, cols = proj(w_ref, c)
            o_ref[0, :, cols] = y.astype(BF16)
        return job

    def gated(w_ref, o_ref):
        def job(c):
            y, cols = proj(w_ref, c)
            o_ref[0, :, cols] = _sigmoid(y).astype(BF16)
        return job

    return proj, plain, gated


def _interleave(*queues):
    order = []
    for i in range(max(len(q) for q in queues)):
        for q in queues:
            order += q[i:i + 1]
    return order


def _in_proj_ml_kernel(h_ref, *refs, tm, fresh):
    cprev_ref = None if fresh else refs[0]
    (wqk_ref, wv_ref, wo_ref, wif_ref, bif_ref, cw_ref, cb_ref,
     q_ref, kt_ref, v_ref, so_ref, gt_ref, conv_ref, hb_ref, raw_ref, kc_ref) = refs[0 if fresh else 1:]
    hb_ref[...] = h_ref[0].astype(BF16)
    proj, plain, gated = _chunk_jobs(hb_ref)

    @pl.when(pl.program_id(1) == 0)
    def _():
        raw_ref[0:8, :] = jnp.zeros((8, 2 * ML_WIDTH), F32)
        if not fresh:
            raw_ref[8 - (CONV_W - 1):8, :] = cprev_ref[0]

    def qk_raw(c):
        y, cols = proj(wqk_ref, c)
        raw_ref[8:8 + tm, cols] = y

    def qk_conv(c):
        cols = slice(c * COL_CHUNK, (c + 1) * COL_CHUNK)
        is_k = c * COL_CHUNK >= ML_WIDTH
        sub = lax.broadcasted_iota(jnp.int32, (8, COL_CHUNK), 0)
        for r in range(tm // CONV_ROWS):
            base = 8 + r * CONV_ROWS
            blocks = [raw_ref[base + 8 * i:base + 8 * i + 8, cols] for i in range(-1, CONV_ROWS // 8)]
            taps = [jnp.concatenate(blocks[1:], axis=0)]
            for j in range(1, CONV_W):
                taps.append(jnp.concatenate(
                    [pltpu.roll(jnp.where(sub < 8 - j, cur, prev), j, axis=0)
                     for prev, cur in zip(blocks[:-1], blocks[1:])], axis=0))
            acc = cb_ref[:, cols]
            for j in range(CONV_W):
                acc = acc + taps[j] * cw_ref[CONV_W - 1 - j, :, cols]
            qk = acc * _sigmoid(acc)
            rows = slice(r * CONV_ROWS, (r + 1) * CONV_ROWS)
            if is_k:
                kc_ref[rows, :] = qk * (ML_DHEAD ** -0.5)
            else:
                q_ref[0, rows, cols] = qk.astype(BF16)
        if is_k:
            kt_ref[0, cols.start - ML_WIDTH:cols.stop - ML_WIDTH, :] = kc_ref[...].astype(BF16).T

    n_chunks = D_MODEL // COL_CHUNK
    light = [functools.partial(job, c) for c in range(n_chunks)
             for job in (plain(wv_ref, v_ref), gated(wo_ref, so_ref))]
    qk_raw(0)
    for c in range(2 * n_chunks):
        if c + 1 < 2 * n_chunks:
            qk_raw(c + 1)
        light[c]()
        qk_conv(c)
    conv_ref[0] = raw_ref[8 + tm - (CONV_W - 1):8 + tm, :]
    raw_ref[0:8, :] = raw_ref[tm:tm + 8, :]

    zg = jnp.dot(hb_ref[...], wif_ref[...], preferred_element_type=F32) + bif_ref[...]
    lane = lax.broadcasted_iota(jnp.int32, zg.shape, 1)
    log_sig = jnp.minimum(zg, 0.0) - jnp.log(1.0 + jnp.exp(-jnp.abs(zg)))
    gt_ref[0] = jnp.where(lane < ML_HEADS, zg, log_sig)


def _in_proj_att_kernel(h_ref, waq_ref, wak_ref, wav_ref, wgm_ref, wga_ref,
                        aq_ref, akt_ref, av_ref, pk_ref, pv_ref, sgm_ref, sga_ref, hb_ref):
    hb_ref[...] = h_ref[0].astype(BF16)
    proj, plain, gated = _chunk_jobs(hb_ref)

    def att_k(c):
        y, cols = proj(wak_ref, c)
        pk_ref[0, :, cols] = y
        akt_ref[0, cols, :] = y.astype(BF16).T

    def att_v(c):
        y, cols = proj(wav_ref, c)
        pv_ref[0, :, cols] = y
        av_ref[0, :, cols] = y.astype(BF16)

    n_chunks = D_MODEL // COL_CHUNK
    heavy = [functools.partial(job, c) for c in range(n_chunks)
             for job in (gated(wgm_ref, sgm_ref), att_k, gated(wga_ref, sga_ref))]
    light = [functools.partial(job, c) for c in range(n_chunks) for job in (plain(waq_ref, aq_ref), att_v)]
    for job in _interleave(heavy, light):
        job()


def _in_proj(h, conv_prev, w):
    b, t, _ = h.shape
    tm = min(512, t)
    nt = t // tm
    keep = min(BAND, t)
    first_kept = (t - keep) // tm
    h_spec = pl.BlockSpec((1, tm, D_MODEL), lambda i, j: (i, j, 0))

    def tok(width, dtype):
        return (pl.BlockSpec((1, tm, width), lambda i, j: (i, j, 0)),
                jax.ShapeDtypeStruct((b, t, width), dtype))

    def kept(width):
        return (pl.BlockSpec((1, tm, width), lambda i, j: (i, jnp.maximum(j - first_kept, 0), 0)),
                jax.ShapeDtypeStruct((b, keep, width), F32))

    conv_spec = pl.BlockSpec((1, CONV_W - 1, 2 * ML_WIDTH), lambda i, j: (i, 0, 0))
    feature_major = (pl.BlockSpec((1, ATT_WIDTH, tm), lambda i, j: (i, 0, j)),
                     jax.ShapeDtypeStruct((b, ATT_WIDTH, t), BF16))
    ml_outs = [
        tok(ML_WIDTH, BF16), feature_major, tok(ML_WIDTH, BF16), tok(ML_WIDTH, BF16),
        tok(GATE_LANES, F32),
        (conv_spec, jax.ShapeDtypeStruct((b, CONV_W - 1, 2 * ML_WIDTH), F32)),
    ]
    w_in = w["w_in"]
    square = (D_MODEL, D_MODEL)
    ml_weights = [w_in, w_in, w_in, w_in, w["bif"], w["conv_w"], w["conv_b"]]
    fresh = conv_prev is None
    carried = [] if fresh else [conv_prev]
    ml = pl.pallas_call(
        functools.partial(_in_proj_ml_kernel, tm=tm, fresh=fresh),
        grid=(b, nt),
        in_specs=[h_spec] + [conv_spec] * len(carried)
                 + [_const_spec((D_MODEL, 2 * ML_WIDTH), 0), _const_spec(square, _W_V), _const_spec(square, _W_O),
                    _const_spec((D_MODEL, GATE_LANES), _W_IF * (D_MODEL // GATE_LANES))]
                 + [_const_spec(a.shape) for a in ml_weights[4:]],
        out_specs=[o[0] for o in ml_outs],
        out_shape=[o[1] for o in ml_outs],
        scratch_shapes=[pltpu.VMEM((tm, D_MODEL), BF16), pltpu.VMEM((tm + 8, 2 * ML_WIDTH), F32),
                        pltpu.VMEM((tm, COL_CHUNK), F32)],
        compiler_params=_params("parallel", "arbitrary"),
        name="in_proj_ml",
    )(h, *carried, *ml_weights)

    att_outs = [
        tok(ATT_WIDTH, BF16),
        feature_major,
        tok(ATT_WIDTH, BF16),
        kept(ATT_WIDTH), kept(ATT_WIDTH),
        tok(D_MODEL, BF16), tok(D_MODEL, BF16),
    ]
    att_weights = [w_in] * 5
    att = pl.pallas_call(
        _in_proj_att_kernel,
        grid=(b, nt),
        in_specs=[h_spec] + [_const_spec(square, c) for c in (_W_AQ, _W_AK, _W_AV, _W_GM, _W_GA)],
        out_specs=[o[0] for o in att_outs],
        out_shape=[o[1] for o in att_outs],
        scratch_shapes=[pltpu.VMEM((tm, D_MODEL), BF16)],
        compiler_params=_params("parallel", "arbitrary"),
        name="in_proj_att",
    )(h, *att_weights)
    return tuple(ml) + tuple(att)


def _split3(x):
    hi = x.astype(BF16)
    rest = x - hi.astype(F32)
    mid = rest.astype(BF16)
    return hi, mid, (rest - mid.astype(F32)).astype(BF16)


def _mlstm_kernel(q_ref, kt_ref, v_ref, so_ref, gt_ref, *refs, blk, fresh):
    c0_ref, n0_ref, m0_ref = (None, None, None) if fresh else refs[:3]
    (ng_ref, hm_ref, c_ref, n_ref, m_ref,
     ncol_ref, d_ref, s_ref, hh_ref, wi_ref, emt_ref, rs_ref) = refs[0 if fresh else 3:]
    heads = range(ML_HEADS)
    rb = min(ROW_BLOCK, blk)
    row_blocks = [slice(r * rb, (r + 1) * rb) for r in range(blk // rb)]
    head_cols = [slice(j * ML_DHEAD, (j + 1) * ML_DHEAD) for j in heads]

    @pl.when(pl.program_id(1) == 0)
    def _():
        if fresh:
            c_ref[...] = jnp.zeros_like(c_ref)
            ncol_ref[...] = jnp.zeros_like(ncol_ref)
            m_ref[...] = jnp.zeros_like(m_ref)
        else:
            c_ref[...] = c0_ref[...]
            ncol_ref[...] = n0_ref[0]
            m_ref[...] = m0_ref[...]

    gates = gt_ref[0]
    tril = (lax.broadcasted_iota(jnp.int32, (blk, blk), 1)
            <= lax.broadcasted_iota(jnp.int32, (blk, blk), 0)).astype(BF16)
    csum = sum(jnp.dot(tril, part, preferred_element_type=F32) for part in _split3(gates))
    bcum = pltpu.roll(csum, LANES - ML_HEADS, axis=1)
    li_rel = gates - bcum
    li_rel_t = li_rel.T
    m_prev = m_ref[0]
    b_last = bcum[blk - 1:blk, :]
    g_s = b_last + li_rel
    m_new = jnp.maximum(b_last + m_prev, jnp.max(g_s, axis=0, keepdims=True))
    decay = jnp.exp(b_last + m_prev - m_new)
    w_s_t = jnp.exp(g_s - m_new).T
    inter = bcum + m_prev

    def lane_of(x, j):
        lane = lax.broadcasted_iota(jnp.int32, x.shape, 1)
        return jnp.sum(jnp.where(lane == j, x, 0.0), axis=-1, keepdims=True)

    def spread(x, width):
        return x[:, :width] if width <= LANES else jnp.concatenate([x] * (width // LANES), axis=1)

    units = [(j, rows) for j in heads for rows in row_blocks]

    for j, rows in units:
        wi_ref[j, rows, :] = jnp.broadcast_to(lane_of(bcum[rows, :], j), (rb, LANES))
    m_prev_j = [lane_of(m_prev, j) for j in heads]
    for j, rows in units:
        causal = (lax.broadcasted_iota(jnp.int32, (rb, blk), 1)
                  <= lax.broadcasted_iota(jnp.int32, (rb, blk), 0) + rows.start)
        bcol = wi_ref[j, rows, :]
        dmat = jnp.where(causal, spread(bcol, blk) + li_rel_t[j:j + 1, :], -jnp.inf)
        icol = bcol + m_prev_j[j]
        m_t = jnp.maximum(icol, jnp.max(dmat, axis=-1, keepdims=True))
        d_ref[j, rows, :] = jnp.exp(dmat - spread(m_t, blk))
        wi_ref[j, rows, :] = jnp.exp(icol - m_t)
        emt_ref[j, rows, :] = jnp.exp(-m_t)

    for j in heads:
        qk = jnp.dot(q_ref[0, :, head_cols[j]], kt_ref[0, head_cols[j], :], preferred_element_type=F32)
        for rows in row_blocks:
            s = qk[rows, :] * d_ref[j, rows, :]
            rs_ref[j, rows, :] = jnp.broadcast_to(jnp.sum(s, axis=-1, keepdims=True), (rb, LANES))
            s_ref[j, rows, :] = s.astype(BF16)

    for j in heads:
        qj, vj = q_ref[0, :, head_cols[j]], v_ref[0, :, head_cols[j]]
        q_c = jnp.dot(qj, c_ref[0, j].astype(BF16), preferred_element_type=F32)
        s_v = jnp.dot(s_ref[j], vj, preferred_element_type=F32)
        q_n = jnp.dot(qj, ncol_ref[j].astype(BF16), preferred_element_type=F32)
        for rows in row_blocks:
            w_inter = wi_ref[j, rows, :]
            den = w_inter * q_n[rows, :] + rs_ref[j, rows, :]
            scale = 1.0 / jnp.maximum(jnp.abs(den), emt_ref[j, rows, :])
            hh = (spread(w_inter, ML_DHEAD) * q_c[rows, :] + s_v[rows, :]) * spread(scale, ML_DHEAD)
            hh_ref[j, rows, :] = hh
            wi_ref[j, rows, :] = jnp.broadcast_to(jnp.mean(hh, axis=-1, keepdims=True), (rb, LANES))
    for j, rows in units:
        hc = hh_ref[j, rows, :] - spread(wi_ref[j, rows, :], ML_DHEAD)
        var = jnp.mean(hc * hc, axis=-1, keepdims=True)
        hn = hc * lax.rsqrt(var + LN_EPS) * ng_ref[:, head_cols[j]] * so_ref[0, rows, head_cols[j]].astype(F32)
        hm_ref[0, rows, head_cols[j]] = hn.astype(BF16)

    ones = jnp.ones((blk, LANES), BF16)
    for j in heads:
        kw = (kt_ref[0, head_cols[j], :].astype(F32) * w_s_t[j:j + 1, :]).astype(BF16)
        d_j = lane_of(decay, j)
        c_ref[0, j] = d_j * c_ref[0, j] + jnp.dot(kw, v_ref[0, :, head_cols[j]], preferred_element_type=F32)
        ncol_ref[j] = d_j * ncol_ref[j] + jnp.dot(kw, ones, preferred_element_type=F32)
    m_ref[0] = m_new

    @pl.when(pl.program_id(1) == pl.num_programs(1) - 1)
    def _():
        for j in heads:
            n_ref[0, j:j + 1, :] = ncol_ref[j].T[0:1, :]


def _mlstm(q, kt, v, so, gates, state, norm_g):
    b, t, _ = q.shape
    blk = min(256, t)
    carried = [] if state is None else list(state)
    tok = pl.BlockSpec((1, blk, ML_WIDTH), lambda i, j: (i, j, 0))
    c_spec = pl.BlockSpec((1, ML_HEADS, ML_DHEAD, ML_DHEAD), lambda i, j: (i, 0, 0, 0))
    n_spec = pl.BlockSpec((1, ML_HEADS, ML_DHEAD), lambda i, j: (i, 0, 0))
    m_spec = pl.BlockSpec((1, 1, LANES), lambda i, j: (i, 0, 0))
    per_head = lambda width, dtype: pltpu.VMEM((ML_HEADS, blk, width), dtype)
    return pl.pallas_call(
        functools.partial(_mlstm_kernel, blk=blk, fresh=state is None),
        grid=(b, t // blk),
        in_specs=[tok, pl.BlockSpec((1, ML_WIDTH, blk), lambda i, j: (i, 0, j)), tok, tok,
                  pl.BlockSpec((1, blk, GATE_LANES), lambda i, j: (i, j, 0))]
                 + [c_spec, pl.BlockSpec((1, ML_HEADS, ML_DHEAD, LANES), lambda i, j: (i, 0, 0, 0)),
                    m_spec][:len(carried)]
                 + [_const_spec(norm_g.shape)],
        out_specs=[tok, c_spec, n_spec, m_spec],
        out_shape=[jax.ShapeDtypeStruct((b, t, ML_WIDTH), BF16),
                   jax.ShapeDtypeStruct((b, ML_HEADS, ML_DHEAD, ML_DHEAD), F32),
                   jax.ShapeDtypeStruct((b, ML_HEADS, ML_DHEAD), F32),
                   jax.ShapeDtypeStruct((b, 1, LANES), F32)],
        scratch_shapes=[pltpu.VMEM((ML_HEADS, ML_DHEAD, LANES), F32),
                        per_head(blk, F32), per_head(blk, BF16), per_head(ML_DHEAD, F32),
                        per_head(LANES, F32), per_head(LANES, F32), per_head(LANES, F32)],
        compiler_params=_params("parallel", "arbitrary"),
        name="mlstm",
    )(q, kt, v, so, gates, *carried, norm_g)


KEY_BLOCK = 256
N_KEY_BLOCKS = 3
N_STAGE = 4
ROW_BLOCK = 64


def _band_attn_kernel(q_ref, kt0_ref, kt1_ref, kt2_ref, v0_ref, v1_ref, v2_ref, bias_ref, o_ref,
                      s_ref, p_ref, mx_ref, linv_ref, even_ref, *, lead_blocks):
    kt_refs = (kt0_ref, kt1_ref, kt2_ref)
    v_refs = (v0_ref, v1_ref, v2_ref)
    qb = q_ref.shape[1]
    lane = lax.broadcasted_iota(jnp.int32, (qb, LANES), 1)
    low = lane < ATT_DHEAD

    def attend(first_slot):
        slots = range(first_slot, N_KEY_BLOCKS)
        k_lo = first_slot * KEY_BLOCK

        def scores(head):
            sl = slice(head // 2 * LANES, (head // 2 + 1) * LANES)
            q2 = q_ref[0, :, sl].astype(F32)
            qh = jnp.where(low if head % 2 == 0 else jnp.logical_not(low), q2, 0.0).astype(BF16)
            row_max = None
            for c in slots:
                cols = slice(c * KEY_BLOCK, (c + 1) * KEY_BLOCK)
                s = jnp.dot(qh, kt_refs[c][0, sl, :], preferred_element_type=F32) + bias_ref[head, :, cols]
                s_ref[head % N_STAGE, :, cols] = s
                part = jnp.maximum(s[:, :LANES], s[:, LANES:])
                row_max = part if row_max is None else jnp.maximum(row_max, part)
            mx_ref[head % N_STAGE] = row_max

        def weighted_values(head):
            sl = slice(head // 2 * LANES, (head // 2 + 1) * LANES)
            buf = head % N_STAGE
            rb = min(ROW_BLOCK, qb)
            for r in range(qb // rb):
                rows = slice(r * rb, (r + 1) * rb)
                m = jnp.max(mx_ref[buf, rows, :], axis=-1, keepdims=True)
                row_sum = None
                for c in slots:
                    cols = slice(c * KEY_BLOCK, (c + 1) * KEY_BLOCK)
                    p = jnp.exp2(s_ref[buf, rows, cols] - m)
                    p_ref[buf, rows, cols] = p.astype(BF16)
                    part = p[:, :LANES] + p[:, LANES:]
                    row_sum = part if row_sum is None else row_sum + part
                l = jnp.sum(row_sum, axis=-1, keepdims=True)
                linv_ref[buf, rows, :] = jnp.broadcast_to(1.0 / l, (rb, LANES))
            v2 = jnp.concatenate([v_refs[c][0, :, sl] for c in slots], axis=0)
            return jnp.dot(p_ref[buf, :, k_lo:], v2, preferred_element_type=F32) * linv_ref[buf]

        scores(0)
        for head in range(ATT_HEADS):
            if head + 1 < ATT_HEADS:
                scores(head + 1)
            out = weighted_values(head)
            if head % 2 == 0:
                even_ref[...] = out
            else:
                sl = slice(head // 2 * LANES, (head // 2 + 1) * LANES)
                o_ref[0, :, sl] = jnp.where(low, even_ref[...], out).astype(BF16)

    j = pl.program_id(1)
    for missing in range(lead_blocks, 0, -1):
        pl.when(j == lead_blocks - missing)(functools.partial(attend, missing))
    pl.when(j >= lead_blocks)(functools.partial(attend, 0))


BIAS_PERIOD = 1024


def _band_bias_kernel(diag_ref, o_ref, *, qb):
    nkeys = N_KEY_BLOCKS * KEY_BLOCK
    toep = pltpu.roll(jnp.broadcast_to(diag_ref[0], (qb, BIAS_PERIOD)), 0, 1, stride=1, stride_axis=0)
    row = lax.broadcasted_iota(jnp.int32, (qb, nkeys), 0)
    kpos = lax.broadcasted_iota(jnp.int32, (qb, nkeys), 1) - (nkeys - qb)
    chunk_start = (row // CHUNK) * CHUNK
    in_band = jnp.logical_and(kpos >= chunk_start - BAND, kpos < chunk_start + CHUNK)
    o_ref[0] = jnp.where(in_band, toep[:, :nkeys] * LOG2_E, NEG_INF)


def _band_bias(table, qb):
    nkeys = N_KEY_BLOCKS * KEY_BLOCK
    heads = table.shape[0]
    shift = np.arange(BIAS_PERIOD)
    shift = np.where(shift < nkeys, shift, shift - BIAS_PERIOD)
    rel_idx = np.clip(nkeys - qb - shift, -MAX_REL, MAX_REL) + MAX_REL
    diag = table[:, rel_idx].astype(F32).reshape(heads, 1, BIAS_PERIOD)
    return pl.pallas_call(
        functools.partial(_band_bias_kernel, qb=qb),
        grid=(heads,),
        in_specs=[pl.BlockSpec((1, 1, BIAS_PERIOD), lambda h: (h, 0, 0))],
        out_specs=pl.BlockSpec((1, qb, nkeys), lambda h: (h, 0, 0)),
        out_shape=jax.ShapeDtypeStruct((heads, qb, nkeys), F32),
        compiler_params=_params("parallel"),
        name="band_bias",
    )(diag)


def _cached_attn_kernel(q_ref, ktn_ref, vn_ref, kp_ref, vp_ref, bias_ref, o_ref):
    t, n_past = q_ref.shape[1], kp_ref.shape[1]
    low = lax.broadcasted_iota(jnp.int32, (t, LANES), 1) < ATT_DHEAD
    for g in range(ATT_HEADS // 2):
        sl = slice(g * LANES, (g + 1) * LANES)
        q2 = q_ref[0, :, sl].astype(F32)
        kp2, vp2 = kp_ref[0, :, sl].astype(BF16), vp_ref[0, :, sl].astype(BF16)
        halves = []
        for half in range(2):
            head = 2 * g + half
            qh = jnp.where(low if half == 0 else jnp.logical_not(low), q2, 0.0).astype(BF16)
            s_past = (lax.dot_general(qh, kp2, (((1,), (1,)), ((), ())), preferred_element_type=F32)
                      + bias_ref[head, :, :n_past])
            s_new = jnp.dot(qh, ktn_ref[0, sl, :], preferred_element_type=F32) + bias_ref[head, :, n_past:]
            m = jnp.maximum(jnp.max(s_past, axis=-1, keepdims=True), jnp.max(s_new, axis=-1, keepdims=True))
            p_past, p_new = jnp.exp2(s_past - m), jnp.exp2(s_new - m)
            l = jnp.sum(p_past, axis=-1, keepdims=True) + jnp.sum(p_new, axis=-1, keepdims=True)
            pv = (jnp.dot(p_past.astype(BF16), vp2, preferred_element_type=F32)
                  + jnp.dot(p_new.astype(BF16), vn_ref[0, :, sl], preferred_element_type=F32))
            halves.append(pv / l)
        o_ref[0, :, sl] = jnp.where(low, halves[0], halves[1]).astype(BF16)


def _cached_attn(q, kt_new, v_new, k_past, v_past, bias):
    b, t, _ = q.shape
    n_past = k_past.shape[1]
    tok = pl.BlockSpec((1, t, ATT_WIDTH), lambda i: (i, 0, 0))
    past = pl.BlockSpec((1, n_past, ATT_WIDTH), lambda i: (i, 0, 0))
    return pl.pallas_call(
        _cached_attn_kernel,
        grid=(b,),
        in_specs=[tok, pl.BlockSpec((1, ATT_WIDTH, t), lambda i: (i, 0, 0)), tok, past, past,
                  _const_spec(bias.shape)],
        out_specs=tok,
        out_shape=jax.ShapeDtypeStruct((b, t, ATT_WIDTH), BF16),
        compiler_params=_params("parallel"),
        name="cached_attn",
    )(q, kt_new, v_new, k_past, v_past, bias)


def _attention(q, kt, v, att_past, bias_tile):
    if att_past is None:
        return _band_attn(q, kt, v, bias_tile)
    k_past, v_past = att_past
    b, t, _ = q.shape
    n_past = k_past.shape[1]
    assert n_past == BAND and t <= CHUNK
    bias = bias_tile[:, :t, :n_past + t]
    return _cached_attn(q, kt, v, k_past.reshape(b, n_past, ATT_WIDTH), v_past.reshape(b, n_past, ATT_WIDTH), bias)


def _band_attn(q, kt, v, bias):
    b, tq, _ = q.shape
    qb = bias.shape[1]
    lead = N_KEY_BLOCKS - qb // KEY_BLOCK

    def kidx(s):
        return lambda i, j: jnp.maximum(j - lead + s, 0)

    kt_specs = [pl.BlockSpec((1, ATT_WIDTH, KEY_BLOCK), (lambda f: lambda i, j: (i, 0, f(i, j)))(kidx(s)))
                for s in range(N_KEY_BLOCKS)]
    v_specs = [pl.BlockSpec((1, KEY_BLOCK, ATT_WIDTH), (lambda f: lambda i, j: (i, f(i, j), 0))(kidx(s)))
               for s in range(N_KEY_BLOCKS)]
    tok = pl.BlockSpec((1, qb, ATT_WIDTH), lambda i, j: (i, j, 0))
    return pl.pallas_call(
        functools.partial(_band_attn_kernel, lead_blocks=lead),
        grid=(b, tq // qb),
        in_specs=[tok] + kt_specs + v_specs + [_const_spec(bias.shape)],
        out_specs=tok,
        out_shape=jax.ShapeDtypeStruct((b, tq, ATT_WIDTH), BF16),
        scratch_shapes=[pltpu.VMEM((N_STAGE, qb, N_KEY_BLOCKS * KEY_BLOCK), F32),
                        pltpu.VMEM((N_STAGE, qb, N_KEY_BLOCKS * KEY_BLOCK), BF16),
                        pltpu.VMEM((N_STAGE, qb, LANES), F32),
                        pltpu.VMEM((N_STAGE, qb, LANES), F32),
                        pltpu.VMEM((qb, LANES), F32)],
        compiler_params=_params("parallel", "arbitrary"),
        name="band_attn",
    )(q, kt, kt, kt, v, v, v, bias)


def _mix_ln_kernel(h_ref, hm_ref, oa_ref, sgm_ref, sga_ref, wm_ref, wa_ref, wout_ref, g_ref, b_ref, o_ref):
    pending = None
    for rows in (slice(r, r + FFN_SUB_ROWS) for r in range(0, h_ref.shape[0], FFN_SUB_ROWS)):
        y_ml = jnp.dot(hm_ref[rows, :], wm_ref[...], preferred_element_type=F32)
        y_att = jnp.dot(oa_ref[rows, :], wa_ref[...], preferred_element_type=F32)
        merged = sgm_ref[rows, :].astype(F32) * y_ml + sga_ref[rows, :].astype(F32) * y_att
        mix = jnp.dot(merged.astype(BF16), wout_ref[...], preferred_element_type=F32)
        if pending is not None:
            p_rows, p_y = pending
            o_ref[p_rows, :] = _layer_norm(p_y, g_ref[...], b_ref[...])
        pending = (rows, ALPHA * h_ref[rows, :] + mix)
    p_rows, p_y = pending
    o_ref[p_rows, :] = _layer_norm(p_y, g_ref[...], b_ref[...])


def _mix_ln(h, hm, oa, sgm, sga, wm, wa, wout, g, b):
    n = h.shape[0]
    tm = min(FFN_TILE_ROWS, n)
    row = pl.BlockSpec((tm, D_MODEL), lambda i: (i, 0))
    consts = [wm, wa, wout, g, b]
    return pl.pallas_call(
        _mix_ln_kernel,
        grid=(n // tm,),
        in_specs=[row] * 5 + [_const_spec(a.shape) for a in consts],
        out_specs=row,
        out_shape=jax.ShapeDtypeStruct((n, D_MODEL), F32),
        compiler_params=_params("parallel"),
        name="mix_ln",
    )(h, hm, oa, sgm, sga, *consts)


def _prep_weights(p):
    w_in = p["w_in"]
    w_att = w_in[:, _O_AQ:]
    col_scale = np.ones((w_att.shape[1],), np.float32)
    col_scale[:ATT_WIDTH] = ATT_DHEAD ** -0.5 * LOG2_E
    pad = jnp.zeros((D_MODEL, GATE_LANES - 2 * ML_HEADS), F32)
    w_all = jnp.concatenate([w_in[:, :_O_MLI], w_att * col_scale, w_in[:, _O_MLI:_O_AQ], pad], axis=1).astype(BF16)
    bif = jnp.concatenate([p["b_ml_i"], p["b_ml_f"], jnp.zeros((GATE_LANES - 2 * ML_HEADS,), F32)])
    row = lambda a: a.reshape(1, -1).astype(F32)
    return {
        "w_in": w_all,
        "bif": row(bif),
        "conv_w": jnp.broadcast_to(p["ml_conv_w"].astype(F32)[:, None, :], (CONV_W, CONV_ROWS, 2 * ML_WIDTH)),
        "conv_b": jnp.broadcast_to(row(p["ml_conv_b"]), (CONV_ROWS, 2 * ML_WIDTH)),
        "norm_g": row(p["ml_norm_g"]),
        "wm": p["w_ml_proj"].astype(BF16),
        "wa": p["w_att_proj"].astype(BF16),
        "wout": p["w_out"].astype(BF16),
        "ffn1": (p["ffn1_w_gu"].astype(BF16), p["ffn1_w_down"].astype(BF16)),
        "ffn2": (p["ffn2_w_gu"].astype(BF16), p["ffn2_w_down"].astype(BF16)),
        "ln1": (row(p["ln1_g"]), row(p["ln1_b"])),
        "ln2": (row(p["ln2_g"]), row(p["ln2_b"])),
        "ln3": (row(p["ln3_g"]), row(p["ln3_b"])),
        "bias_tile": _band_bias(p["att_rel_bias"], KEY_BLOCK),
    }


def _encoder_layer(x, w, conv_prev, ml_state, att_past):
    b, t, _ = x.shape
    n = b * t
    flat = lambda a: a.reshape(n, a.shape[-1])
    h1 = _ffn_ln(flat(x), *w["ffn1"], *w["ln1"])
    (q, k, v, so, gates, new_conv, aq, akt, av, k_rows, v_rows, sgm, sga) = _in_proj(
        h1.reshape(b, t, D_MODEL), conv_prev, w)

    if ml_state is not None:
        c0, n0, m0 = ml_state
        m0p = jnp.pad(m0.astype(F32), ((0, 0), (0, LANES - ML_HEADS))).reshape(b, 1, LANES)
        n0_rep = jnp.broadcast_to(n0.astype(F32)[..., None], n0.shape + (LANES,))
        ml_state = (c0.astype(F32), n0_rep, m0p)
    hm, c1, n1, m1p = _mlstm(q, k, v, so, gates, ml_state, w["norm_g"])
    m1 = m1p[:, 0, :ML_HEADS]

    oa = _attention(aq, akt, av, att_past, w["bias_tile"])
    h2 = _mix_ln(h1, flat(hm), flat(oa), flat(sgm), flat(sga), w["wm"], w["wa"], w["wout"], *w["ln2"])
    y = _ffn_ln(h2, *w["ffn2"], *w["ln3"]).reshape(b, t, D_MODEL)
    keep = k_rows.shape[1]
    state = (new_conv, c1, n1, m1,
             k_rows.reshape(b, keep, ATT_HEADS, ATT_DHEAD), v_rows.reshape(b, keep, ATT_HEADS, ATT_DHEAD))
    return y, state


def kernel(x_prompt, x_sample, state_ml_conv, state_ml_C, state_ml_n, state_ml_m, cache_att_k, cache_att_v,
           w_in, b_ml_i, b_ml_f, ml_conv_w, ml_conv_b, ml_norm_g, att_rel_bias, w_ml_proj, w_att_proj, w_out,
           ffn1_w_gu, ffn1_w_down, ffn2_w_gu, ffn2_w_down, ln1_g, ln1_b, ln2_g, ln2_b, ln3_g, ln3_b):
    depth = w_in.shape[0]
    y_p, y_s = x_prompt, x_sample
    p_states, s_states = [], []
    for l in range(depth):
        w = _prep_weights({
            "w_in": w_in[l], "b_ml_i": b_ml_i[l], "b_ml_f": b_ml_f[l], "ml_conv_w": ml_conv_w[l],
            "ml_conv_b": ml_conv_b[l], "ml_norm_g": ml_norm_g[l], "att_rel_bias": att_rel_bias[l],
            "w_ml_proj": w_ml_proj[l], "w_att_proj": w_att_proj[l], "w_out": w_out[l],
            "ffn1_w_gu": ffn1_w_gu[l], "ffn1_w_down": ffn1_w_down[l], "ffn2_w_gu": ffn2_w_gu[l],
            "ffn2_w_down": ffn2_w_down[l], "ln1_g": ln1_g[l], "ln1_b": ln1_b[l], "ln2_g": ln2_g[l],
            "ln2_b": ln2_b[l], "ln3_g": ln3_g[l], "ln3_b": ln3_b[l]})
        y_p, st_p = _encoder_layer(y_p, w, None, None, None)
        y_s, st_s = _encoder_layer(y_s, w, state_ml_conv[l],
                                   (state_ml_C[l], state_ml_n[l], state_ml_m[l]),
                                   (cache_att_k[l], cache_att_v[l]))
        p_states.append(st_p)
        s_states.append(st_s)
    p_conv, p_c, p_n, p_m, p_k, p_v = [jnp.stack(s) for s in zip(*p_states)]
    s_conv, s_c, s_n, s_m, s_k, s_v = [jnp.stack(s) for s in zip(*s_states)]
    return (y_p, y_s, p_conv, s_conv, p_c, s_c, p_n, s_n, p_m, s_m, p_k, s_k, p_v, s_v)
```

```python
import functools

import jax
import jax.numpy as jnp
import numpy as np
from jax import lax
from jax.experimental import pallas as pl
from jax.experimental.pallas import tpu as pltpu

F32 = jnp.float32
BF16 = jnp.bfloat16

D_MODEL = 1024
CHUNK = 64
ML_HEADS = 4
ML_DHEAD = D_MODEL // ML_HEADS
ML_WIDTH = ML_HEADS * ML_DHEAD
CONV_W = 4
ATT_HEADS = 16
ATT_DHEAD = D_MODEL // ATT_HEADS
ATT_WIDTH = ATT_HEADS * ATT_DHEAD
BAND_CHUNKS = 8
BAND = BAND_CHUNKS * CHUNK
MAX_REL = 128
D_FF = ((8 * D_MODEL) // 3 + 127) // 128 * 128
ALPHA = 2.0 ** 0.25
LN_EPS = 1e-5
NEG_INF = -1e30
LOG2_E = 1.4426950408889634

LANES = 128
GATE_LANES = LANES
VMEM_LIMIT = 56 * 1024 * 1024

_O_MLI = 4 * ML_WIDTH
_O_AQ = _O_MLI + 2 * ML_HEADS
_W_V, _W_O, _W_AQ, _W_AK, _W_AV, _W_GM, _W_GA = 2, 3, 4, 5, 6, 7, 8


def _const_spec(shape, col_block=0):
    index = (0,) * (len(shape) - 1) + (col_block,)
    return pl.BlockSpec(shape, lambda *_: index, pipeline_mode=pl.Buffered(1))


def _params(*sem):
    return pltpu.CompilerParams(dimension_semantics=sem, vmem_limit_bytes=VMEM_LIMIT)


def _layer_norm(y, g, b):
    mu = jnp.mean(y, axis=-1, keepdims=True)
    yc = y - mu
    var = jnp.mean(yc * yc, axis=-1, keepdims=True)
    return yc * lax.rsqrt(var + LN_EPS) * g + b


def _sigmoid(x):
    return 1.0 / (1.0 + jnp.exp(-x))


_FF_CHUNKS = ((0, 768), (768, 1792), (1792, D_FF))
FFN_SUB_ROWS = 256
FFN_TILE_ROWS = 1024


def _ffn_ln_kernel(x_ref, wg_ref, wu_ref, wd_ref, g_ref, b_ref, o_ref):
    pending = None
    for rows in (slice(r, r + FFN_SUB_ROWS) for r in range(0, x_ref.shape[0], FFN_SUB_ROWS)):
        x = x_ref[rows, :]
        xb = x.astype(BF16)
        acc = None
        for s, e in _FF_CHUNKS:
            gate = jnp.dot(xb, wg_ref[:, s:e], preferred_element_type=F32)
            up = jnp.dot(xb, wu_ref[:, s:e], preferred_element_type=F32)
            hid = (gate * _sigmoid(gate) * up).astype(BF16)
            part = jnp.dot(hid, wd_ref[s:e, :], preferred_element_type=F32)
            acc = part if acc is None else acc + part
        if pending is not None:
            p_rows, p_y = pending
            o_ref[p_rows, :] = _layer_norm(p_y, g_ref[...], b_ref[...])
        pending = (rows, ALPHA * x + 0.5 * acc)
    p_rows, p_y = pending
    o_ref[p_rows, :] = _layer_norm(p_y, g_ref[...], b_ref[...])


def _ffn_ln(x, wgu, wd, g, b):
    n = x.shape[0]
    tm = min(FFN_TILE_ROWS, n)
    row = pl.BlockSpec((tm, D_MODEL), lambda i: (i, 0))
    return pl.pallas_call(
        _ffn_ln_kernel,
        grid=(n // tm,),
        in_specs=[row, _const_spec((D_MODEL, D_FF), 0), _const_spec((D_MODEL, D_FF), 1), _const_spec(wd.shape),
                  _const_spec(g.shape), _const_spec(b.shape)],
        out_specs=row,
        out_shape=jax.ShapeDtypeStruct((n, D_MODEL), F32),
        compiler_params=_params("parallel"),
        name="ffn_ln",
    )(x, wgu, wgu, wd, g, b)


PROJ_TILE_ROWS = 512
COL_CHUNK = 256
CONV_ROWS = 16


def _chunk_jobs(h_ref, hb_ref):
    seqs, frames, _ = h_ref.shape
    for s in range(seqs):
        hb_ref[s * frames:(s + 1) * frames, :] = h_ref[s].astype(BF16)

    def proj(w_ref, c):
        cols = slice(c * COL_CHUNK, (c + 1) * COL_CHUNK)
        return jnp.dot(hb_ref[...], w_ref[:, cols], preferred_element_type=F32), cols

    def scatter(o_ref, cols, y):
        for s in range(seqs):
            o_ref[s, :, cols] = y[s * frames:(s + 1) * frames, :]

    def plain(w_ref, o_ref):
        def job(c):
            y, cols = proj(w_ref, c)
            scatter(o_ref, cols, y.astype(BF16))
        return job

    def gated(w_ref, o_ref):
        def job(c):
            y, cols = proj(w_ref, c)
            scatter(o_ref, cols, _sigmoid(y).astype(BF16))
        return job

    return proj, scatter, plain, gated


def _interleave(*queues):
    order = []
    for i in range(max(len(q) for q in queues)):
        for q in queues:
            order += q[i:i + 1]
    return order


def _in_proj_ml_kernel(h_ref, *refs, fresh):
    cprev_ref = None if fresh else refs[0]
    (wqk_ref, wv_ref, wo_ref, wif_ref, bif_ref, cw_ref, cb_ref,
     q_ref, kt_ref, v_ref, so_ref, gt_ref, conv_ref, hb_ref, raw_ref, kc_ref) = refs[0 if fresh else 1:]
    seqs, frames, _ = h_ref.shape
    proj, scatter, plain, gated = _chunk_jobs(h_ref, hb_ref)

    @pl.when(pl.program_id(1) == 0)
    def _():
        for s in range(seqs):
            raw_ref[s, 0:8, :] = jnp.zeros((8, 2 * ML_WIDTH), F32)
            if not fresh:
                raw_ref[s, 8 - (CONV_W - 1):8, :] = cprev_ref[s]

    def qk_raw(c):
        y, cols = proj(wqk_ref, c)
        for s in range(seqs):
            raw_ref[s, 8:8 + frames, cols] = y[s * frames:(s + 1) * frames, :]

    def qk_conv(c):
        cols = slice(c * COL_CHUNK, (c + 1) * COL_CHUNK)
        is_k = c * COL_CHUNK >= ML_WIDTH
        sub = lax.broadcasted_iota(jnp.int32, (8, COL_CHUNK), 0)
        for s in range(seqs):
            for r in range(frames // CONV_ROWS):
                base = 8 + r * CONV_ROWS
                blocks = [raw_ref[s, base + 8 * i:base + 8 * i + 8, cols] for i in range(-1, CONV_ROWS // 8)]
                taps = [jnp.concatenate(blocks[1:], axis=0)]
                for j in range(1, CONV_W):
                    taps.append(jnp.concatenate(
                        [pltpu.roll(jnp.where(sub < 8 - j, cur, prev), j, axis=0)
                         for prev, cur in zip(blocks[:-1], blocks[1:])], axis=0))
                acc = cb_ref[:, cols]
                for j in range(CONV_W):
                    acc = acc + taps[j] * cw_ref[CONV_W - 1 - j, :, cols]
                qk = acc * _sigmoid(acc)
                rows = slice(r * CONV_ROWS, (r + 1) * CONV_ROWS)
                if is_k:
                    kc_ref[rows, :] = qk * (ML_DHEAD ** -0.5)
                else:
                    q_ref[s, rows, cols] = qk.astype(BF16)
            if is_k:
                kt_ref[s, cols.start - ML_WIDTH:cols.stop - ML_WIDTH, :] = kc_ref[...].astype(BF16).T

    n_chunks = D_MODEL // COL_CHUNK
    light = [functools.partial(job, c) for c in range(n_chunks)
             for job in (plain(wv_ref, v_ref), gated(wo_ref, so_ref))]
    qk_raw(0)
    for c in range(2 * n_chunks):
        if c + 1 < 2 * n_chunks:
            qk_raw(c + 1)
        light[c]()
        qk_conv(c)
    for s in range(seqs):
        conv_ref[s] = raw_ref[s, 8 + frames - (CONV_W - 1):8 + frames, :]
        raw_ref[s, 0:8, :] = raw_ref[s, frames:frames + 8, :]

    zg = jnp.dot(hb_ref[...], wif_ref[...], preferred_element_type=F32) + bif_ref[...]
    lane = lax.broadcasted_iota(jnp.int32, zg.shape, 1)
    log_sig = jnp.minimum(zg, 0.0) - jnp.log(1.0 + jnp.exp(-jnp.abs(zg)))
    scatter(gt_ref, slice(None), jnp.where(lane < ML_HEADS, zg, log_sig))


def _in_proj_att_kernel(h_ref, waq_ref, wak_ref, wav_ref, wgm_ref, wga_ref,
                        aq_ref, akt_ref, av_ref, pk_ref, pv_ref, sgm_ref, sga_ref, hb_ref):
    seqs, frames, _ = h_ref.shape
    proj, scatter, plain, gated = _chunk_jobs(h_ref, hb_ref)

    def att_k(c):
        y, cols = proj(wak_ref, c)
        scatter(pk_ref, cols, y)
        yb = y.astype(BF16)
        for s in range(seqs):
            akt_ref[s, cols, :] = yb[s * frames:(s + 1) * frames, :].T

    def att_v(c):
        y, cols = proj(wav_ref, c)
        scatter(pv_ref, cols, y)
        scatter(av_ref, cols, y.astype(BF16))

    n_chunks = D_MODEL // COL_CHUNK
    heavy = [functools.partial(job, c) for c in range(n_chunks)
             for job in (gated(wgm_ref, sgm_ref), att_k, gated(wga_ref, sga_ref))]
    light = [functools.partial(job, c) for c in range(n_chunks) for job in (plain(waq_ref, aq_ref), att_v)]
    for job in _interleave(heavy, light):
        job()


def _in_proj(h, conv_prev, w):
    b, t, _ = h.shape
    tm = min(PROJ_TILE_ROWS, t)
    sb = max(1, min(b, PROJ_TILE_ROWS // t))
    nt = t // tm
    keep = min(BAND, t)
    first_kept = (t - keep) // tm
    h_spec = pl.BlockSpec((sb, tm, D_MODEL), lambda i, j: (i, j, 0))

    def tok(width, dtype):
        return (pl.BlockSpec((sb, tm, width), lambda i, j: (i, j, 0)),
                jax.ShapeDtypeStruct((b, t, width), dtype))

    def kept(width):
        return (pl.BlockSpec((sb, tm, width), lambda i, j: (i, jnp.maximum(j - first_kept, 0), 0)),
                jax.ShapeDtypeStruct((b, keep, width), F32))

    conv_spec = pl.BlockSpec((sb, CONV_W - 1, 2 * ML_WIDTH), lambda i, j: (i, 0, 0))
    feature_major = (pl.BlockSpec((sb, ATT_WIDTH, tm), lambda i, j: (i, 0, j)),
                     jax.ShapeDtypeStruct((b, ATT_WIDTH, t), BF16))
    ml_outs = [
        tok(ML_WIDTH, BF16), feature_major, tok(ML_WIDTH, BF16), tok(ML_WIDTH, BF16),
        tok(GATE_LANES, F32),
        (conv_spec, jax.ShapeDtypeStruct((b, CONV_W - 1, 2 * ML_WIDTH), F32)),
    ]
    w_in = w["w_in"]
    square = (D_MODEL, D_MODEL)
    ml_weights = [w_in, w_in, w_in, w["w_gate"], w["bif"], w["conv_w"], w["conv_b"]]
    fresh = conv_prev is None
    carried = [] if fresh else [conv_prev]
    ml = pl.pallas_call(
        functools.partial(_in_proj_ml_kernel, fresh=fresh),
        grid=(b // sb, nt),
        in_specs=[h_spec] + [conv_spec] * len(carried)
                 + [_const_spec((D_MODEL, 2 * ML_WIDTH), 0), _const_spec(square, _W_V), _const_spec(square, _W_O)]
                 + [_const_spec(a.shape) for a in ml_weights[3:]],
        out_specs=[o[0] for o in ml_outs],
        out_shape=[o[1] for o in ml_outs],
        scratch_shapes=[pltpu.VMEM((sb * tm, D_MODEL), BF16), pltpu.VMEM((sb, tm + 8, 2 * ML_WIDTH), F32),
                        pltpu.VMEM((tm, COL_CHUNK), F32)],
        compiler_params=_params("parallel", "arbitrary"),
        name="in_proj_ml",
    )(h, *carried, *ml_weights)

    att_outs = [
        tok(ATT_WIDTH, BF16),
        feature_major,
        tok(ATT_WIDTH, BF16),
        kept(ATT_WIDTH), kept(ATT_WIDTH),
        tok(D_MODEL, BF16), tok(D_MODEL, BF16),
    ]
    att_weights = [w_in] * 5
    att = pl.pallas_call(
        _in_proj_att_kernel,
        grid=(b // sb, nt),
        in_specs=[h_spec] + [_const_spec(square, c) for c in (_W_AQ, _W_AK, _W_AV, _W_GM, _W_GA)],
        out_specs=[o[0] for o in att_outs],
        out_shape=[o[1] for o in att_outs],
        scratch_shapes=[pltpu.VMEM((sb * tm, D_MODEL), BF16)],
        compiler_params=_params("parallel", "arbitrary"),
        name="in_proj_att",
    )(h, *att_weights)
    return tuple(ml) + tuple(att)


def _split3(x):
    hi = x.astype(BF16)
    rest = x - hi.astype(F32)
    mid = rest.astype(BF16)
    return hi, mid, (rest - mid.astype(F32)).astype(BF16)


def _mlstm_kernel(q_ref, kt_ref, v_ref, so_ref, gt_ref, *refs, blk, fresh):
    c0_ref, n0_ref, m0_ref = (None, None, None) if fresh else refs[:3]
    (ng_ref, hm_ref, c_ref, n_ref, m_ref,
     ncol_ref, d_ref, s_ref, hh_ref, wi_ref, emt_ref, rs_ref) = refs[0 if fresh else 3:]
    heads = range(ML_HEADS)
    rb = min(ROW_BLOCK, blk)
    row_blocks = [slice(r * rb, (r + 1) * rb) for r in range(blk // rb)]
    head_cols = [slice(j * ML_DHEAD, (j + 1) * ML_DHEAD) for j in heads]

    @pl.when(pl.program_id(1) == 0)
    def _():
        if fresh:
            c_ref[...] = jnp.zeros_like(c_ref)
            ncol_ref[...] = jnp.zeros_like(ncol_ref)
            m_ref[...] = jnp.zeros_like(m_ref)
        else:
            c_ref[...] = c0_ref[...]
            ncol_ref[...] = n0_ref[0]
            m_ref[...] = m0_ref[...]

    gates = gt_ref[0]
    tril = (lax.broadcasted_iota(jnp.int32, (blk, blk), 1)
            <= lax.broadcasted_iota(jnp.int32, (blk, blk), 0)).astype(BF16)
    csum = sum(jnp.dot(tril, part, preferred_element_type=F32) for part in _split3(gates))
    bcum = pltpu.roll(csum, LANES - ML_HEADS, axis=1)
    li_rel = gates - bcum
    li_rel_t = li_rel.T
    m_prev = m_ref[0]
    b_last = bcum[blk - 1:blk, :]
    g_s = b_last + li_rel
    m_new = jnp.maximum(b_last + m_prev, jnp.max(g_s, axis=0, keepdims=True))
    decay = jnp.exp(b_last + m_prev - m_new)
    w_s_t = jnp.exp(g_s - m_new).T
    inter = bcum + m_prev

    def lane_of(x, j):
        lane = lax.broadcasted_iota(jnp.int32, x.shape, 1)
        return jnp.sum(jnp.where(lane == j, x, 0.0), axis=-1, keepdims=True)

    def spread(x, width):
        return x[:, :width] if width <= LANES else jnp.concatenate([x] * (width // LANES), axis=1)

    units = [(j, rows) for j in heads for rows in row_blocks]

    for j, rows in units:
        wi_ref[j, rows, :] = jnp.broadcast_to(lane_of(bcum[rows, :], j), (rb, LANES))
    m_prev_j = [lane_of(m_prev, j) for j in heads]
    for j, rows in units:
        causal = (lax.broadcasted_iota(jnp.int32, (rb, blk), 1)
                  <= lax.broadcasted_iota(jnp.int32, (rb, blk), 0) + rows.start)
        bcol = wi_ref[j, rows, :]
        dmat = jnp.where(causal, spread(bcol, blk) + li_rel_t[j:j + 1, :], -jnp.inf)
        icol = bcol + m_prev_j[j]
        m_t = jnp.maximum(icol, jnp.max(dmat, axis=-1, keepdims=True))
        d_ref[j, rows, :] = jnp.exp(dmat - spread(m_t, blk))
        wi_ref[j, rows, :] = jnp.exp(icol - m_t)
        emt_ref[j, rows, :] = jnp.exp(-m_t)

    for j in heads:
        qk = jnp.dot(q_ref[0, :, head_cols[j]], kt_ref[0, head_cols[j], :], preferred_element_type=F32)
        for rows in row_blocks:
            s = qk[rows, :] * d_ref[j, rows, :]
            rs_ref[j, rows, :] = jnp.broadcast_to(jnp.sum(s, axis=-1, keepdims=True), (rb, LANES))
            s_ref[j, rows, :] = s.astype(BF16)

    for j in heads:
        qj, vj = q_ref[0, :, head_cols[j]], v_ref[0, :, head_cols[j]]
        q_c = jnp.dot(qj, c_ref[0, j].astype(BF16), preferred_element_type=F32)
        s_v = jnp.dot(s_ref[j], vj, preferred_element_type=F32)
        q_n = jnp.dot(qj, ncol_ref[j].astype(BF16), preferred_element_type=F32)
        for rows in row_blocks:
            w_inter = wi_ref[j, rows, :]
            den = w_inter * q_n[rows, :] + rs_ref[j, rows, :]
            scale = 1.0 / jnp.maximum(jnp.abs(den), emt_ref[j, rows, :])
            hh = (spread(w_inter, ML_DHEAD) * q_c[rows, :] + s_v[rows, :]) * spread(scale, ML_DHEAD)
            hh_ref[j, rows, :] = hh
            wi_ref[j, rows, :] = jnp.broadcast_to(jnp.mean(hh, axis=-1, keepdims=True), (rb, LANES))
    for j, rows in units:
        hc = hh_ref[j, rows, :] - spread(wi_ref[j, rows, :], ML_DHEAD)
        var = jnp.mean(hc * hc, axis=-1, keepdims=True)
        hn = hc * lax.rsqrt(var + LN_EPS) * ng_ref[:, head_cols[j]] * so_ref[0, rows, head_cols[j]].astype(F32)
        hm_ref[0, rows, head_cols[j]] = hn.astype(BF16)

    ones = jnp.ones((blk, LANES), BF16)
    for j in heads:
        kw = (kt_ref[0, head_cols[j], :].astype(F32) * w_s_t[j:j + 1, :]).astype(BF16)
        d_j = lane_of(decay, j)
        c_ref[0, j] = d_j * c_ref[0, j] + jnp.dot(kw, v_ref[0, :, head_cols[j]], preferred_element_type=F32)
        ncol_ref[j] = d_j * ncol_ref[j] + jnp.dot(kw, ones, preferred_element_type=F32)
    m_ref[0] = m_new

    @pl.when(pl.program_id(1) == pl.num_programs(1) - 1)
    def _():
        for j in heads:
            n_ref[0, j:j + 1, :] = ncol_ref[j].T[0:1, :]


def _mlstm(q, kt, v, so, gates, state, norm_g):
    b, t, _ = q.shape
    blk = min(256, t)
    carried = [] if state is None else list(state)
    tok = pl.BlockSpec((1, blk, ML_WIDTH), lambda i, j: (i, j, 0))
    c_spec = pl.BlockSpec((1, ML_HEADS, ML_DHEAD, ML_DHEAD), lambda i, j: (i, 0, 0, 0))
    n_spec = pl.BlockSpec((1, ML_HEADS, ML_DHEAD), lambda i, j: (i, 0, 0))
    m_spec = pl.BlockSpec((1, 1, LANES), lambda i, j: (i, 0, 0))
    per_head = lambda width, dtype: pltpu.VMEM((ML_HEADS, blk, width), dtype)
    return pl.pallas_call(
        functools.partial(_mlstm_kernel, blk=blk, fresh=state is None),
        grid=(b, t // blk),
        in_specs=[tok, pl.BlockSpec((1, ML_WIDTH, blk), lambda i, j: (i, 0, j)), tok, tok,
                  pl.BlockSpec((1, blk, GATE_LANES), lambda i, j: (i, j, 0))]
                 + [c_spec, pl.BlockSpec((1, ML_HEADS, ML_DHEAD, LANES), lambda i, j: (i, 0, 0, 0)),
                    m_spec][:len(carried)]
                 + [_const_spec(norm_g.shape)],
        out_specs=[tok, c_spec, n_spec, m_spec],
        out_shape=[jax.ShapeDtypeStruct((b, t, ML_WIDTH), BF16),
                   jax.ShapeDtypeStruct((b, ML_HEADS, ML_DHEAD, ML_DHEAD), F32),
                   jax.ShapeDtypeStruct((b, ML_HEADS, ML_DHEAD), F32),
                   jax.ShapeDtypeStruct((b, 1, LANES), F32)],
        scratch_shapes=[pltpu.VMEM((ML_HEADS, ML_DHEAD, LANES), F32),
                        per_head(blk, F32), per_head(blk, BF16), per_head(ML_DHEAD, F32),
                        per_head(LANES, F32), per_head(LANES, F32), per_head(LANES, F32)],
        compiler_params=_params("parallel", "arbitrary"),
        name="mlstm",
    )(q, kt, v, so, gates, *carried, norm_g)


KEY_BLOCK = 256
N_KEY_BLOCKS = 3
N_STAGE = 4
ROW_BLOCK = 64


def _band_attn_kernel(q_ref, kt0_ref, kt1_ref, kt2_ref, v0_ref, v1_ref, v2_ref, bias_ref, o_ref,
                      s_ref, p_ref, mx_ref, linv_ref, even_ref, *, lead_blocks):
    kt_refs = (kt0_ref, kt1_ref, kt2_ref)
    v_refs = (v0_ref, v1_ref, v2_ref)
    qb = q_ref.shape[1]
    lane = lax.broadcasted_iota(jnp.int32, (qb, LANES), 1)
    low = lane < ATT_DHEAD

    def attend(first_slot):
        slots = range(first_slot, N_KEY_BLOCKS)
        k_lo = first_slot * KEY_BLOCK

        def scores(head):
            sl = slice(head // 2 * LANES, (head // 2 + 1) * LANES)
            q2 = q_ref[0, :, sl].astype(F32)
            qh = jnp.where(low if head % 2 == 0 else jnp.logical_not(low), q2, 0.0).astype(BF16)
            row_max = None
            for c in slots:
                cols = slice(c * KEY_BLOCK, (c + 1) * KEY_BLOCK)
                s = jnp.dot(qh, kt_refs[c][0, sl, :], preferred_element_type=F32) + bias_ref[head, :, cols]
                s_ref[head % N_STAGE, :, cols] = s
                part = jnp.maximum(s[:, :LANES], s[:, LANES:])
                row_max = part if row_max is None else jnp.maximum(row_max, part)
            mx_ref[head % N_STAGE] = row_max

        def weighted_values(head):
            sl = slice(head // 2 * LANES, (head // 2 + 1) * LANES)
            buf = head % N_STAGE
            rb = min(ROW_BLOCK, qb)
            for r in range(qb // rb):
                rows = slice(r * rb, (r + 1) * rb)
                m = jnp.max(mx_ref[buf, rows, :], axis=-1, keepdims=True)
                row_sum = None
                for c in slots:
                    cols = slice(c * KEY_BLOCK, (c + 1) * KEY_BLOCK)
                    p = jnp.exp2(s_ref[buf, rows, cols] - m)
                    p_ref[buf, rows, cols] = p.astype(BF16)
                    part = p[:, :LANES] + p[:, LANES:]
                    row_sum = part if row_sum is None else row_sum + part
                l = jnp.sum(row_sum, axis=-1, keepdims=True)
                linv_ref[buf, rows, :] = jnp.broadcast_to(1.0 / l, (rb, LANES))
            v2 = jnp.concatenate([v_refs[c][0, :, sl] for c in slots], axis=0)
            return jnp.dot(p_ref[buf, :, k_lo:], v2, preferred_element_type=F32) * linv_ref[buf]

        scores(0)
        for head in range(ATT_HEADS):
            if head + 1 < ATT_HEADS:
                scores(head + 1)
            out = weighted_values(head)
            if head % 2 == 0:
                even_ref[...] = out
            else:
                sl = slice(head // 2 * LANES, (head // 2 + 1) * LANES)
                o_ref[0, :, sl] = jnp.where(low, even_ref[...], out).astype(BF16)

    j = pl.program_id(1)
    for missing in range(lead_blocks, 0, -1):
        pl.when(j == lead_blocks - missing)(functools.partial(attend, missing))
    pl.when(j >= lead_blocks)(functools.partial(attend, 0))


BIAS_PERIOD = 1024


def _band_bias_kernel(diag_ref, o_ref, *, qb):
    nkeys = N_KEY_BLOCKS * KEY_BLOCK
    toep = pltpu.roll(jnp.broadcast_to(diag_ref[0], (qb, BIAS_PERIOD)), 0, 1, stride=1, stride_axis=0)
    row = lax.broadcasted_iota(jnp.int32, (qb, nkeys), 0)
    kpos = lax.broadcasted_iota(jnp.int32, (qb, nkeys), 1) - (nkeys - qb)
    chunk_start = (row // CHUNK) * CHUNK
    in_band = jnp.logical_and(kpos >= chunk_start - BAND, kpos < chunk_start + CHUNK)
    o_ref[0] = jnp.where(in_band, toep[:, :nkeys] * LOG2_E, NEG_INF)


def _band_bias(table, qb):
    nkeys = N_KEY_BLOCKS * KEY_BLOCK
    heads = table.shape[0]
    shift = np.arange(BIAS_PERIOD)
    shift = np.where(shift < nkeys, shift, shift - BIAS_PERIOD)
    rel_idx = np.clip(nkeys - qb - shift, -MAX_REL, MAX_REL) + MAX_REL
    diag = table[:, rel_idx].astype(F32).reshape(heads, 1, BIAS_PERIOD)
    return pl.pallas_call(
        functools.partial(_band_bias_kernel, qb=qb),
        grid=(heads,),
        in_specs=[pl.BlockSpec((1, 1, BIAS_PERIOD), lambda h: (h, 0, 0))],
        out_specs=pl.BlockSpec((1, qb, nkeys), lambda h: (h, 0, 0)),
        out_shape=jax.ShapeDtypeStruct((heads, qb, nkeys), F32),
        compiler_params=_params("parallel"),
        name="band_bias",
    )(diag)


def _cached_attn_kernel(q_ref, ktn_ref, vn_ref, kp_ref, vp_ref, bias_ref, o_ref):
    t, n_past = q_ref.shape[1], kp_ref.shape[1]
    low = lax.broadcasted_iota(jnp.int32, (t, LANES), 1) < ATT_DHEAD
    for g in range(ATT_HEADS // 2):
        sl = slice(g * LANES, (g + 1) * LANES)
        q2 = q_ref[0, :, sl].astype(F32)
        kp2, vp2 = kp_ref[0, :, sl].astype(BF16), vp_ref[0, :, sl].astype(BF16)
        qs = jnp.concatenate([jnp.where(low, q2, 0.0), jnp.where(low, 0.0, q2)], axis=0).astype(BF16)
        bias = jnp.concatenate([bias_ref[2 * g], bias_ref[2 * g + 1]], axis=0)
        s_past = (lax.dot_general(qs, kp2, (((1,), (1,)), ((), ())), preferred_element_type=F32)
                  + bias[:, :n_past])
        s_new = jnp.dot(qs, ktn_ref[0, sl, :], preferred_element_type=F32) + bias[:, n_past:]
        m = jnp.maximum(jnp.max(s_past, axis=-1, keepdims=True), jnp.max(s_new, axis=-1, keepdims=True))
        p_past, p_new = jnp.exp2(s_past - m), jnp.exp2(s_new - m)
        l = jnp.sum(p_past, axis=-1, keepdims=True) + jnp.sum(p_new, axis=-1, keepdims=True)
        pv = (jnp.dot(p_past.astype(BF16), vp2, preferred_element_type=F32)
              + jnp.dot(p_new.astype(BF16), vn_ref[0, :, sl], preferred_element_type=F32)) / l
        o_ref[0, :, sl] = jnp.where(low, pv[:t, :], pv[t:, :]).astype(BF16)


def _cached_attn(q, kt_new, v_new, k_past, v_past, bias):
    b, t, _ = q.shape
    n_past = k_past.shape[1]
    tok = pl.BlockSpec((1, t, ATT_WIDTH), lambda i: (i, 0, 0))
    past = pl.BlockSpec((1, n_past, ATT_WIDTH), lambda i: (i, 0, 0))
    return pl.pallas_call(
        _cached_attn_kernel,
        grid=(b,),
        in_specs=[tok, pl.BlockSpec((1, ATT_WIDTH, t), lambda i: (i, 0, 0)), tok, past, past,
                  _const_spec(bias.shape)],
        out_specs=tok,
        out_shape=jax.ShapeDtypeStruct((b, t, ATT_WIDTH), BF16),
        compiler_params=_params("parallel"),
        name="cached_attn",
    )(q, kt_new, v_new, k_past, v_past, bias)


def _attention(q, kt, v, att_past, bias_tile):
    if att_past is None:
        return _band_attn(q, kt, v, bias_tile)
    k_past, v_past = att_past
    b, t, _ = q.shape
    n_past = k_past.shape[1]
    assert n_past == BAND and t <= CHUNK
    bias = bias_tile[:, :t, :n_past + t]
    return _cached_attn(q, kt, v, k_past.reshape(b, n_past, ATT_WIDTH), v_past.reshape(b, n_past, ATT_WIDTH), bias)


def _band_attn(q, kt, v, bias):
    b, tq, _ = q.shape
    qb = bias.shape[1]
    lead = N_KEY_BLOCKS - qb // KEY_BLOCK

    def kidx(s):
        return lambda i, j: jnp.maximum(j - lead + s, 0)

    kt_specs = [pl.BlockSpec((1, ATT_WIDTH, KEY_BLOCK), (lambda f: lambda i, j: (i, 0, f(i, j)))(kidx(s)))
                for s in range(N_KEY_BLOCKS)]
    v_specs = [pl.BlockSpec((1, KEY_BLOCK, ATT_WIDTH), (lambda f: lambda i, j: (i, f(i, j), 0))(kidx(s)))
               for s in range(N_KEY_BLOCKS)]
    tok = pl.BlockSpec((1, qb, ATT_WIDTH), lambda i, j: (i, j, 0))
    return pl.pallas_call(
        functools.partial(_band_attn_kernel, lead_blocks=lead),
        grid=(b, tq // qb),
        in_specs=[tok] + kt_specs + v_specs + [_const_spec(bias.shape)],
        out_specs=tok,
        out_shape=jax.ShapeDtypeStruct((b, tq, ATT_WIDTH), BF16),
        scratch_shapes=[pltpu.VMEM((N_STAGE, qb, N_KEY_BLOCKS * KEY_BLOCK), F32),
                        pltpu.VMEM((N_STAGE, qb, N_KEY_BLOCKS * KEY_BLOCK), BF16),
                        pltpu.VMEM((N_STAGE, qb, LANES), F32),
                        pltpu.VMEM((N_STAGE, qb, LANES), F32),
                        pltpu.VMEM((qb, LANES), F32)],
        compiler_params=_params("parallel", "arbitrary"),
        name="band_attn",
    )(q, kt, kt, kt, v, v, v, bias)


def _mix_ln_kernel(h_ref, hm_ref, oa_ref, sgm_ref, sga_ref, wm_ref, wa_ref, wout_ref, g_ref, b_ref, o_ref):
    pending = None
    for rows in (slice(r, r + FFN_SUB_ROWS) for r in range(0, h_ref.shape[0], FFN_SUB_ROWS)):
        y_ml = jnp.dot(hm_ref[rows, :], wm_ref[...], preferred_element_type=F32)
        y_att = jnp.dot(oa_ref[rows, :], wa_ref[...], preferred_element_type=F32)
        merged = sgm_ref[rows, :].astype(F32) * y_ml + sga_ref[rows, :].astype(F32) * y_att
        mix = jnp.dot(merged.astype(BF16), wout_ref[...], preferred_element_type=F32)
        if pending is not None:
            p_rows, p_y = pending
            o_ref[p_rows, :] = _layer_norm(p_y, g_ref[...], b_ref[...])
        pending = (rows, ALPHA * h_ref[rows, :] + mix)
    p_rows, p_y = pending
    o_ref[p_rows, :] = _layer_norm(p_y, g_ref[...], b_ref[...])


def _mix_ln(h, hm, oa, sgm, sga, wm, wa, wout, g, b):
    n = h.shape[0]
    tm = min(FFN_TILE_ROWS, n)
    row = pl.BlockSpec((tm, D_MODEL), lambda i: (i, 0))
    consts = [wm, wa, wout, g, b]
    return pl.pallas_call(
        _mix_ln_kernel,
        grid=(n // tm,),
        in_specs=[row] * 5 + [_const_spec(a.shape) for a in consts],
        out_specs=row,
        out_shape=jax.ShapeDtypeStruct((n, D_MODEL), F32),
        compiler_params=_params("parallel"),
        name="mix_ln",
    )(h, hm, oa, sgm, sga, *consts)


W_PREP_ROWS = 256


def _w_prep_kernel(a_ref, b_ref, main_ref, gate_ref):
    c = pl.program_id(1)
    rows, width = a_ref.shape
    gate_cols = _O_AQ - _O_MLI

    @pl.when(c < _W_AQ)
    def _():
        main_ref[...] = a_ref[...].astype(BF16)

    @pl.when(c >= _W_AQ)
    def _():
        lane = lax.broadcasted_iota(jnp.int32, (rows, width), 1)
        moved = jnp.where(lane < width - gate_cols,
                          pltpu.roll(a_ref[...], width - gate_cols, 1), pltpu.roll(b_ref[...], width - gate_cols, 1))
        scale = jnp.where(c == _W_AQ, ATT_DHEAD ** -0.5 * LOG2_E, 1.0)
        main_ref[...] = (moved * scale).astype(BF16)

    @pl.when(c == _W_AQ)
    def _():
        lane = lax.broadcasted_iota(jnp.int32, (rows, GATE_LANES), 1)
        gate_ref[...] = jnp.where(lane < gate_cols, a_ref[:, :GATE_LANES], 0.0).astype(BF16)


def _w_prep(w_in):
    n_blocks = _W_GA + 1
    last = pl.cdiv(w_in.shape[1], D_MODEL) - 1
    return pl.pallas_call(
        _w_prep_kernel,
        grid=(D_MODEL // W_PREP_ROWS, n_blocks),
        in_specs=[pl.BlockSpec((W_PREP_ROWS, D_MODEL), lambda r, c: (r, c)),
                  pl.BlockSpec((W_PREP_ROWS, D_MODEL), lambda r, c: (r, jnp.clip(c + 1, _W_AQ + 1, last)))],
        out_specs=[pl.BlockSpec((W_PREP_ROWS, D_MODEL), lambda r, c: (r, c)),
                   pl.BlockSpec((W_PREP_ROWS, GATE_LANES), lambda r, c: (r, 0))],
        out_shape=[jax.ShapeDtypeStruct((D_MODEL, n_blocks * D_MODEL), BF16),
                   jax.ShapeDtypeStruct((D_MODEL, GATE_LANES), BF16)],
        compiler_params=_params("parallel", "arbitrary"),
        name="w_prep",
    )(w_in, w_in)


def _prep_weights(p):
    w_main, w_gate = _w_prep(p["w_in"])
    bif = jnp.concatenate([p["b_ml_i"], p["b_ml_f"], jnp.zeros((GATE_LANES - 2 * ML_HEADS,), F32)])
    row = lambda a: a.reshape(1, -1).astype(F32)
    return {
        "w_in": w_main,
        "w_gate": w_gate,
        "bif": row(bif),
        "conv_w": jnp.broadcast_to(p["ml_conv_w"].astype(F32)[:, None, :], (CONV_W, CONV_ROWS, 2 * ML_WIDTH)),
        "conv_b": jnp.broadcast_to(row(p["ml_conv_b"]), (CONV_ROWS, 2 * ML_WIDTH)),
        "norm_g": row(p["ml_norm_g"]),
        "wm": p["w_ml_proj"].astype(BF16),
        "wa": p["w_att_proj"].astype(BF16),
        "wout": p["w_out"].astype(BF16),
        "ffn1": (p["ffn1_w_gu"].astype(BF16), p["ffn1_w_down"].astype(BF16)),
        "ffn2": (p["ffn2_w_gu"].astype(BF16), p["ffn2_w_down"].astype(BF16)),
        "ln1": (row(p["ln1_g"]), row(p["ln1_b"])),
        "ln2": (row(p["ln2_g"]), row(p["ln2_b"])),
        "ln3": (row(p["ln3_g"]), row(p["ln3_b"])),
        "bias_tile": _band_bias(p["att_rel_bias"], KEY_BLOCK),
    }


def _encoder_layer(x, w, conv_prev, ml_state, att_past):
    b, t, _ = x.shape
    n = b * t
    flat = lambda a: a.reshape(n, a.shape[-1])
    h1 = _ffn_ln(flat(x), *w["ffn1"], *w["ln1"])
    (q, k, v, so, gates, new_conv, aq, akt, av, k_rows, v_rows, sgm, sga) = _in_proj(
        h1.reshape(b, t, D_MODEL), conv_prev, w)

    if ml_state is not None:
        c0, n0, m0 = ml_state
        m0p = jnp.pad(m0.astype(F32), ((0, 0), (0, LANES - ML_HEADS))).reshape(b, 1, LANES)
        n0_rep = jnp.broadcast_to(n0.astype(F32)[..., None], n0.shape + (LANES,))
        ml_state = (c0.astype(F32), n0_rep, m0p)
    hm, c1, n1, m1p = _mlstm(q, k, v, so, gates, ml_state, w["norm_g"])
    m1 = m1p[:, 0, :ML_HEADS]

    oa = _attention(aq, akt, av, att_past, w["bias_tile"])
    h2 = _mix_ln(h1, flat(hm), flat(oa), flat(sgm), flat(sga), w["wm"], w["wa"], w["wout"], *w["ln2"])
    y = _ffn_ln(h2, *w["ffn2"], *w["ln3"]).reshape(b, t, D_MODEL)
    keep = k_rows.shape[1]
    state = (new_conv, c1, n1, m1,
             k_rows.reshape(b, keep, ATT_HEADS, ATT_DHEAD), v_rows.reshape(b, keep, ATT_HEADS, ATT_DHEAD))
    return y, state


def kernel(x_prompt, x_sample, state_ml_conv, state_ml_C, state_ml_n, state_ml_m, cache_att_k, cache_att_v,
           w_in, b_ml_i, b_ml_f, ml_conv_w, ml_conv_b, ml_norm_g, att_rel_bias, w_ml_proj, w_att_proj, w_out,
           ffn1_w_gu, ffn1_w_down, ffn2_w_gu, ffn2_w_down, ln1_g, ln1_b, ln2_g, ln2_b, ln3_g, ln3_b):
    depth = w_in.shape[0]
    y_p, y_s = x_prompt, x_sample
    p_states, s_states = [], []
    for l in range(depth):
        w = _prep_weights({
            "w_in": w_in[l], "b_ml_i": b_ml_i[l], "b_ml_f": b_ml_f[l], "ml_conv_w": ml_conv_w[l],
            "ml_conv_b": ml_conv_b[l], "ml_norm_g": ml_norm_g[l], "att_rel_bias": att_rel_bias[l],
            "w_ml_proj": w_ml_proj[l], "w_att_proj": w_att_proj[l], "w_out": w_out[l],
            "ffn1_w_gu": ffn1_w_gu[l], "ffn1_w_down": ffn1_w_down[l], "ffn2_w_gu": ffn2_w_gu[l],
            "ffn2_w_down": ffn2_w_down[l], "ln1_g": ln1_g[l], "ln1_b": ln1_b[l], "ln2_g": ln2_g[l],
            "ln2_b": ln2_b[l], "ln3_g": ln3_g[l], "ln3_b": ln3_b[l]})
        y_p, st_p = _encoder_layer(y_p, w, None, None, None)
        y_s, st_s = _encoder_layer(y_s, w, state_ml_conv[l],
                                   (state_ml_C[l], state_ml_n[l], state_ml_m[l]),
                                   (cache_att_k[l], cache_att_v[l]))
        p_states.append(st_p)
        s_states.append(st_s)
    p_conv, p_c, p_n, p_m, p_k, p_v = [jnp.stack(s) for s in zip(*p_states)]
    s_conv, s_c, s_n, s_m, s_k, s_v = [jnp.stack(s) for s in zip(*s_states)]
    return (y_p, y_s, p_conv, s_conv, p_c, s_c, p_n, s_n, p_m, s_m, p_k, s_k, p_v, s_v)
```

```python
import functools

import jax
import jax.numpy as jnp
import numpy as np
from jax import lax
from jax.experimental import pallas as pl
from jax.experimental.pallas import tpu as pltpu

F32 = jnp.float32
BF16 = jnp.bfloat16

D_MODEL = 1024
CHUNK = 64
ML_HEADS = 4
ML_DHEAD = D_MODEL // ML_HEADS
ML_WIDTH = ML_HEADS * ML_DHEAD
CONV_W = 4
ATT_HEADS = 16
ATT_DHEAD = D_MODEL // ATT_HEADS
ATT_WIDTH = ATT_HEADS * ATT_DHEAD
BAND_CHUNKS = 8
BAND = BAND_CHUNKS * CHUNK
MAX_REL = 128
D_FF = ((8 * D_MODEL) // 3 + 127) // 128 * 128
ALPHA = 2.0 ** 0.25
LN_EPS = 1e-5
NEG_INF = -1e30
LOG2_E = 1.4426950408889634

LANES = 128
GATE_LANES = LANES
VMEM_LIMIT = 56 * 1024 * 1024

_O_MLI = 4 * ML_WIDTH
_O_AQ = _O_MLI + 2 * ML_HEADS
_W_V, _W_O, _W_AQ, _W_AK, _W_AV, _W_GM, _W_GA = 2, 3, 4, 5, 6, 7, 8


def _const_spec(shape, col_block=0):
    index = (0,) * (len(shape) - 1) + (col_block,)
    return pl.BlockSpec(shape, lambda *_: index, pipeline_mode=pl.Buffered(1))


def _params(*sem):
    return pltpu.CompilerParams(dimension_semantics=sem, vmem_limit_bytes=VMEM_LIMIT)


def _layer_norm(y, g, b):
    mu = jnp.mean(y, axis=-1, keepdims=True)
    yc = y - mu
    var = jnp.mean(yc * yc, axis=-1, keepdims=True)
    return yc * lax.rsqrt(var + LN_EPS) * g + b


def _sigmoid(x):
    return 1.0 / (1.0 + jnp.exp(-x))


_FF_CHUNKS = ((0, 768), (768, 1792), (1792, D_FF))
FFN_SUB_ROWS = 256
FFN_TILE_ROWS = 1024


def _ffn_ln_kernel(x_ref, wg_ref, wu_ref, wd_ref, g_ref, b_ref, o_ref):
    pending = None
    for rows in (slice(r, r + FFN_SUB_ROWS) for r in range(0, x_ref.shape[0], FFN_SUB_ROWS)):
        x = x_ref[rows, :]
        xb = x.astype(BF16)
        acc = None
        for s, e in _FF_CHUNKS:
            gate = jnp.dot(xb, wg_ref[:, s:e], preferred_element_type=F32)
            up = jnp.dot(xb, wu_ref[:, s:e], preferred_element_type=F32)
            hid = (gate * _sigmoid(gate) * up).astype(BF16)
            part = jnp.dot(hid, wd_ref[s:e, :], preferred_element_type=F32)
            acc = part if acc is None else acc + part
        if pending is not None:
            p_rows, p_y = pending
            o_ref[p_rows, :] = _layer_norm(p_y, g_ref[...], b_ref[...])
        pending = (rows, ALPHA * x + 0.5 * acc)
    p_rows, p_y = pending
    o_ref[p_rows, :] = _layer_norm(p_y, g_ref[...], b_ref[...])


def _ffn_ln(x, wgu, wd, g, b):
    n = x.shape[0]
    tm = min(FFN_TILE_ROWS, n)
    row = pl.BlockSpec((tm, D_MODEL), lambda i: (i, 0))
    return pl.pallas_call(
        _ffn_ln_kernel,
        grid=(n // tm,),
        in_specs=[row, _const_spec((D_MODEL, D_FF), 0), _const_spec((D_MODEL, D_FF), 1), _const_spec(wd.shape),
                  _const_spec(g.shape), _const_spec(b.shape)],
        out_specs=row,
        out_shape=jax.ShapeDtypeStruct((n, D_MODEL), F32),
        compiler_params=_params("parallel"),
        name="ffn_ln",
    )(x, wgu, wgu, wd, g, b)


PROJ_TILE_ROWS = 512
COL_CHUNK = 256
CONV_ROWS = 16


def _chunk_jobs(h_ref, hb_ref):
    seqs, frames, _ = h_ref.shape
    for s in range(seqs):
        hb_ref[s * frames:(s + 1) * frames, :] = h_ref[s].astype(BF16)

    def proj(w_ref, c):
        cols = slice(c * COL_CHUNK, (c + 1) * COL_CHUNK)
        return jnp.dot(hb_ref[...], w_ref[:, cols], preferred_element_type=F32), cols

    def scatter(o_ref, cols, y):
        for s in range(seqs):
            o_ref[s, :, cols] = y[s * frames:(s + 1) * frames, :]

    def plain(w_ref, o_ref):
        def job(c):
            y, cols = proj(w_ref, c)
            scatter(o_ref, cols, y.astype(BF16))
        return job

    def gated(w_ref, o_ref):
        def job(c):
            y, cols = proj(w_ref, c)
            scatter(o_ref, cols, _sigmoid(y).astype(BF16))
        return job

    return proj, scatter, plain, gated


def _interleave(*queues):
    order = []
    for i in range(max(len(q) for q in queues)):
        for q in queues:
            order += q[i:i + 1]
    return order


def _in_proj_ml_kernel(h_ref, *refs, fresh):
    cprev_ref = None if fresh else refs[0]
    (wqk_ref, wv_ref, wo_ref, wif_ref, bif_ref, cw_ref, cb_ref,
     q_ref, kt_ref, v_ref, so_ref, gt_ref, conv_ref, hb_ref, raw_ref, kc_ref) = refs[0 if fresh else 1:]
    seqs, frames, _ = h_ref.shape
    proj, scatter, plain, gated = _chunk_jobs(h_ref, hb_ref)

    @pl.when(pl.program_id(1) == 0)
    def _():
        for s in range(seqs):
            raw_ref[s, 0:8, :] = jnp.zeros((8, 2 * ML_WIDTH), F32)
            if not fresh:
                raw_ref[s, 8 - (CONV_W - 1):8, :] = cprev_ref[s]

    def qk_raw(c):
        y, cols = proj(wqk_ref, c)
        for s in range(seqs):
            raw_ref[s, 8:8 + frames, cols] = y[s * frames:(s + 1) * frames, :]

    def qk_conv(c):
        cols = slice(c * COL_CHUNK, (c + 1) * COL_CHUNK)
        is_k = c * COL_CHUNK >= ML_WIDTH
        sub = lax.broadcasted_iota(jnp.int32, (8, COL_CHUNK), 0)
        for s in range(seqs):
            for r in range(frames // CONV_ROWS):
                base = 8 + r * CONV_ROWS
                blocks = [raw_ref[s, base + 8 * i:base + 8 * i + 8, cols] for i in range(-1, CONV_ROWS // 8)]
                taps = [jnp.concatenate(blocks[1:], axis=0)]
                for j in range(1, CONV_W):
                    taps.append(jnp.concatenate(
                        [pltpu.roll(jnp.where(sub < 8 - j, cur, prev), j, axis=0)
                         for prev, cur in zip(blocks[:-1], blocks[1:])], axis=0))
                acc = cb_ref[:, cols]
                for j in range(CONV_W):
                    acc = acc + taps[j] * cw_ref[CONV_W - 1 - j, :, cols]
                qk = acc * _sigmoid(acc)
                rows = slice(r * CONV_ROWS, (r + 1) * CONV_ROWS)
                if is_k:
                    kc_ref[rows, :] = qk * (ML_DHEAD ** -0.5)
                else:
                    q_ref[s, rows, cols] = qk.astype(BF16)
            if is_k:
                kt_ref[s, cols.start - ML_WIDTH:cols.stop - ML_WIDTH, :] = kc_ref[...].astype(BF16).T

    n_chunks = D_MODEL // COL_CHUNK
    light = [functools.partial(job, c) for c in range(n_chunks)
             for job in (plain(wv_ref, v_ref), gated(wo_ref, so_ref))]
    qk_raw(0)
    for c in range(2 * n_chunks):
        if c + 1 < 2 * n_chunks:
            qk_raw(c + 1)
        light[c]()
        qk_conv(c)
    for s in range(seqs):
        conv_ref[s] = raw_ref[s, 8 + frames - (CONV_W - 1):8 + frames, :]
        raw_ref[s, 0:8, :] = raw_ref[s, frames:frames + 8, :]

    zg = jnp.dot(hb_ref[...], wif_ref[...], preferred_element_type=F32) + bif_ref[...]
    lane = lax.broadcasted_iota(jnp.int32, zg.shape, 1)
    log_sig = jnp.minimum(zg, 0.0) - jnp.log(1.0 + jnp.exp(-jnp.abs(zg)))
    scatter(gt_ref, slice(None), jnp.where(lane < ML_HEADS, zg, log_sig))


def _in_proj_att_kernel(h_ref, waq_ref, wak_ref, wav_ref, wgm_ref, wga_ref,
                        aq_ref, akt_ref, av_ref, pk_ref, pv_ref, sgm_ref, sga_ref, hb_ref):
    seqs, frames, _ = h_ref.shape
    proj, scatter, plain, gated = _chunk_jobs(h_ref, hb_ref)

    def att_k(c):
        y, cols = proj(wak_ref, c)
        scatter(pk_ref, cols, y)
        yb = y.astype(BF16)
        for s in range(seqs):
            akt_ref[s, cols, :] = yb[s * frames:(s + 1) * frames, :].T

    def att_v(c):
        y, cols = proj(wav_ref, c)
        scatter(pv_ref, cols, y)
        scatter(av_ref, cols, y.astype(BF16))

    n_chunks = D_MODEL // COL_CHUNK
    heavy = [functools.partial(job, c) for c in range(n_chunks)
             for job in (gated(wgm_ref, sgm_ref), att_k, gated(wga_ref, sga_ref))]
    light = [functools.partial(job, c) for c in range(n_chunks) for job in (plain(waq_ref, aq_ref), att_v)]
    for job in _interleave(heavy, light):
        job()


def _in_proj(h, conv_prev, w):
    b, t, _ = h.shape
    tm = min(PROJ_TILE_ROWS, t)
    sb = max(1, min(b, PROJ_TILE_ROWS // t))
    nt = t // tm
    keep = min(BAND, t)
    first_kept = (t - keep) // tm
    h_spec = pl.BlockSpec((sb, tm, D_MODEL), lambda i, j: (i, j, 0))

    def tok(width, dtype):
        return (pl.BlockSpec((sb, tm, width), lambda i, j: (i, j, 0)),
                jax.ShapeDtypeStruct((b, t, width), dtype))

    def kept(width):
        return (pl.BlockSpec((sb, tm, width), lambda i, j: (i, jnp.maximum(j - first_kept, 0), 0)),
                jax.ShapeDtypeStruct((b, keep, width), F32))

    conv_spec = pl.BlockSpec((sb, CONV_W - 1, 2 * ML_WIDTH), lambda i, j: (i, 0, 0))
    feature_major = (pl.BlockSpec((sb, ATT_WIDTH, tm), lambda i, j: (i, 0, j)),
                     jax.ShapeDtypeStruct((b, ATT_WIDTH, t), BF16))
    ml_outs = [
        tok(ML_WIDTH, BF16), feature_major, tok(ML_WIDTH, BF16), tok(ML_WIDTH, BF16),
        tok(GATE_LANES, F32),
        (conv_spec, jax.ShapeDtypeStruct((b, CONV_W - 1, 2 * ML_WIDTH), F32)),
    ]
    w_in = w["w_in"]
    square = (D_MODEL, D_MODEL)
    ml_weights = [w_in, w_in, w_in, w["w_gate"], w["bif"], w["conv_w"], w["conv_b"]]
    fresh = conv_prev is None
    carried = [] if fresh else [conv_prev]
    ml = pl.pallas_call(
        functools.partial(_in_proj_ml_kernel, fresh=fresh),
        grid=(b // sb, nt),
        in_specs=[h_spec] + [conv_spec] * len(carried)
                 + [_const_spec((D_MODEL, 2 * ML_WIDTH), 0), _const_spec(square, _W_V), _const_spec(square, _W_O)]
                 + [_const_spec(a.shape) for a in ml_weights[3:]],
        out_specs=[o[0] for o in ml_outs],
        out_shape=[o[1] for o in ml_outs],
        scratch_shapes=[pltpu.VMEM((sb * tm, D_MODEL), BF16), pltpu.VMEM((sb, tm + 8, 2 * ML_WIDTH), F32),
                        pltpu.VMEM((tm, COL_CHUNK), F32)],
        compiler_params=_params("parallel", "arbitrary"),
        name="in_proj_ml",
    )(h, *carried, *ml_weights)

    att_outs = [
        tok(ATT_WIDTH, BF16),
        feature_major,
        tok(ATT_WIDTH, BF16),
        kept(ATT_WIDTH), kept(ATT_WIDTH),
        tok(D_MODEL, BF16), tok(D_MODEL, BF16),
    ]
    att_weights = [w_in] * 5
    att = pl.pallas_call(
        _in_proj_att_kernel,
        grid=(b // sb, nt),
        in_specs=[h_spec] + [_const_spec(square, c) for c in (_W_AQ, _W_AK, _W_AV, _W_GM, _W_GA)],
        out_specs=[o[0] for o in att_outs],
        out_shape=[o[1] for o in att_outs],
        scratch_shapes=[pltpu.VMEM((sb * tm, D_MODEL), BF16)],
        compiler_params=_params("parallel", "arbitrary"),
        name="in_proj_att",
    )(h, *att_weights)
    return tuple(ml) + tuple(att)


def _split3(x):
    hi = x.astype(BF16)
    rest = x - hi.astype(F32)
    mid = rest.astype(BF16)
    return hi, mid, (rest - mid.astype(F32)).astype(BF16)


def _mlstm_kernel(q_ref, kt_ref, v_ref, so_ref, gt_ref, *refs, blk, fresh):
    c0_ref, n0_ref, m0_ref = (None, None, None) if fresh else refs[:3]
    (ng_ref, hm_ref, c_ref, n_ref, m_ref,
     ncol_ref, d_ref, s_ref, hh_ref, wi_ref, emt_ref, rs_ref) = refs[0 if fresh else 3:]
    heads = range(ML_HEADS)
    rb = min(ROW_BLOCK, blk)
    row_blocks = [slice(r * rb, (r + 1) * rb) for r in range(blk // rb)]
    head_cols = [slice(j * ML_DHEAD, (j + 1) * ML_DHEAD) for j in heads]

    @pl.when(pl.program_id(1) == 0)
    def _():
        if fresh:
            c_ref[...] = jnp.zeros_like(c_ref)
            ncol_ref[...] = jnp.zeros_like(ncol_ref)
            m_ref[...] = jnp.zeros_like(m_ref)
        else:
            c_ref[...] = c0_ref[...]
            ncol_ref[...] = n0_ref[0]
            m_ref[...] = m0_ref[...]

    gates = gt_ref[0]
    tril = (lax.broadcasted_iota(jnp.int32, (blk, blk), 1)
            <= lax.broadcasted_iota(jnp.int32, (blk, blk), 0)).astype(BF16)
    csum = sum(jnp.dot(tril, part, preferred_element_type=F32) for part in _split3(gates))
    bcum = pltpu.roll(csum, LANES - ML_HEADS, axis=1)
    li_rel = gates - bcum
    li_rel_t = li_rel.T
    m_prev = m_ref[0]
    b_last = bcum[blk - 1:blk, :]
    g_s = b_last + li_rel
    m_new = jnp.maximum(b_last + m_prev, jnp.max(g_s, axis=0, keepdims=True))
    decay = jnp.exp(b_last + m_prev - m_new)
    w_s_t = jnp.exp(g_s - m_new).T
    inter = bcum + m_prev

    def lane_of(x, j):
        lane = lax.broadcasted_iota(jnp.int32, x.shape, 1)
        return jnp.sum(jnp.where(lane == j, x, 0.0), axis=-1, keepdims=True)

    def spread(x, width):
        return x[:, :width] if width <= LANES else jnp.concatenate([x] * (width // LANES), axis=1)

    units = [(j, rows) for j in heads for rows in row_blocks]

    for j, rows in units:
        wi_ref[j, rows, :] = jnp.broadcast_to(lane_of(bcum[rows, :], j), (rb, LANES))
    m_prev_j = [lane_of(m_prev, j) for j in heads]
    for j, rows in units:
        causal = (lax.broadcasted_iota(jnp.int32, (rb, blk), 1)
                  <= lax.broadcasted_iota(jnp.int32, (rb, blk), 0) + rows.start)
        bcol = wi_ref[j, rows, :]
        dmat = jnp.where(causal, spread(bcol, blk) + li_rel_t[j:j + 1, :], -jnp.inf)
        icol = bcol + m_prev_j[j]
        m_t = jnp.maximum(icol, jnp.max(dmat, axis=-1, keepdims=True))
        d_ref[j, rows, :] = jnp.exp(dmat - spread(m_t, blk))
        wi_ref[j, rows, :] = jnp.exp(icol - m_t)
        emt_ref[j, rows, :] = jnp.exp(-m_t)

    for j in heads:
        qk = jnp.dot(q_ref[0, :, head_cols[j]], kt_ref[0, head_cols[j], :], preferred_element_type=F32)
        for rows in row_blocks:
            s = qk[rows, :] * d_ref[j, rows, :]
            rs_ref[j, rows, :] = jnp.broadcast_to(jnp.sum(s, axis=-1, keepdims=True), (rb, LANES))
            s_ref[j, rows, :] = s.astype(BF16)

    for j in heads:
        qj, vj = q_ref[0, :, head_cols[j]], v_ref[0, :, head_cols[j]]
        q_c = jnp.dot(qj, c_ref[0, j].astype(BF16), preferred_element_type=F32)
        s_v = jnp.dot(s_ref[j], vj, preferred_element_type=F32)
        q_n = jnp.dot(qj, ncol_ref[j].astype(BF16), preferred_element_type=F32)
        for rows in row_blocks:
            w_inter = wi_ref[j, rows, :]
            den = w_inter * q_n[rows, :] + rs_ref[j, rows, :]
            scale = 1.0 / jnp.maximum(jnp.abs(den), emt_ref[j, rows, :])
            hh = (spread(w_inter, ML_DHEAD) * q_c[rows, :] + s_v[rows, :]) * spread(scale, ML_DHEAD)
            hh_ref[j, rows, :] = hh
            wi_ref[j, rows, :] = jnp.broadcast_to(jnp.mean(hh, axis=-1, keepdims=True), (rb, LANES))
    for j, rows in units:
        hc = hh_ref[j, rows, :] - spread(wi_ref[j, rows, :], ML_DHEAD)
        var = jnp.mean(hc * hc, axis=-1, keepdims=True)
        hn = hc * lax.rsqrt(var + LN_EPS) * ng_ref[:, head_cols[j]] * so_ref[0, rows, head_cols[j]].astype(F32)
        hm_ref[0, rows, head_cols[j]] = hn.astype(BF16)

    ones = jnp.ones((blk, LANES), BF16)
    for j in heads:
        kw = (kt_ref[0, head_cols[j], :].astype(F32) * w_s_t[j:j + 1, :]).astype(BF16)
        d_j = lane_of(decay, j)
        c_ref[0, j] = d_j * c_ref[0, j] + jnp.dot(kw, v_ref[0, :, head_cols[j]], preferred_element_type=F32)
        ncol_ref[j] = d_j * ncol_ref[j] + jnp.dot(kw, ones, preferred_element_type=F32)
    m_ref[0] = m_new

    @pl.when(pl.program_id(1) == pl.num_programs(1) - 1)
    def _():
        for j in heads:
            n_ref[0, j:j + 1, :] = ncol_ref[j].T[0:1, :]


def _mlstm(q, kt, v, so, gates, state, norm_g):
    b, t, _ = q.shape
    blk = min(256, t)
    carried = [] if state is None else list(state)
    tok = pl.BlockSpec((1, blk, ML_WIDTH), lambda i, j: (i, j, 0))
    c_spec = pl.BlockSpec((1, ML_HEADS, ML_DHEAD, ML_DHEAD), lambda i, j: (i, 0, 0, 0))
    n_spec = pl.BlockSpec((1, ML_HEADS, ML_DHEAD), lambda i, j: (i, 0, 0))
    m_spec = pl.BlockSpec((1, 1, LANES), lambda i, j: (i, 0, 0))
    per_head = lambda width, dtype: pltpu.VMEM((ML_HEADS, blk, width), dtype)
    return pl.pallas_call(
        functools.partial(_mlstm_kernel, blk=blk, fresh=state is None),
        grid=(b, t // blk),
        in_specs=[tok, pl.BlockSpec((1, ML_WIDTH, blk), lambda i, j: (i, 0, j)), tok, tok,
                  pl.BlockSpec((1, blk, GATE_LANES), lambda i, j: (i, j, 0))]
                 + [c_spec, pl.BlockSpec((1, ML_HEADS, ML_DHEAD, LANES), lambda i, j: (i, 0, 0, 0)),
                    m_spec][:len(carried)]
                 + [_const_spec(norm_g.shape)],
        out_specs=[tok, c_spec, n_spec, m_spec],
        out_shape=[jax.ShapeDtypeStruct((b, t, ML_WIDTH), BF16),
                   jax.ShapeDtypeStruct((b, ML_HEADS, ML_DHEAD, ML_DHEAD), F32),
                   jax.ShapeDtypeStruct((b, ML_HEADS, ML_DHEAD), F32),
                   jax.ShapeDtypeStruct((b, 1, LANES), F32)],
        scratch_shapes=[pltpu.VMEM((ML_HEADS, ML_DHEAD, LANES), F32),
                        per_head(blk, F32), per_head(blk, BF16), per_head(ML_DHEAD, F32),
                        per_head(LANES, F32), per_head(LANES, F32), per_head(LANES, F32)],
        compiler_params=_params("parallel", "arbitrary"),
        name="mlstm",
    )(q, kt, v, so, gates, *carried, norm_g)


KEY_BLOCK = 256
N_KEY_BLOCKS = 3
N_STAGE = 4
ROW_BLOCK = 64


def _band_attn_kernel(q_ref, kt0_ref, kt1_ref, kt2_ref, v0_ref, v1_ref, v2_ref, bias_ref, o_ref,
                      s_ref, p_ref, mx_ref, linv_ref, even_ref, *, lead_blocks):
    kt_refs = (kt0_ref, kt1_ref, kt2_ref)
    v_refs = (v0_ref, v1_ref, v2_ref)
    qb = q_ref.shape[1]
    lane = lax.broadcasted_iota(jnp.int32, (qb, LANES), 1)
    low = lane < ATT_DHEAD

    def attend(first_slot):
        slots = range(first_slot, N_KEY_BLOCKS)
        k_lo = first_slot * KEY_BLOCK

        def scores(head):
            sl = slice(head // 2 * LANES, (head // 2 + 1) * LANES)
            q2 = q_ref[0, :, sl].astype(F32)
            qh = jnp.where(low if head % 2 == 0 else jnp.logical_not(low), q2, 0.0).astype(BF16)
            row_max = None
            for c in slots:
                cols = slice(c * KEY_BLOCK, (c + 1) * KEY_BLOCK)
                s = jnp.dot(qh, kt_refs[c][0, sl, :], preferred_element_type=F32) + bias_ref[head, :, cols]
                s_ref[head % N_STAGE, :, cols] = s
                part = jnp.maximum(s[:, :LANES], s[:, LANES:])
                row_max = part if row_max is None else jnp.maximum(row_max, part)
            mx_ref[head % N_STAGE] = row_max

        def weighted_values(head):
            sl = slice(head // 2 * LANES, (head // 2 + 1) * LANES)
            buf = head % N_STAGE
            rb = min(ROW_BLOCK, qb)
            for r in range(qb // rb):
                rows = slice(r * rb, (r + 1) * rb)
                m = jnp.max(mx_ref[buf, rows, :], axis=-1, keepdims=True)
                row_sum = None
                for c in slots:
                    cols = slice(c * KEY_BLOCK, (c + 1) * KEY_BLOCK)
                    p = jnp.exp2(s_ref[buf, rows, cols] - m)
                    p_ref[buf, rows, cols] = p.astype(BF16)
                    part = p[:, :LANES] + p[:, LANES:]
                    row_sum = part if row_sum is None else row_sum + part
                l = jnp.sum(row_sum, axis=-1, keepdims=True)
                linv_ref[buf, rows, :] = jnp.broadcast_to(1.0 / l, (rb, LANES))
            v2 = jnp.concatenate([v_refs[c][0, :, sl] for c in slots], axis=0)
            return jnp.dot(p_ref[buf, :, k_lo:], v2, preferred_element_type=F32) * linv_ref[buf]

        scores(0)
        for head in range(ATT_HEADS):
            if head + 1 < ATT_HEADS:
                scores(head + 1)
            out = weighted_values(head)
            if head % 2 == 0:
                even_ref[...] = out
            else:
                sl = slice(head // 2 * LANES, (head // 2 + 1) * LANES)
                o_ref[0, :, sl] = jnp.where(low, even_ref[...], out).astype(BF16)

    j = pl.program_id(1)
    for missing in range(lead_blocks, 0, -1):
        pl.when(j == lead_blocks - missing)(functools.partial(attend, missing))
    pl.when(j >= lead_blocks)(functools.partial(attend, 0))


BIAS_PERIOD = 1024


def _band_bias_kernel(diag_ref, o_ref, *, qb):
    nkeys = N_KEY_BLOCKS * KEY_BLOCK
    toep = pltpu.roll(jnp.broadcast_to(diag_ref[0], (qb, BIAS_PERIOD)), 0, 1, stride=1, stride_axis=0)
    row = lax.broadcasted_iota(jnp.int32, (qb, nkeys), 0)
    kpos = lax.broadcasted_iota(jnp.int32, (qb, nkeys), 1) - (nkeys - qb)
    chunk_start = (row // CHUNK) * CHUNK
    in_band = jnp.logical_and(kpos >= chunk_start - BAND, kpos < chunk_start + CHUNK)
    o_ref[0] = jnp.where(in_band, toep[:, :nkeys] * LOG2_E, NEG_INF)


def _band_bias(table, qb):
    nkeys = N_KEY_BLOCKS * KEY_BLOCK
    heads = table.shape[0]
    shift = np.arange(BIAS_PERIOD)
    shift = np.where(shift < nkeys, shift, shift - BIAS_PERIOD)
    rel_idx = np.clip(nkeys - qb - shift, -MAX_REL, MAX_REL) + MAX_REL
    diag = table[:, rel_idx].astype(F32).reshape(heads, 1, BIAS_PERIOD)
    return pl.pallas_call(
        functools.partial(_band_bias_kernel, qb=qb),
        grid=(heads,),
        in_specs=[pl.BlockSpec((1, 1, BIAS_PERIOD), lambda h: (h, 0, 0))],
        out_specs=pl.BlockSpec((1, qb, nkeys), lambda h: (h, 0, 0)),
        out_shape=jax.ShapeDtypeStruct((heads, qb, nkeys), F32),
        compiler_params=_params("parallel"),
        name="band_bias",
    )(diag)


def _cached_attn_kernel(q_ref, ktn_ref, vn_ref, kp_ref, vp_ref, bias_ref, o_ref):
    t, n_past = q_ref.shape[1], kp_ref.shape[2]
    low = lax.broadcasted_iota(jnp.int32, (t, LANES), 1) < ATT_DHEAD
    for g in range(ATT_HEADS // 2):
        sl = slice(g * LANES, (g + 1) * LANES)
        q2 = q_ref[0, :, sl].astype(F32)
        kpt, vpt = kp_ref[0, sl, :].astype(BF16), vp_ref[0, sl, :].astype(BF16)
        qs = jnp.concatenate([jnp.where(low, q2, 0.0), jnp.where(low, 0.0, q2)], axis=0).astype(BF16)
        bias = jnp.concatenate([bias_ref[2 * g], bias_ref[2 * g + 1]], axis=0)
        s_past = jnp.dot(qs, kpt, preferred_element_type=F32) + bias[:, :n_past]
        s_new = jnp.dot(qs, ktn_ref[0, sl, :], preferred_element_type=F32) + bias[:, n_past:]
        m = jnp.maximum(jnp.max(s_past, axis=-1, keepdims=True), jnp.max(s_new, axis=-1, keepdims=True))
        p_past, p_new = jnp.exp2(s_past - m), jnp.exp2(s_new - m)
        l = jnp.sum(p_past, axis=-1, keepdims=True) + jnp.sum(p_new, axis=-1, keepdims=True)
        pv = (lax.dot_general(p_past.astype(BF16), vpt, (((1,), (1,)), ((), ())), preferred_element_type=F32)
              + jnp.dot(p_new.astype(BF16), vn_ref[0, :, sl], preferred_element_type=F32)) / l
        o_ref[0, :, sl] = jnp.where(low, pv[:t, :], pv[t:, :]).astype(BF16)


def _cached_attn(q, kt_new, v_new, k_past, v_past, bias):
    b, t, _ = q.shape
    n_past = k_past.shape[2]
    tok = pl.BlockSpec((1, t, ATT_WIDTH), lambda i: (i, 0, 0))
    past = pl.BlockSpec((1, ATT_WIDTH, n_past), lambda i: (i, 0, 0))
    return pl.pallas_call(
        _cached_attn_kernel,
        grid=(b,),
        in_specs=[tok, pl.BlockSpec((1, ATT_WIDTH, t), lambda i: (i, 0, 0)), tok, past, past,
                  _const_spec(bias.shape)],
        out_specs=tok,
        out_shape=jax.ShapeDtypeStruct((b, t, ATT_WIDTH), BF16),
        compiler_params=_params("parallel"),
        name="cached_attn",
    )(q, kt_new, v_new, k_past, v_past, bias)


def _attention(q, kt, v, att_past, bias_tile):
    if att_past is None:
        return _band_attn(q, kt, v, bias_tile)
    k_past, v_past = att_past
    b, t, _ = q.shape
    n_past = k_past.shape[1]
    assert n_past == BAND and t <= CHUNK
    bias = bias_tile[:, :t, :n_past + t]
    feature_major = lambda a: jnp.swapaxes(a.reshape(b, n_past, ATT_WIDTH), 1, 2)
    return _cached_attn(q, kt, v, feature_major(k_past), feature_major(v_past), bias)


def _band_attn(q, kt, v, bias):
    b, tq, _ = q.shape
    qb = bias.shape[1]
    lead = N_KEY_BLOCKS - qb // KEY_BLOCK

    def kidx(s):
        return lambda i, j: jnp.maximum(j - lead + s, 0)

    kt_specs = [pl.BlockSpec((1, ATT_WIDTH, KEY_BLOCK), (lambda f: lambda i, j: (i, 0, f(i, j)))(kidx(s)))
                for s in range(N_KEY_BLOCKS)]
    v_specs = [pl.BlockSpec((1, KEY_BLOCK, ATT_WIDTH), (lambda f: lambda i, j: (i, f(i, j), 0))(kidx(s)))
               for s in range(N_KEY_BLOCKS)]
    tok = pl.BlockSpec((1, qb, ATT_WIDTH), lambda i, j: (i, j, 0))
    return pl.pallas_call(
        functools.partial(_band_attn_kernel, lead_blocks=lead),
        grid=(b, tq // qb),
        in_specs=[tok] + kt_specs + v_specs + [_const_spec(bias.shape)],
        out_specs=tok,
        out_shape=jax.ShapeDtypeStruct((b, tq, ATT_WIDTH), BF16),
        scratch_shapes=[pltpu.VMEM((N_STAGE, qb, N_KEY_BLOCKS * KEY_BLOCK), F32),
                        pltpu.VMEM((N_STAGE, qb, N_KEY_BLOCKS * KEY_BLOCK), BF16),
                        pltpu.VMEM((N_STAGE, qb, LANES), F32),
                        pltpu.VMEM((N_STAGE, qb, LANES), F32),
                        pltpu.VMEM((qb, LANES), F32)],
        compiler_params=_params("parallel", "arbitrary"),
        name="band_attn",
    )(q, kt, kt, kt, v, v, v, bias)


def _mix_ln_kernel(h_ref, hm_ref, oa_ref, sgm_ref, sga_ref, wm_ref, wa_ref, wout_ref, g_ref, b_ref, o_ref):
    pending = None
    for rows in (slice(r, r + FFN_SUB_ROWS) for r in range(0, h_ref.shape[0], FFN_SUB_ROWS)):
        y_ml = jnp.dot(hm_ref[rows, :], wm_ref[...], preferred_element_type=F32)
        y_att = jnp.dot(oa_ref[rows, :], wa_ref[...], preferred_element_type=F32)
        merged = sgm_ref[rows, :].astype(F32) * y_ml + sga_ref[rows, :].astype(F32) * y_att
        mix = jnp.dot(merged.astype(BF16), wout_ref[...], preferred_element_type=F32)
        if pending is not None:
            p_rows, p_y = pending
            o_ref[p_rows, :] = _layer_norm(p_y, g_ref[...], b_ref[...])
        pending = (rows, ALPHA * h_ref[rows, :] + mix)
    p_rows, p_y = pending
    o_ref[p_rows, :] = _layer_norm(p_y, g_ref[...], b_ref[...])


def _mix_ln(h, hm, oa, sgm, sga, wm, wa, wout, g, b):
    n = h.shape[0]
    tm = min(FFN_TILE_ROWS, n)
    row = pl.BlockSpec((tm, D_MODEL), lambda i: (i, 0))
    consts = [wm, wa, wout, g, b]
    return pl.pallas_call(
        _mix_ln_kernel,
        grid=(n // tm,),
        in_specs=[row] * 5 + [_const_spec(a.shape) for a in consts],
        out_specs=row,
        out_shape=jax.ShapeDtypeStruct((n, D_MODEL), F32),
        compiler_params=_params("parallel"),
        name="mix_ln",
    )(h, hm, oa, sgm, sga, *consts)


W_PREP_ROWS = 256


def _w_prep_kernel(a_ref, b_ref, g_ref, main_ref, gate_ref):
    c = pl.program_id(1)
    gate_rows = g_ref.shape[0]

    @pl.when(c < _W_AQ)
    def _():
        main_ref[...] = a_ref[...].T.astype(BF16)

    @pl.when(c >= _W_AQ)
    def _():
        moved = jnp.concatenate([a_ref[gate_rows:, :], b_ref[...]], axis=0)
        scale = jnp.where(c == _W_AQ, ATT_DHEAD ** -0.5 * LOG2_E, 1.0)
        main_ref[...] = (moved * scale).T.astype(BF16)

    @pl.when(c == _W_AQ)
    def _():
        padded = jnp.concatenate([g_ref[...], jnp.zeros((GATE_LANES - gate_rows, g_ref.shape[1]), F32)], axis=0)
        gate_ref[...] = padded.T.astype(BF16)


def _w_prep(w_in):
    w_t = jnp.swapaxes(w_in, 0, 1)
    n_blocks = _W_GA + 1
    gate_rows = _O_AQ - _O_MLI
    per_block = D_MODEL // gate_rows
    return pl.pallas_call(
        _w_prep_kernel,
        grid=(D_MODEL // W_PREP_ROWS, n_blocks),
        in_specs=[pl.BlockSpec((D_MODEL, W_PREP_ROWS), lambda r, c: (c, r)),
                  pl.BlockSpec((gate_rows, W_PREP_ROWS),
                               lambda r, c: (jnp.maximum(c + 1, _W_AQ + 1) * per_block, r)),
                  pl.BlockSpec((gate_rows, W_PREP_ROWS), lambda r, c: (_W_AQ * per_block, r))],
        out_specs=[pl.BlockSpec((W_PREP_ROWS, D_MODEL), lambda r, c: (r, c)),
                   pl.BlockSpec((W_PREP_ROWS, GATE_LANES), lambda r, c: (r, 0))],
        out_shape=[jax.ShapeDtypeStruct((D_MODEL, n_blocks * D_MODEL), BF16),
                   jax.ShapeDtypeStruct((D_MODEL, GATE_LANES), BF16)],
        compiler_params=_params("parallel", "arbitrary"),
        name="w_prep",
    )(w_t, w_t, w_t)


def _prep_weights(p):
    w_main, w_gate = _w_prep(p["w_in"])
    bif = jnp.concatenate([p["b_ml_i"], p["b_ml_f"], jnp.zeros((GATE_LANES - 2 * ML_HEADS,), F32)])
    row = lambda a: a.reshape(1, -1).astype(F32)
    return {
        "w_in": w_main,
        "w_gate": w_gate,
        "bif": row(bif),
        "conv_w": jnp.broadcast_to(p["ml_conv_w"].astype(F32)[:, None, :], (CONV_W, CONV_ROWS, 2 * ML_WIDTH)),
        "conv_b": jnp.broadcast_to(row(p["ml_conv_b"]), (CONV_ROWS, 2 * ML_WIDTH)),
        "norm_g": row(p["ml_norm_g"]),
        "wm": p["w_ml_proj"].astype(BF16),
        "wa": p["w_att_proj"].astype(BF16),
        "wout": p["w_out"].astype(BF16),
        "ffn1": (p["ffn1_w_gu"].astype(BF16), p["ffn1_w_down"].astype(BF16)),
        "ffn2": (p["ffn2_w_gu"].astype(BF16), p["ffn2_w_down"].astype(BF16)),
        "ln1": (row(p["ln1_g"]), row(p["ln1_b"])),
        "ln2": (row(p["ln2_g"]), row(p["ln2_b"])),
        "ln3": (row(p["ln3_g"]), row(p["ln3_b"])),
        "bias_tile": _band_bias(p["att_rel_bias"], KEY_BLOCK),
    }


def _encoder_layer(x, w, conv_prev, ml_state, att_past):
    b, t, _ = x.shape
    n = b * t
    flat = lambda a: a.reshape(n, a.shape[-1])
    h1 = _ffn_ln(flat(x), *w["ffn1"], *w["ln1"])
    (q, k, v, so, gates, new_conv, aq, akt, av, k_rows, v_rows, sgm, sga) = _in_proj(
        h1.reshape(b, t, D_MODEL), conv_prev, w)

    if ml_state is not None:
        c0, n0, m0 = ml_state
        m0p = jnp.pad(m0.astype(F32), ((0, 0), (0, LANES - ML_HEADS))).reshape(b, 1, LANES)
        n0_rep = jnp.broadcast_to(n0.astype(F32)[..., None], n0.shape + (LANES,))
        ml_state = (c0.astype(F32), n0_rep, m0p)
    hm, c1, n1, m1p = _mlstm(q, k, v, so, gates, ml_state, w["norm_g"])
    m1 = m1p[:, 0, :ML_HEADS]

    oa = _attention(aq, akt, av, att_past, w["bias_tile"])
    h2 = _mix_ln(h1, flat(hm), flat(oa), flat(sgm), flat(sga), w["wm"], w["wa"], w["wout"], *w["ln2"])
    y = _ffn_ln(h2, *w["ffn2"], *w["ln3"]).reshape(b, t, D_MODEL)
    keep = k_rows.shape[1]
    state = (new_conv, c1, n1, m1,
             k_rows.reshape(b, keep, ATT_HEADS, ATT_DHEAD), v_rows.reshape(b, keep, ATT_HEADS, ATT_DHEAD))
    return y, state


def kernel(x_prompt, x_sample, state_ml_conv, state_ml_C, state_ml_n, state_ml_m, cache_att_k, cache_att_v,
           w_in, b_ml_i, b_ml_f, ml_conv_w, ml_conv_b, ml_norm_g, att_rel_bias, w_ml_proj, w_att_proj, w_out,
           ffn1_w_gu, ffn1_w_down, ffn2_w_gu, ffn2_w_down, ln1_g, ln1_b, ln2_g, ln2_b, ln3_g, ln3_b):
    depth = w_in.shape[0]
    y_p, y_s = x_prompt, x_sample
    p_states, s_states = [], []
    for l in range(depth):
        w = _prep_weights({
            "w_in": w_in[l], "b_ml_i": b_ml_i[l], "b_ml_f": b_ml_f[l], "ml_conv_w": ml_conv_w[l],
            "ml_conv_b": ml_conv_b[l], "ml_norm_g": ml_norm_g[l], "att_rel_bias": att_rel_bias[l],
            "w_ml_proj": w_ml_proj[l], "w_att_proj": w_att_proj[l], "w_out": w_out[l],
            "ffn1_w_gu": ffn1_w_gu[l], "ffn1_w_down": ffn1_w_down[l], "ffn2_w_gu": ffn2_w_gu[l],
            "ffn2_w_down": ffn2_w_down[l], "ln1_g": ln1_g[l], "ln1_b": ln1_b[l], "ln2_g": ln2_g[l],
            "ln2_b": ln2_b[l], "ln3_g": ln3_g[l], "ln3_b": ln3_b[l]})
        y_p, st_p = _encoder_layer(y_p, w, None, None, None)
        y_s, st_s = _encoder_layer(y_s, w, state_ml_conv[l],
                                   (state_ml_C[l], state_ml_n[l], state_ml_m[l]),
                                   (cache_att_k[l], cache_att_v[l]))
        p_states.append(st_p)
        s_states.append(st_s)
    p_conv, p_c, p_n, p_m, p_k, p_v = [jnp.stack(s) for s in zip(*p_states)]
    s_conv, s_c, s_n, s_m, s_k, s_v = [jnp.stack(s) for s in zip(*s_states)]
    return (y_p, y_s, p_conv, s_conv, p_c, s_c, p_n, s_n, p_m, s_m, p_k, s_k, p_v, s_v)
```

```python
import functools

import jax
import jax.numpy as jnp
import numpy as np
from jax import lax
from jax.experimental import pallas as pl
from jax.experimental.pallas import tpu as pltpu

F32 = jnp.float32
BF16 = jnp.bfloat16

D_MODEL = 1024
CHUNK = 64
ML_HEADS = 4
ML_DHEAD = D_MODEL // ML_HEADS
ML_WIDTH = ML_HEADS * ML_DHEAD
CONV_W = 4
ATT_HEADS = 16
ATT_DHEAD = D_MODEL // ATT_HEADS
ATT_WIDTH = ATT_HEADS * ATT_DHEAD
BAND_CHUNKS = 8
BAND = BAND_CHUNKS * CHUNK
MAX_REL = 128
D_FF = ((8 * D_MODEL) // 3 + 127) // 128 * 128
ALPHA = 2.0 ** 0.25
LN_EPS = 1e-5
NEG_INF = -1e30
LOG2_E = 1.4426950408889634

LANES = 128
GATE_LANES = LANES
VMEM_LIMIT = 56 * 1024 * 1024

_O_MLI = 4 * ML_WIDTH
_O_AQ = _O_MLI + 2 * ML_HEADS
_W_V, _W_O, _W_AQ, _W_AK, _W_AV, _W_GM, _W_GA = 2, 3, 4, 5, 6, 7, 8


def _const_spec(shape, col_block=0):
    index = (0,) * (len(shape) - 1) + (col_block,)
    return pl.BlockSpec(shape, lambda *_: index, pipeline_mode=pl.Buffered(1))


def _params(*sem):
    return pltpu.CompilerParams(dimension_semantics=sem, vmem_limit_bytes=VMEM_LIMIT)


def _layer_norm(y, g, b):
    mu = jnp.mean(y, axis=-1, keepdims=True)
    yc = y - mu
    var = jnp.mean(yc * yc, axis=-1, keepdims=True)
    return yc * lax.rsqrt(var + LN_EPS) * g + b


def _sigmoid(x):
    return 1.0 / (1.0 + jnp.exp(-x))


_FF_CHUNKS = ((0, 768), (768, 1792), (1792, D_FF))
FFN_SUB_ROWS = 256
FFN_TILE_ROWS = 1024


def _ffn_ln_kernel(x_ref, wg_ref, wu_ref, wd_ref, g_ref, b_ref, o_ref):
    pending = None
    for rows in (slice(r, r + FFN_SUB_ROWS) for r in range(0, x_ref.shape[0], FFN_SUB_ROWS)):
        x = x_ref[rows, :]
        xb = x.astype(BF16)
        acc = None
        for s, e in _FF_CHUNKS:
            gate = jnp.dot(xb, wg_ref[:, s:e], preferred_element_type=F32)
            up = jnp.dot(xb, wu_ref[:, s:e], preferred_element_type=F32)
            hid = (gate * _sigmoid(gate) * up).astype(BF16)
            part = jnp.dot(hid, wd_ref[s:e, :], preferred_element_type=F32)
            acc = part if acc is None else acc + part
        if pending is not None:
            p_rows, p_y = pending
            o_ref[p_rows, :] = _layer_norm(p_y, g_ref[...], b_ref[...])
        pending = (rows, ALPHA * x + 0.5 * acc)
    p_rows, p_y = pending
    o_ref[p_rows, :] = _layer_norm(p_y, g_ref[...], b_ref[...])


def _ffn_ln(x, wgu, wd, g, b):
    n = x.shape[0]
    tm = min(FFN_TILE_ROWS, n)
    row = pl.BlockSpec((tm, D_MODEL), lambda i: (i, 0))
    return pl.pallas_call(
        _ffn_ln_kernel,
        grid=(n // tm,),
        in_specs=[row, _const_spec((D_MODEL, D_FF), 0), _const_spec((D_MODEL, D_FF), 1), _const_spec(wd.shape),
                  _const_spec(g.shape), _const_spec(b.shape)],
        out_specs=row,
        out_shape=jax.ShapeDtypeStruct((n, D_MODEL), F32),
        compiler_params=_params("parallel"),
        name="ffn_ln",
    )(x, wgu, wgu, wd, g, b)


PROJ_TILE_ROWS = 512
COL_CHUNK = 256
CONV_ROWS = 16


def _chunk_jobs(h_ref, hb_ref):
    seqs, frames, _ = h_ref.shape
    for s in range(seqs):
        hb_ref[s * frames:(s + 1) * frames, :] = h_ref[s].astype(BF16)

    def proj(w_ref, c):
        cols = slice(c * COL_CHUNK, (c + 1) * COL_CHUNK)
        return jnp.dot(hb_ref[...], w_ref[:, cols], preferred_element_type=F32), cols

    def scatter(o_ref, cols, y):
        for s in range(seqs):
            o_ref[s, :, cols] = y[s * frames:(s + 1) * frames, :]

    def plain(w_ref, o_ref):
        def job(c):
            y, cols = proj(w_ref, c)
            scatter(o_ref, cols, y.astype(BF16))
        return job

    def gated(w_ref, o_ref):
        def job(c):
            y, cols = proj(w_ref, c)
            scatter(o_ref, cols, _sigmoid(y).astype(BF16))
        return job

    return proj, scatter, plain, gated


def _interleave(*queues):
    order = []
    for i in range(max(len(q) for q in queues)):
        for q in queues:
            order += q[i:i + 1]
    return order


def _in_proj_ml_kernel(h_ref, *refs, fresh):
    cprev_ref = None if fresh else refs[0]
    (wqk_ref, wv_ref, wo_ref, wif_ref, bif_ref, cw_ref, cb_ref,
     q_ref, kt_ref, v_ref, so_ref, gt_ref, conv_ref, hb_ref, raw_ref, kc_ref) = refs[0 if fresh else 1:]
    seqs, frames, _ = h_ref.shape
    proj, scatter, plain, gated = _chunk_jobs(h_ref, hb_ref)

    @pl.when(pl.program_id(1) == 0)
    def _():
        for s in range(seqs):
            raw_ref[s, 0:8, :] = jnp.zeros((8, 2 * ML_WIDTH), F32)
            if not fresh:
                raw_ref[s, 8 - (CONV_W - 1):8, :] = cprev_ref[s]

    def qk_raw(c):
        y, cols = proj(wqk_ref, c)
        for s in range(seqs):
            raw_ref[s, 8:8 + frames, cols] = y[s * frames:(s + 1) * frames, :]

    def qk_conv(c):
        cols = slice(c * COL_CHUNK, (c + 1) * COL_CHUNK)
        is_k = c * COL_CHUNK >= ML_WIDTH
        sub = lax.broadcasted_iota(jnp.int32, (8, COL_CHUNK), 0)
        for s in range(seqs):
            for r in range(frames // CONV_ROWS):
                base = 8 + r * CONV_ROWS
                blocks = [raw_ref[s, base + 8 * i:base + 8 * i + 8, cols] for i in range(-1, CONV_ROWS // 8)]
                taps = [jnp.concatenate(blocks[1:], axis=0)]
                for j in range(1, CONV_W):
                    taps.append(jnp.concatenate(
                        [pltpu.roll(jnp.where(sub < 8 - j, cur, prev), j, axis=0)
                         for prev, cur in zip(blocks[:-1], blocks[1:])], axis=0))
                acc = cb_ref[:, cols]
                for j in range(CONV_W):
                    acc = acc + taps[j] * cw_ref[CONV_W - 1 - j, :, cols]
                qk = acc * _sigmoid(acc)
                rows = slice(r * CONV_ROWS, (r + 1) * CONV_ROWS)
                if is_k:
                    kc_ref[rows, :] = qk * (ML_DHEAD ** -0.5)
                else:
                    q_ref[s, rows, cols] = qk.astype(BF16)
            if is_k:
                kt_ref[s, cols.start - ML_WIDTH:cols.stop - ML_WIDTH, :] = kc_ref[...].astype(BF16).T

    n_chunks = D_MODEL // COL_CHUNK
    light = [functools.partial(job, c) for c in range(n_chunks)
             for job in (plain(wv_ref, v_ref), gated(wo_ref, so_ref))]
    qk_raw(0)
    for c in range(2 * n_chunks):
        if c + 1 < 2 * n_chunks:
            qk_raw(c + 1)
        light[c]()
        qk_conv(c)
    for s in range(seqs):
        conv_ref[s] = raw_ref[s, 8 + frames - (CONV_W - 1):8 + frames, :]
        raw_ref[s, 0:8, :] = raw_ref[s, frames:frames + 8, :]

    zg = jnp.dot(hb_ref[...], wif_ref[...], preferred_element_type=F32) + bif_ref[...]
    lane = lax.broadcasted_iota(jnp.int32, zg.shape, 1)
    log_sig = jnp.minimum(zg, 0.0) - jnp.log(1.0 + jnp.exp(-jnp.abs(zg)))
    scatter(gt_ref, slice(None), jnp.where(lane < ML_HEADS, zg, log_sig))


def _in_proj_att_kernel(h_ref, waq_ref, wak_ref, wav_ref, wgm_ref, wga_ref,
                        aq_ref, akt_ref, av_ref, pk_ref, pv_ref, sgm_ref, sga_ref, hb_ref):
    seqs, frames, _ = h_ref.shape
    proj, scatter, plain, gated = _chunk_jobs(h_ref, hb_ref)

    def att_k(c):
        y, cols = proj(wak_ref, c)
        scatter(pk_ref, cols, y)
        yb = y.astype(BF16)
        for s in range(seqs):
            akt_ref[s, cols, :] = yb[s * frames:(s + 1) * frames, :].T

    def att_v(c):
        y, cols = proj(wav_ref, c)
        scatter(pv_ref, cols, y)
        scatter(av_ref, cols, y.astype(BF16))

    n_chunks = D_MODEL // COL_CHUNK
    heavy = [functools.partial(job, c) for c in range(n_chunks)
             for job in (gated(wgm_ref, sgm_ref), att_k, gated(wga_ref, sga_ref))]
    light = [functools.partial(job, c) for c in range(n_chunks) for job in (plain(waq_ref, aq_ref), att_v)]
    for job in _interleave(heavy, light):
        job()


def _in_proj(h, conv_prev, w):
    b, t, _ = h.shape
    tm = min(PROJ_TILE_ROWS, t)
    sb = max(1, min(b, PROJ_TILE_ROWS // t))
    nt = t // tm
    keep = min(BAND, t)
    first_kept = (t - keep) // tm
    h_spec = pl.BlockSpec((sb, tm, D_MODEL), lambda i, j: (i, j, 0))

    def tok(width, dtype):
        return (pl.BlockSpec((sb, tm, width), lambda i, j: (i, j, 0)),
                jax.ShapeDtypeStruct((b, t, width), dtype))

    def kept(width):
        return (pl.BlockSpec((sb, tm, width), lambda i, j: (i, jnp.maximum(j - first_kept, 0), 0)),
                jax.ShapeDtypeStruct((b, keep, width), F32))

    conv_spec = pl.BlockSpec((sb, CONV_W - 1, 2 * ML_WIDTH), lambda i, j: (i, 0, 0))
    feature_major = (pl.BlockSpec((sb, ATT_WIDTH, tm), lambda i, j: (i, 0, j)),
                     jax.ShapeDtypeStruct((b, ATT_WIDTH, t), BF16))
    ml_outs = [
        tok(ML_WIDTH, BF16), feature_major, tok(ML_WIDTH, BF16), tok(ML_WIDTH, BF16),
        tok(GATE_LANES, F32),
        (conv_spec, jax.ShapeDtypeStruct((b, CONV_W - 1, 2 * ML_WIDTH), F32)),
    ]
    w_in = w["w_in"]
    square = (D_MODEL, D_MODEL)
    ml_weights = [w_in, w_in, w_in, w["w_gate"], w["bif"], w["conv_w"], w["conv_b"]]
    fresh = conv_prev is None
    carried = [] if fresh else [conv_prev]
    ml = pl.pallas_call(
        functools.partial(_in_proj_ml_kernel, fresh=fresh),
        grid=(b // sb, nt),
        in_specs=[h_spec] + [conv_spec] * len(carried)
                 + [_const_spec((D_MODEL, 2 * ML_WIDTH), 0), _const_spec(square, _W_V), _const_spec(square, _W_O)]
                 + [_const_spec(a.shape) for a in ml_weights[3:]],
        out_specs=[o[0] for o in ml_outs],
        out_shape=[o[1] for o in ml_outs],
        scratch_shapes=[pltpu.VMEM((sb * tm, D_MODEL), BF16), pltpu.VMEM((sb, tm + 8, 2 * ML_WIDTH), F32),
                        pltpu.VMEM((tm, COL_CHUNK), F32)],
        compiler_params=_params("parallel", "arbitrary"),
        name="in_proj_ml",
    )(h, *carried, *ml_weights)

    att_outs = [
        tok(ATT_WIDTH, BF16),
        feature_major,
        tok(ATT_WIDTH, BF16),
        kept(ATT_WIDTH), kept(ATT_WIDTH),
        tok(D_MODEL, BF16), tok(D_MODEL, BF16),
    ]
    att_weights = [w_in] * 5
    att = pl.pallas_call(
        _in_proj_att_kernel,
        grid=(b // sb, nt),
        in_specs=[h_spec] + [_const_spec(square, c) for c in (_W_AQ, _W_AK, _W_AV, _W_GM, _W_GA)],
        out_specs=[o[0] for o in att_outs],
        out_shape=[o[1] for o in att_outs],
        scratch_shapes=[pltpu.VMEM((sb * tm, D_MODEL), BF16)],
        compiler_params=_params("parallel", "arbitrary"),
        name="in_proj_att",
    )(h, *att_weights)
    return tuple(ml) + tuple(att)


ML_STREAMS = 1


def _split3(x):
    hi = x.astype(BF16)
    rest = x - hi.astype(F32)
    mid = rest.astype(BF16)
    return hi, mid, (rest - mid.astype(F32)).astype(BF16)


def _mlstm_kernel(q_ref, kt_ref, v_ref, so_ref, gt_ref, *refs, blk, fresh):
    c0_ref, n0_ref, m0_ref = (None, None, None) if fresh else refs[:3]
    (ng_ref, hm_ref, c_ref, n_ref, m_ref,
     ncol_ref, d_ref, s_ref, hh_ref, wi_ref, emt_ref, rs_ref) = refs[0 if fresh else 3:]
    heads = range(ML_HEADS)
    streams = range(q_ref.shape[0])
    rb = min(ROW_BLOCK, blk)
    row_blocks = [slice(r * rb, (r + 1) * rb) for r in range(blk // rb)]
    head_cols = [slice(j * ML_DHEAD, (j + 1) * ML_DHEAD) for j in heads]
    slot = lambda s, j: s * ML_HEADS + j

    @pl.when(pl.program_id(1) == 0)
    def _():
        if fresh:
            c_ref[...] = jnp.zeros_like(c_ref)
            ncol_ref[...] = jnp.zeros_like(ncol_ref)
            m_ref[...] = jnp.zeros_like(m_ref)
        else:
            c_ref[...] = c0_ref[...]
            for s in streams:
                for j in heads:
                    ncol_ref[slot(s, j)] = n0_ref[s, j]
            m_ref[...] = m0_ref[...]

    tril = (lax.broadcasted_iota(jnp.int32, (blk, blk), 1)
            <= lax.broadcasted_iota(jnp.int32, (blk, blk), 0)).astype(BF16)
    bcum, li_rel_t, m_prev, m_new, decay, w_s_t = [], [], [], [], [], []
    for s in streams:
        gates = gt_ref[s]
        csum = sum(jnp.dot(tril, part, preferred_element_type=F32) for part in _split3(gates))
        bcum.append(pltpu.roll(csum, LANES - ML_HEADS, axis=1))
        li_rel = gates - bcum[s]
        li_rel_t.append(li_rel.T)
        m_prev.append(m_ref[s])
        b_last = bcum[s][blk - 1:blk, :]
        g_s = b_last + li_rel
        m_new.append(jnp.maximum(b_last + m_prev[s], jnp.max(g_s, axis=0, keepdims=True)))
        decay.append(jnp.exp(b_last + m_prev[s] - m_new[s]))
        w_s_t.append(jnp.exp(g_s - m_new[s]).T)

    def lane_of(x, j):
        lane = lax.broadcasted_iota(jnp.int32, x.shape, 1)
        return jnp.sum(jnp.where(lane == j, x, 0.0), axis=-1, keepdims=True)

    def spread(x, width):
        return x[:, :width] if width <= LANES else jnp.concatenate([x] * (width // LANES), axis=1)

    pairs = [(s, j) for s in streams for j in heads]
    units = [(s, j, rows) for s, j in pairs for rows in row_blocks]

    for s, j, rows in units:
        wi_ref[slot(s, j), rows, :] = jnp.broadcast_to(lane_of(bcum[s][rows, :], j), (rb, LANES))
    m_prev_j = {(s, j): lane_of(m_prev[s], j) for s, j in pairs}
    for s, j, rows in units:
        u = slot(s, j)
        causal = (lax.broadcasted_iota(jnp.int32, (rb, blk), 1)
                  <= lax.broadcasted_iota(jnp.int32, (rb, blk), 0) + rows.start)
        bcol = wi_ref[u, rows, :]
        dmat = jnp.where(causal, spread(bcol, blk) + li_rel_t[s][j:j + 1, :], -jnp.inf)
        icol = bcol + m_prev_j[s, j]
        m_t = jnp.maximum(icol, jnp.max(dmat, axis=-1, keepdims=True))
        d_ref[u, rows, :] = jnp.exp(dmat - spread(m_t, blk))
        wi_ref[u, rows, :] = jnp.exp(icol - m_t)
        emt_ref[u, rows, :] = jnp.exp(-m_t)

    for s, j in pairs:
        u = slot(s, j)
        qk = jnp.dot(q_ref[s, :, head_cols[j]], kt_ref[s, head_cols[j], :], preferred_element_type=F32)
        for rows in row_blocks:
            sc = qk[rows, :] * d_ref[u, rows, :]
            rs_ref[u, rows, :] = jnp.broadcast_to(jnp.sum(sc, axis=-1, keepdims=True), (rb, LANES))
            s_ref[u, rows, :] = sc.astype(BF16)

    for s, j in pairs:
        u = slot(s, j)
        qj, vj = q_ref[s, :, head_cols[j]], v_ref[s, :, head_cols[j]]
        q_c = jnp.dot(qj, c_ref[s, j].astype(BF16), preferred_element_type=F32)
        s_v = jnp.dot(s_ref[u], vj, preferred_element_type=F32)
        q_n = jnp.dot(qj, ncol_ref[u].astype(BF16), preferred_element_type=F32)
        for rows in row_blocks:
            w_inter = wi_ref[u, rows, :]
            den = w_inter * q_n[rows, :] + rs_ref[u, rows, :]
            scale = 1.0 / jnp.maximum(jnp.abs(den), emt_ref[u, rows, :])
            hh = (spread(w_inter, ML_DHEAD) * q_c[rows, :] + s_v[rows, :]) * spread(scale, ML_DHEAD)
            hh_ref[u, rows, :] = hh
            wi_ref[u, rows, :] = jnp.broadcast_to(jnp.mean(hh, axis=-1, keepdims=True), (rb, LANES))
    for s, j, rows in units:
        u = slot(s, j)
        hc = hh_ref[u, rows, :] - spread(wi_ref[u, rows, :], ML_DHEAD)
        var = jnp.mean(hc * hc, axis=-1, keepdims=True)
        hn = hc * lax.rsqrt(var + LN_EPS) * ng_ref[:, head_cols[j]] * so_ref[s, rows, head_cols[j]].astype(F32)
        hm_ref[s, rows, head_cols[j]] = hn.astype(BF16)

    ones = jnp.ones((blk, LANES), BF16)
    for s, j in pairs:
        u = slot(s, j)
        kw = (kt_ref[s, head_cols[j], :].astype(F32) * w_s_t[s][j:j + 1, :]).astype(BF16)
        d_j = lane_of(decay[s], j)
        c_ref[s, j] = d_j * c_ref[s, j] + jnp.dot(kw, v_ref[s, :, head_cols[j]], preferred_element_type=F32)
        ncol_ref[u] = d_j * ncol_ref[u] + jnp.dot(kw, ones, preferred_element_type=F32)
    for s in streams:
        m_ref[s] = m_new[s]

    @pl.when(pl.program_id(1) == pl.num_programs(1) - 1)
    def _():
        for s, j in pairs:
            n_ref[s, j:j + 1, :] = ncol_ref[slot(s, j)].T[0:1, :]


def _mlstm(q, kt, v, so, gates, state, norm_g):
    b, t, _ = q.shape
    blk = min(256, t)
    nb = ML_STREAMS if b % ML_STREAMS == 0 else 1
    carried = [] if state is None else list(state)
    tok = pl.BlockSpec((nb, blk, ML_WIDTH), lambda i, j: (i, j, 0))
    c_spec = pl.BlockSpec((nb, ML_HEADS, ML_DHEAD, ML_DHEAD), lambda i, j: (i, 0, 0, 0))
    n_spec = pl.BlockSpec((nb, ML_HEADS, ML_DHEAD), lambda i, j: (i, 0, 0))
    m_spec = pl.BlockSpec((nb, 1, LANES), lambda i, j: (i, 0, 0))
    per_head = lambda width, dtype: pltpu.VMEM((nb * ML_HEADS, blk, width), dtype)
    return pl.pallas_call(
        functools.partial(_mlstm_kernel, blk=blk, fresh=state is None),
        grid=(b // nb, t // blk),
        in_specs=[tok, pl.BlockSpec((nb, ML_WIDTH, blk), lambda i, j: (i, 0, j)), tok, tok,
                  pl.BlockSpec((nb, blk, GATE_LANES), lambda i, j: (i, j, 0))]
                 + [c_spec, pl.BlockSpec((nb, ML_HEADS, ML_DHEAD, LANES), lambda i, j: (i, 0, 0, 0)),
                    m_spec][:len(carried)]
                 + [_const_spec(norm_g.shape)],
        out_specs=[tok, c_spec, n_spec, m_spec],
        out_shape=[jax.ShapeDtypeStruct((b, t, ML_WIDTH), BF16),
                   jax.ShapeDtypeStruct((b, ML_HEADS, ML_DHEAD, ML_DHEAD), F32),
                   jax.ShapeDtypeStruct((b, ML_HEADS, ML_DHEAD), F32),
                   jax.ShapeDtypeStruct((b, 1, LANES), F32)],
        scratch_shapes=[pltpu.VMEM((nb * ML_HEADS, ML_DHEAD, LANES), F32),
                        per_head(blk, F32), per_head(blk, BF16), per_head(ML_DHEAD, F32),
                        per_head(LANES, F32), per_head(LANES, F32), per_head(LANES, F32)],
        compiler_params=_params("parallel", "arbitrary"),
        name="mlstm",
    )(q, kt, v, so, gates, *carried, norm_g)


KEY_BLOCK = 256
N_KEY_BLOCKS = 3
N_STAGE = 4
PIPE_AHEAD = 1
ROW_BLOCK = 64


def _band_attn_kernel(q_ref, kt0_ref, kt1_ref, kt2_ref, v0_ref, v1_ref, v2_ref, bias_ref, o_ref,
                      s_ref, p_ref, mx_ref, linv_ref, even_ref, *, lead_blocks):
    kt_refs = (kt0_ref, kt1_ref, kt2_ref)
    v_refs = (v0_ref, v1_ref, v2_ref)
    qb = q_ref.shape[1]
    lane = lax.broadcasted_iota(jnp.int32, (qb, LANES), 1)
    low = lane < ATT_DHEAD

    def attend(first_slot):
        slots = range(first_slot, N_KEY_BLOCKS)
        k_lo = first_slot * KEY_BLOCK

        def scores(head):
            sl = slice(head // 2 * LANES, (head // 2 + 1) * LANES)
            q2 = q_ref[0, :, sl].astype(F32)
            qh = jnp.where(low if head % 2 == 0 else jnp.logical_not(low), q2, 0.0).astype(BF16)
            row_max = None
            for c in slots:
                cols = slice(c * KEY_BLOCK, (c + 1) * KEY_BLOCK)
                s = jnp.dot(qh, kt_refs[c][0, sl, :], preferred_element_type=F32) + bias_ref[head, :, cols]
                s_ref[head % N_STAGE, :, cols] = s
                part = jnp.maximum(s[:, :LANES], s[:, LANES:])
                row_max = part if row_max is None else jnp.maximum(row_max, part)
            mx_ref[head % N_STAGE] = row_max

        def weighted_values(head):
            sl = slice(head // 2 * LANES, (head // 2 + 1) * LANES)
            buf = head % N_STAGE
            rb = min(ROW_BLOCK, qb)
            for r in range(qb // rb):
                rows = slice(r * rb, (r + 1) * rb)
                m = jnp.max(mx_ref[buf, rows, :], axis=-1, keepdims=True)
                row_sum = None
                for c in slots:
                    cols = slice(c * KEY_BLOCK, (c + 1) * KEY_BLOCK)
                    p = jnp.exp2(s_ref[buf, rows, cols] - m)
                    p_ref[buf, rows, cols] = p.astype(BF16)
                    part = p[:, :LANES] + p[:, LANES:]
                    row_sum = part if row_sum is None else row_sum + part
                l = jnp.sum(row_sum, axis=-1, keepdims=True)
                linv_ref[buf, rows, :] = jnp.broadcast_to(1.0 / l, (rb, LANES))
            v2 = jnp.concatenate([v_refs[c][0, :, sl] for c in slots], axis=0)
            return jnp.dot(p_ref[buf, :, k_lo:], v2, preferred_element_type=F32) * linv_ref[buf]

        for head in range(PIPE_AHEAD):
            scores(head)
        for head in range(ATT_HEADS):
            if head + PIPE_AHEAD < ATT_HEADS:
                scores(head + PIPE_AHEAD)
            out = weighted_values(head)
            if head % 2 == 0:
                even_ref[...] = out
            else:
                sl = slice(head // 2 * LANES, (head // 2 + 1) * LANES)
                o_ref[0, :, sl] = jnp.where(low, even_ref[...], out).astype(BF16)

    j = pl.program_id(1)
    for missing in range(lead_blocks, 0, -1):
        pl.when(j == lead_blocks - missing)(functools.partial(attend, missing))
    pl.when(j >= lead_blocks)(functools.partial(attend, 0))


BIAS_PERIOD = 1024


BIAS_HEADS_PER_STEP = 4


def _band_bias_kernel(diag_ref, o_ref, *, qb):
    nkeys = N_KEY_BLOCKS * KEY_BLOCK
    row = lax.broadcasted_iota(jnp.int32, (qb, nkeys), 0)
    kpos = lax.broadcasted_iota(jnp.int32, (qb, nkeys), 1) - (nkeys - qb)
    chunk_start = (row // CHUNK) * CHUNK
    in_band = jnp.logical_and(kpos >= chunk_start - BAND, kpos < chunk_start + CHUNK)
    for h in range(diag_ref.shape[0]):
        toep = pltpu.roll(jnp.broadcast_to(diag_ref[h], (qb, BIAS_PERIOD)), 0, 1, stride=1, stride_axis=0)
        o_ref[h] = jnp.where(in_band, toep[:, :nkeys] * LOG2_E, NEG_INF)


def _band_bias(table, qb):
    nkeys = N_KEY_BLOCKS * KEY_BLOCK
    heads = table.shape[0]
    shift = np.arange(BIAS_PERIOD)
    shift = np.where(shift < nkeys, shift, shift - BIAS_PERIOD)
    rel_idx = np.clip(nkeys - qb - shift, -MAX_REL, MAX_REL) + MAX_REL
    diag = table[:, rel_idx].astype(F32).reshape(heads, 1, BIAS_PERIOD)
    return pl.pallas_call(
        functools.partial(_band_bias_kernel, qb=qb),
        grid=(heads // BIAS_HEADS_PER_STEP,),
        in_specs=[pl.BlockSpec((BIAS_HEADS_PER_STEP, 1, BIAS_PERIOD), lambda h: (h, 0, 0))],
        out_specs=pl.BlockSpec((BIAS_HEADS_PER_STEP, qb, nkeys), lambda h: (h, 0, 0)),
        out_shape=jax.ShapeDtypeStruct((heads, qb, nkeys), F32),
        compiler_params=_params("parallel"),
        name="band_bias",
    )(diag)


def _cached_attn_kernel(q_ref, ktn_ref, vn_ref, kp_ref, vp_ref, bias_ref, o_ref):
    t, n_past = q_ref.shape[1], kp_ref.shape[2]
    low = lax.broadcasted_iota(jnp.int32, (t, LANES), 1) < ATT_DHEAD
    for g in range(ATT_HEADS // 2):
        sl = slice(g * LANES, (g + 1) * LANES)
        q2 = q_ref[0, :, sl].astype(F32)
        kpt, vpt = kp_ref[0, sl, :].astype(BF16), vp_ref[0, sl, :].astype(BF16)
        qs = jnp.concatenate([jnp.where(low, q2, 0.0), jnp.where(low, 0.0, q2)], axis=0).astype(BF16)
        bias = jnp.concatenate([bias_ref[2 * g], bias_ref[2 * g + 1]], axis=0)
        s_past = jnp.dot(qs, kpt, preferred_element_type=F32) + bias[:, :n_past]
        s_new = jnp.dot(qs, ktn_ref[0, sl, :], preferred_element_type=F32) + bias[:, n_past:]
        m = jnp.maximum(jnp.max(s_past, axis=-1, keepdims=True), jnp.max(s_new, axis=-1, keepdims=True))
        p_past, p_new = jnp.exp2(s_past - m), jnp.exp2(s_new - m)
        l = jnp.sum(p_past, axis=-1, keepdims=True) + jnp.sum(p_new, axis=-1, keepdims=True)
        pv = (lax.dot_general(p_past.astype(BF16), vpt, (((1,), (1,)), ((), ())), preferred_element_type=F32)
              + jnp.dot(p_new.astype(BF16), vn_ref[0, :, sl], preferred_element_type=F32)) / l
        o_ref[0, :, sl] = jnp.where(low, pv[:t, :], pv[t:, :]).astype(BF16)


def _cached_attn(q, kt_new, v_new, k_past, v_past, bias):
    b, t, _ = q.shape
    n_past = k_past.shape[2]
    tok = pl.BlockSpec((1, t, ATT_WIDTH), lambda i: (i, 0, 0))
    past = pl.BlockSpec((1, ATT_WIDTH, n_past), lambda i: (i, 0, 0))
    return pl.pallas_call(
        _cached_attn_kernel,
        grid=(b,),
        in_specs=[tok, pl.BlockSpec((1, ATT_WIDTH, t), lambda i: (i, 0, 0)), tok, past, past,
                  _const_spec(bias.shape)],
        out_specs=tok,
        out_shape=jax.ShapeDtypeStruct((b, t, ATT_WIDTH), BF16),
        compiler_params=_params("parallel"),
        name="cached_attn",
    )(q, kt_new, v_new, k_past, v_past, bias)


def _attention(q, kt, v, att_past, bias_tile):
    if att_past is None:
        return _band_attn(q, kt, v, bias_tile)
    k_past, v_past = att_past
    b, t, _ = q.shape
    n_past = k_past.shape[1]
    assert n_past == BAND and t <= CHUNK
    bias = bias_tile[:, :t, :n_past + t]
    feature_major = lambda a: jnp.swapaxes(a.reshape(b, n_past, ATT_WIDTH), 1, 2)
    return _cached_attn(q, kt, v, feature_major(k_past), feature_major(v_past), bias)


def _band_attn(q, kt, v, bias):
    b, tq, _ = q.shape
    qb = bias.shape[1]
    lead = N_KEY_BLOCKS - qb // KEY_BLOCK

    def kidx(s):
        return lambda i, j: jnp.maximum(j - lead + s, 0)

    kt_specs = [pl.BlockSpec((1, ATT_WIDTH, KEY_BLOCK), (lambda f: lambda i, j: (i, 0, f(i, j)))(kidx(s)))
                for s in range(N_KEY_BLOCKS)]
    v_specs = [pl.BlockSpec((1, KEY_BLOCK, ATT_WIDTH), (lambda f: lambda i, j: (i, f(i, j), 0))(kidx(s)))
               for s in range(N_KEY_BLOCKS)]
    tok = pl.BlockSpec((1, qb, ATT_WIDTH), lambda i, j: (i, j, 0))
    return pl.pallas_call(
        functools.partial(_band_attn_kernel, lead_blocks=lead),
        grid=(b, tq // qb),
        in_specs=[tok] + kt_specs + v_specs + [_const_spec(bias.shape)],
        out_specs=tok,
        out_shape=jax.ShapeDtypeStruct((b, tq, ATT_WIDTH), BF16),
        scratch_shapes=[pltpu.VMEM((N_STAGE, qb, N_KEY_BLOCKS * KEY_BLOCK), F32),
                        pltpu.VMEM((N_STAGE, qb, N_KEY_BLOCKS * KEY_BLOCK), BF16),
                        pltpu.VMEM((N_STAGE, qb, LANES), F32),
                        pltpu.VMEM((N_STAGE, qb, LANES), F32),
                        pltpu.VMEM((qb, LANES), F32)],
        compiler_params=_params("parallel", "arbitrary"),
        name="band_attn",
    )(q, kt, kt, kt, v, v, v, bias)


def _mix_ln_kernel(h_ref, hm_ref, oa_ref, sgm_ref, sga_ref, wm_ref, wa_ref, wout_ref, g_ref, b_ref, o_ref):
    pending = None
    for rows in (slice(r, r + FFN_SUB_ROWS) for r in range(0, h_ref.shape[0], FFN_SUB_ROWS)):
        y_ml = jnp.dot(hm_ref[rows, :], wm_ref[...], preferred_element_type=F32)
        y_att = jnp.dot(oa_ref[rows, :], wa_ref[...], preferred_element_type=F32)
        merged = sgm_ref[rows, :].astype(F32) * y_ml + sga_ref[rows, :].astype(F32) * y_att
        mix = jnp.dot(merged.astype(BF16), wout_ref[...], preferred_element_type=F32)
        if pending is not None:
            p_rows, p_y = pending
            o_ref[p_rows, :] = _layer_norm(p_y, g_ref[...], b_ref[...])
        pending = (rows, ALPHA * h_ref[rows, :] + mix)
    p_rows, p_y = pending
    o_ref[p_rows, :] = _layer_norm(p_y, g_ref[...], b_ref[...])


def _mix_ln(h, hm, oa, sgm, sga, wm, wa, wout, g, b):
    n = h.shape[0]
    tm = min(FFN_TILE_ROWS, n)
    row = pl.BlockSpec((tm, D_MODEL), lambda i: (i, 0))
    consts = [wm, wa, wout, g, b]
    return pl.pallas_call(
        _mix_ln_kernel,
        grid=(n // tm,),
        in_specs=[row] * 5 + [_const_spec(a.shape) for a in consts],
        out_specs=row,
        out_shape=jax.ShapeDtypeStruct((n, D_MODEL), F32),
        compiler_params=_params("parallel"),
        name="mix_ln",
    )(h, hm, oa, sgm, sga, *consts)


W_PREP_ROWS = 512


def _w_prep_kernel(a_ref, b_ref, g_ref, main_ref, gate_ref):
    c = pl.program_id(1)
    gate_rows = g_ref.shape[0]

    @pl.when(c < _W_AQ)
    def _():
        main_ref[...] = a_ref[...].T.astype(BF16)

    @pl.when(c >= _W_AQ)
    def _():
        moved = jnp.concatenate([a_ref[gate_rows:, :], b_ref[...]], axis=0)
        scale = jnp.where(c == _W_AQ, ATT_DHEAD ** -0.5 * LOG2_E, 1.0)
        main_ref[...] = (moved * scale).T.astype(BF16)

    @pl.when(c == _W_AQ)
    def _():
        padded = jnp.concatenate([g_ref[...], jnp.zeros((GATE_LANES - gate_rows, g_ref.shape[1]), F32)], axis=0)
        gate_ref[...] = padded.T.astype(BF16)


def _w_prep(w_in):
    w_t = jnp.swapaxes(w_in, 0, 1)
    n_blocks = _W_GA + 1
    gate_rows = _O_AQ - _O_MLI
    per_block = D_MODEL // gate_rows
    return pl.pallas_call(
        _w_prep_kernel,
        grid=(D_MODEL // W_PREP_ROWS, n_blocks),
        in_specs=[pl.BlockSpec((D_MODEL, W_PREP_ROWS), lambda r, c: (c, r)),
                  pl.BlockSpec((gate_rows, W_PREP_ROWS),
                               lambda r, c: (jnp.maximum(c + 1, _W_AQ + 1) * per_block, r)),
                  pl.BlockSpec((gate_rows, W_PREP_ROWS), lambda r, c: (_W_AQ * per_block, r))],
        out_specs=[pl.BlockSpec((W_PREP_ROWS, D_MODEL), lambda r, c: (r, c)),
                   pl.BlockSpec((W_PREP_ROWS, GATE_LANES), lambda r, c: (r, 0))],
        out_shape=[jax.ShapeDtypeStruct((D_MODEL, n_blocks * D_MODEL), BF16),
                   jax.ShapeDtypeStruct((D_MODEL, GATE_LANES), BF16)],
        compiler_params=_params("parallel", "arbitrary"),
        name="w_prep",
    )(w_t, w_t, w_t)


def _prep_weights(p):
    w_main, w_gate = _w_prep(p["w_in"])
    bif = jnp.concatenate([p["b_ml_i"], p["b_ml_f"], jnp.zeros((GATE_LANES - 2 * ML_HEADS,), F32)])
    row = lambda a: a.reshape(1, -1).astype(F32)
    return {
        "w_in": w_main,
        "w_gate": w_gate,
        "bif": row(bif),
        "conv_w": jnp.broadcast_to(p["ml_conv_w"].astype(F32)[:, None, :], (CONV_W, CONV_ROWS, 2 * ML_WIDTH)),
        "conv_b": jnp.broadcast_to(row(p["ml_conv_b"]), (CONV_ROWS, 2 * ML_WIDTH)),
        "norm_g": row(p["ml_norm_g"]),
        "wm": p["w_ml_proj"].astype(BF16),
        "wa": p["w_att_proj"].astype(BF16),
        "wout": p["w_out"].astype(BF16),
        "ffn1": (p["ffn1_w_gu"].astype(BF16), p["ffn1_w_down"].astype(BF16)),
        "ffn2": (p["ffn2_w_gu"].astype(BF16), p["ffn2_w_down"].astype(BF16)),
        "ln1": (row(p["ln1_g"]), row(p["ln1_b"])),
        "ln2": (row(p["ln2_g"]), row(p["ln2_b"])),
        "ln3": (row(p["ln3_g"]), row(p["ln3_b"])),
        "bias_tile": _band_bias(p["att_rel_bias"], KEY_BLOCK),
    }


def _encoder_layer(x, w, conv_prev, ml_state, att_past):
    b, t, _ = x.shape
    n = b * t
    flat = lambda a: a.reshape(n, a.shape[-1])
    h1 = _ffn_ln(flat(x), *w["ffn1"], *w["ln1"])
    (q, k, v, so, gates, new_conv, aq, akt, av, k_rows, v_rows, sgm, sga) = _in_proj(
        h1.reshape(b, t, D_MODEL), conv_prev, w)

    if ml_state is not None:
        c0, n0, m0 = ml_state
        m0p = jnp.pad(m0.astype(F32), ((0, 0), (0, LANES - ML_HEADS))).reshape(b, 1, LANES)
        n0_rep = jnp.broadcast_to(n0.astype(F32)[..., None], n0.shape + (LANES,))
        ml_state = (c0.astype(F32), n0_rep, m0p)
    hm, c1, n1, m1p = _mlstm(q, k, v, so, gates, ml_state, w["norm_g"])
    m1 = m1p[:, 0, :ML_HEADS]

    oa = _attention(aq, akt, av, att_past, w["bias_tile"])
    h2 = _mix_ln(h1, flat(hm), flat(oa), flat(sgm), flat(sga), w["wm"], w["wa"], w["wout"], *w["ln2"])
    y = _ffn_ln(h2, *w["ffn2"], *w["ln3"]).reshape(b, t, D_MODEL)
    keep = k_rows.shape[1]
    state = (new_conv, c1, n1, m1,
             k_rows.reshape(b, keep, ATT_HEADS, ATT_DHEAD), v_rows.reshape(b, keep, ATT_HEADS, ATT_DHEAD))
    return y, state


def kernel(x_prompt, x_sample, state_ml_conv, state_ml_C, state_ml_n, state_ml_m, cache_att_k, cache_att_v,
           w_in, b_ml_i, b_ml_f, ml_conv_w, ml_conv_b, ml_norm_g, att_rel_bias, w_ml_proj, w_att_proj, w_out,
           ffn1_w_gu, ffn1_w_down, ffn2_w_gu, ffn2_w_down, ln1_g, ln1_b, ln2_g, ln2_b, ln3_g, ln3_b):
    depth = w_in.shape[0]
    y_p, y_s = x_prompt, x_sample
    p_states, s_states = [], []
    for l in range(depth):
        w = _prep_weights({
            "w_in": w_in[l], "b_ml_i": b_ml_i[l], "b_ml_f": b_ml_f[l], "ml_conv_w": ml_conv_w[l],
            "ml_conv_b": ml_conv_b[l], "ml_norm_g": ml_norm_g[l], "att_rel_bias": att_rel_bias[l],
            "w_ml_proj": w_ml_proj[l], "w_att_proj": w_att_proj[l], "w_out": w_out[l],
            "ffn1_w_gu": ffn1_w_gu[l], "ffn1_w_down": ffn1_w_down[l], "ffn2_w_gu": ffn2_w_gu[l],
            "ffn2_w_down": ffn2_w_down[l], "ln1_g": ln1_g[l], "ln1_b": ln1_b[l], "ln2_g": ln2_g[l],
            "ln2_b": ln2_b[l], "ln3_g": ln3_g[l], "ln3_b": ln3_b[l]})
        y_p, st_p = _encoder_layer(y_p, w, None, None, None)
        y_s, st_s = _encoder_layer(y_s, w, state_ml_conv[l],
                                   (state_ml_C[l], state_ml_n[l], state_ml_m[l]),
                                   (cache_att_k[l], cache_att_v[l]))
        p_states.append(st_p)
        s_states.append(st_s)
    p_conv, p_c, p_n, p_m, p_k, p_v = [jnp.stack(s) for s in zip(*p_states)]
    s_conv, s_c, s_n, s_m, s_k, s_v = [jnp.stack(s) for s in zip(*s_states)]
    return (y_p, y_s, p_conv, s_conv, p_c, s_c, p_n, s_n, p_m, s_m, p_k, s_k, p_v, s_v)
```

```python
import functools

import jax
import jax.numpy as jnp
import numpy as np
from jax import lax
from jax.experimental import pallas as pl
from jax.experimental.pallas import tpu as pltpu

F32 = jnp.float32
BF16 = jnp.bfloat16

D_MODEL = 1024
CHUNK = 64
ML_HEADS = 4
ML_DHEAD = D_MODEL // ML_HEADS
ML_WIDTH = ML_HEADS * ML_DHEAD
CONV_W = 4
ATT_HEADS = 16
ATT_DHEAD = D_MODEL // ATT_HEADS
ATT_WIDTH = ATT_HEADS * ATT_DHEAD
BAND_CHUNKS = 8
BAND = BAND_CHUNKS * CHUNK
MAX_REL = 128
D_FF = ((8 * D_MODEL) // 3 + 127) // 128 * 128
ALPHA = 2.0 ** 0.25
LN_EPS = 1e-5
NEG_INF = -1e30
LOG2_E = 1.4426950408889634

LANES = 128
GATE_LANES = LANES
VMEM_LIMIT = 56 * 1024 * 1024

_O_MLI = 4 * ML_WIDTH
_O_AQ = _O_MLI + 2 * ML_HEADS
_W_V, _W_O, _W_AQ, _W_AK, _W_AV, _W_GM, _W_GA = 2, 3, 4, 5, 6, 7, 8


def _const_spec(shape, col_block=0):
    index = (0,) * (len(shape) - 1) + (col_block,)
    return pl.BlockSpec(shape, lambda *_: index, pipeline_mode=pl.Buffered(1))


def _params(*sem):
    return pltpu.CompilerParams(dimension_semantics=sem, vmem_limit_bytes=VMEM_LIMIT)


def _layer_norm(y, g, b):
    mu = jnp.mean(y, axis=-1, keepdims=True)
    yc = y - mu
    var = jnp.mean(yc * yc, axis=-1, keepdims=True)
    return yc * lax.rsqrt(var + LN_EPS) * g + b


def _sigmoid(x):
    return 1.0 / (1.0 + jnp.exp(-x))


_FF_CHUNKS = ((0, 768), (768, 1792), (1792, D_FF))
FFN_SUB_ROWS = 256
FFN_TILE_ROWS = 1024


def _ffn_ln_kernel(x_ref, wg_ref, wu_ref, wd_ref, g_ref, b_ref, o_ref, *ob_ref):
    def emit(rows, y):
        out = _layer_norm(y, g_ref[...], b_ref[...])
        o_ref[rows, :] = out
        for r in ob_ref:
            r[rows, :] = out.astype(BF16)

    pending = None
    for rows in (slice(r, r + FFN_SUB_ROWS) for r in range(0, x_ref.shape[0], FFN_SUB_ROWS)):
        x = x_ref[rows, :]
        xb = x.astype(BF16)
        acc = None
        for s, e in _FF_CHUNKS:
            gate = jnp.dot(xb, wg_ref[:, s:e], preferred_element_type=F32)
            up = jnp.dot(xb, wu_ref[:, s:e], preferred_element_type=F32)
            hid = (gate * _sigmoid(gate) * up).astype(BF16)
            part = jnp.dot(hid, wd_ref[s:e, :], preferred_element_type=F32)
            acc = part if acc is None else acc + part
        if pending is not None:
            emit(*pending)
        pending = (rows, ALPHA * x + 0.5 * acc)
    emit(*pending)


def _ffn_ln(x, wgu, wd, g, b, also_bf16=False):
    n = x.shape[0]
    tm = min(FFN_TILE_ROWS, n)
    row = pl.BlockSpec((tm, D_MODEL), lambda i: (i, 0))
    f32_out = jax.ShapeDtypeStruct((n, D_MODEL), F32)
    return pl.pallas_call(
        _ffn_ln_kernel,
        grid=(n // tm,),
        in_specs=[row, _const_spec((D_MODEL, D_FF), 0), _const_spec((D_MODEL, D_FF), 1), _const_spec(wd.shape),
                  _const_spec(g.shape), _const_spec(b.shape)],
        out_specs=[row, row] if also_bf16 else row,
        out_shape=[f32_out, jax.ShapeDtypeStruct((n, D_MODEL), BF16)] if also_bf16 else f32_out,
        compiler_params=_params("parallel"),
        name="ffn_ln",
    )(x, wgu, wgu, wd, g, b)


PROJ_TILE_ROWS = 512
COL_CHUNK = 256
CONV_ROWS = 16


def _chunk_jobs(h_ref, hb_ref):
    seqs, frames, _ = h_ref.shape
    for s in range(seqs):
        hb_ref[s * frames:(s + 1) * frames, :] = h_ref[s].astype(BF16)

    def proj(w_ref, c):
        cols = slice(c * COL_CHUNK, (c + 1) * COL_CHUNK)
        return jnp.dot(hb_ref[...], w_ref[:, cols], preferred_element_type=F32), cols

    def scatter(o_ref, cols, y):
        for s in range(seqs):
            o_ref[s, :, cols] = y[s * frames:(s + 1) * frames, :]

    def plain(w_ref, o_ref):
        def job(c):
            y, cols = proj(w_ref, c)
            scatter(o_ref, cols, y.astype(BF16))
        return job

    def gated(w_ref, o_ref):
        def job(c):
            y, cols = proj(w_ref, c)
            scatter(o_ref, cols, _sigmoid(y).astype(BF16))
        return job

    return proj, scatter, plain, gated


def _interleave(*queues):
    order = []
    for i in range(max(len(q) for q in queues)):
        for q in queues:
            order += q[i:i + 1]
    return order


def _in_proj_ml_kernel(h_ref, *refs, fresh):
    cprev_ref = None if fresh else refs[0]
    (wqk_ref, wv_ref, wo_ref, wif_ref, bif_ref, cw_ref, cb_ref,
     q_ref, kt_ref, v_ref, so_ref, gt_ref, conv_ref, hb_ref, raw_ref, kc_ref) = refs[0 if fresh else 1:]
    seqs, frames, _ = h_ref.shape
    proj, scatter, plain, gated = _chunk_jobs(h_ref, hb_ref)

    @pl.when(pl.program_id(1) == 0)
    def _():
        for s in range(seqs):
            raw_ref[s, 0:8, :] = jnp.zeros((8, 2 * ML_WIDTH), F32)
            if not fresh:
                raw_ref[s, 8 - (CONV_W - 1):8, :] = cprev_ref[s]

    def qk_raw(c):
        y, cols = proj(wqk_ref, c)
        for s in range(seqs):
            raw_ref[s, 8:8 + frames, cols] = y[s * frames:(s + 1) * frames, :]

    def qk_conv(c):
        cols = slice(c * COL_CHUNK, (c + 1) * COL_CHUNK)
        is_k = c * COL_CHUNK >= ML_WIDTH
        sub = lax.broadcasted_iota(jnp.int32, (8, COL_CHUNK), 0)
        for s in range(seqs):
            for r in range(frames // CONV_ROWS):
                base = 8 + r * CONV_ROWS
                blocks = [raw_ref[s, base + 8 * i:base + 8 * i + 8, cols] for i in range(-1, CONV_ROWS // 8)]
                taps = [jnp.concatenate(blocks[1:], axis=0)]
                for j in range(1, CONV_W):
                    taps.append(jnp.concatenate(
                        [pltpu.roll(jnp.where(sub < 8 - j, cur, prev), j, axis=0)
                         for prev, cur in zip(blocks[:-1], blocks[1:])], axis=0))
                acc = cb_ref[:, cols]
                for j in range(CONV_W):
                    acc = acc + taps[j] * cw_ref[CONV_W - 1 - j, :, cols]
                qk = acc * _sigmoid(acc)
                rows = slice(r * CONV_ROWS, (r + 1) * CONV_ROWS)
                if is_k:
                    kc_ref[rows, :] = qk * (ML_DHEAD ** -0.5)
                else:
                    q_ref[s, rows, cols] = qk.astype(BF16)
            if is_k:
                kt_ref[s, cols.start - ML_WIDTH:cols.stop - ML_WIDTH, :] = kc_ref[...].astype(BF16).T

    n_chunks = D_MODEL // COL_CHUNK
    light = [functools.partial(job, c) for c in range(n_chunks)
             for job in (plain(wv_ref, v_ref), gated(wo_ref, so_ref))]
    qk_raw(0)
    for c in range(2 * n_chunks):
        if c + 1 < 2 * n_chunks:
            qk_raw(c + 1)
        light[c]()
        qk_conv(c)
    for s in range(seqs):
        conv_ref[s] = raw_ref[s, 8 + frames - (CONV_W - 1):8 + frames, :]
        raw_ref[s, 0:8, :] = raw_ref[s, frames:frames + 8, :]

    zg = jnp.dot(hb_ref[...], wif_ref[...], preferred_element_type=F32) + bif_ref[...]
    lane = lax.broadcasted_iota(jnp.int32, zg.shape, 1)
    log_sig = jnp.minimum(zg, 0.0) - jnp.log(1.0 + jnp.exp(-jnp.abs(zg)))
    scatter(gt_ref, slice(None), jnp.where(lane < ML_HEADS, zg, log_sig))


def _in_proj_att_kernel(h_ref, waq_ref, wak_ref, wav_ref, wgm_ref, wga_ref,
                        aq_ref, akt_ref, av_ref, pk_ref, pv_ref, sgm_ref, sga_ref, hb_ref):
    seqs, frames, _ = h_ref.shape
    proj, scatter, plain, gated = _chunk_jobs(h_ref, hb_ref)

    def att_k(c):
        y, cols = proj(wak_ref, c)
        scatter(pk_ref, cols, y)
        yb = y.astype(BF16)
        for s in range(seqs):
            akt_ref[s, cols, :] = yb[s * frames:(s + 1) * frames, :].T

    def att_v(c):
        y, cols = proj(wav_ref, c)
        scatter(pv_ref, cols, y)
        scatter(av_ref, cols, y.astype(BF16))

    n_chunks = D_MODEL // COL_CHUNK
    heavy = [functools.partial(job, c) for c in range(n_chunks)
             for job in (gated(wgm_ref, sgm_ref), att_k, gated(wga_ref, sga_ref))]
    light = [functools.partial(job, c) for c in range(n_chunks) for job in (plain(waq_ref, aq_ref), att_v)]
    for job in _interleave(heavy, light):
        job()


def _in_proj(h, conv_prev, w):
    b, t, _ = h.shape
    tm = min(PROJ_TILE_ROWS, t)
    sb = max(1, min(b, PROJ_TILE_ROWS // t))
    nt = t // tm
    keep = min(BAND, t)
    first_kept = (t - keep) // tm
    h_spec = pl.BlockSpec((sb, tm, D_MODEL), lambda i, j: (i, j, 0))

    def tok(width, dtype):
        return (pl.BlockSpec((sb, tm, width), lambda i, j: (i, j, 0)),
                jax.ShapeDtypeStruct((b, t, width), dtype))

    def kept(width):
        return (pl.BlockSpec((sb, tm, width), lambda i, j: (i, jnp.maximum(j - first_kept, 0), 0)),
                jax.ShapeDtypeStruct((b, keep, width), F32))

    conv_spec = pl.BlockSpec((sb, CONV_W - 1, 2 * ML_WIDTH), lambda i, j: (i, 0, 0))
    feature_major = (pl.BlockSpec((sb, ATT_WIDTH, tm), lambda i, j: (i, 0, j)),
                     jax.ShapeDtypeStruct((b, ATT_WIDTH, t), BF16))
    ml_outs = [
        tok(ML_WIDTH, BF16), feature_major, tok(ML_WIDTH, BF16), tok(ML_WIDTH, BF16),
        tok(GATE_LANES, F32),
        (conv_spec, jax.ShapeDtypeStruct((b, CONV_W - 1, 2 * ML_WIDTH), F32)),
    ]
    w_in = w["w_in"]
    square = (D_MODEL, D_MODEL)
    ml_weights = [w_in, w_in, w_in, w["w_gate"], w["bif"], w["conv_w"], w["conv_b"]]
    fresh = conv_prev is None
    carried = [] if fresh else [conv_prev]
    ml = pl.pallas_call(
        functools.partial(_in_proj_ml_kernel, fresh=fresh),
        grid=(b // sb, nt),
        in_specs=[h_spec] + [conv_spec] * len(carried)
                 + [_const_spec((D_MODEL, 2 * ML_WIDTH), 0), _const_spec(square, _W_V), _const_spec(square, _W_O)]
                 + [_const_spec(a.shape) for a in ml_weights[3:]],
        out_specs=[o[0] for o in ml_outs],
        out_shape=[o[1] for o in ml_outs],
        scratch_shapes=[pltpu.VMEM((sb * tm, D_MODEL), BF16), pltpu.VMEM((sb, tm + 8, 2 * ML_WIDTH), F32),
                        pltpu.VMEM((tm, COL_CHUNK), F32)],
        compiler_params=_params("parallel", "arbitrary"),
        name="in_proj_ml",
    )(h, *carried, *ml_weights)

    att_outs = [
        tok(ATT_WIDTH, BF16),
        feature_major,
        tok(ATT_WIDTH, BF16),
        kept(ATT_WIDTH), kept(ATT_WIDTH),
        tok(D_MODEL, BF16), tok(D_MODEL, BF16),
    ]
    att_weights = [w_in] * 5
    att = pl.pallas_call(
        _in_proj_att_kernel,
        grid=(b // sb, nt),
        in_specs=[h_spec] + [_const_spec(square, c) for c in (_W_AQ, _W_AK, _W_AV, _W_GM, _W_GA)],
        out_specs=[o[0] for o in att_outs],
        out_shape=[o[1] for o in att_outs],
        scratch_shapes=[pltpu.VMEM((sb * tm, D_MODEL), BF16)],
        compiler_params=_params("parallel", "arbitrary"),
        name="in_proj_att",
    )(h, *att_weights)
    return tuple(ml) + tuple(att)


ML_STREAMS = 1


def _split3(x):
    hi = x.astype(BF16)
    rest = x - hi.astype(F32)
    mid = rest.astype(BF16)
    return hi, mid, (rest - mid.astype(F32)).astype(BF16)


def _mlstm_kernel(q_ref, kt_ref, v_ref, so_ref, gt_ref, *refs, blk, fresh):
    c0_ref, n0_ref, m0_ref = (None, None, None) if fresh else refs[:3]
    (ng_ref, hm_ref, c_ref, n_ref, m_ref,
     ncol_ref, d_ref, s_ref, hh_ref, wi_ref, emt_ref, rs_ref) = refs[0 if fresh else 3:]
    heads = range(ML_HEADS)
    streams = range(q_ref.shape[0])
    rb = min(ROW_BLOCK, blk)
    row_blocks = [slice(r * rb, (r + 1) * rb) for r in range(blk // rb)]
    head_cols = [slice(j * ML_DHEAD, (j + 1) * ML_DHEAD) for j in heads]
    slot = lambda s, j: s * ML_HEADS + j

    @pl.when(pl.program_id(1) == 0)
    def _():
        if fresh:
            c_ref[...] = jnp.zeros_like(c_ref)
            ncol_ref[...] = jnp.zeros_like(ncol_ref)
            m_ref[...] = jnp.zeros_like(m_ref)
        else:
            c_ref[...] = c0_ref[...]
            for s in streams:
                for j in heads:
                    ncol_ref[slot(s, j)] = n0_ref[s, j]
            m_ref[...] = m0_ref[...]

    tril = (lax.broadcasted_iota(jnp.int32, (blk, blk), 1)
            <= lax.broadcasted_iota(jnp.int32, (blk, blk), 0)).astype(BF16)
    bcum, li_rel_t, m_prev, m_new, decay, w_s_t = [], [], [], [], [], []
    for s in streams:
        gates = gt_ref[s]
        csum = sum(jnp.dot(tril, part, preferred_element_type=F32) for part in _split3(gates))
        bcum.append(pltpu.roll(csum, LANES - ML_HEADS, axis=1))
        li_rel = gates - bcum[s]
        li_rel_t.append(li_rel.T)
        m_prev.append(m_ref[s])
        b_last = bcum[s][blk - 1:blk, :]
        g_s = b_last + li_rel
        m_new.append(jnp.maximum(b_last + m_prev[s], jnp.max(g_s, axis=0, keepdims=True)))
        decay.append(jnp.exp(b_last + m_prev[s] - m_new[s]))
        w_s_t.append(jnp.exp(g_s - m_new[s]).T)

    def lane_of(x, j):
        lane = lax.broadcasted_iota(jnp.int32, x.shape, 1)
        return jnp.sum(jnp.where(lane == j, x, 0.0), axis=-1, keepdims=True)

    def spread(x, width):
        return x[:, :width] if width <= LANES else jnp.concatenate([x] * (width // LANES), axis=1)

    pairs = [(s, j) for s in streams for j in heads]
    units = [(s, j, rows) for s, j in pairs for rows in row_blocks]

    for s, j, rows in units:
        wi_ref[slot(s, j), rows, :] = jnp.broadcast_to(lane_of(bcum[s][rows, :], j), (rb, LANES))
    m_prev_j = {(s, j): lane_of(m_prev[s], j) for s, j in pairs}
    for s, j, rows in units:
        u = slot(s, j)
        causal = (lax.broadcasted_iota(jnp.int32, (rb, blk), 1)
                  <= lax.broadcasted_iota(jnp.int32, (rb, blk), 0) + rows.start)
        bcol = wi_ref[u, rows, :]
        dmat = jnp.where(causal, spread(bcol, blk) + li_rel_t[s][j:j + 1, :], -jnp.inf)
        icol = bcol + m_prev_j[s, j]
        m_t = jnp.maximum(icol, jnp.max(dmat, axis=-1, keepdims=True))
        d_ref[u, rows, :] = jnp.exp(dmat - spread(m_t, blk))
        wi_ref[u, rows, :] = jnp.exp(icol - m_t)
        emt_ref[u, rows, :] = jnp.exp(-m_t)

    for s, j in pairs:
        u = slot(s, j)
        qk = jnp.dot(q_ref[s, :, head_cols[j]], kt_ref[s, head_cols[j], :], preferred_element_type=F32)
        for rows in row_blocks:
            sc = qk[rows, :] * d_ref[u, rows, :]
            rs_ref[u, rows, :] = jnp.broadcast_to(jnp.sum(sc, axis=-1, keepdims=True), (rb, LANES))
            s_ref[u, rows, :] = sc.astype(BF16)

    for s, j in pairs:
        u = slot(s, j)
        qj, vj = q_ref[s, :, head_cols[j]], v_ref[s, :, head_cols[j]]
        q_c = jnp.dot(qj, c_ref[s, j].astype(BF16), preferred_element_type=F32)
        s_v = jnp.dot(s_ref[u], vj, preferred_element_type=F32)
        q_n = jnp.dot(qj, ncol_ref[u].astype(BF16), preferred_element_type=F32)
        for rows in row_blocks:
            w_inter = wi_ref[u, rows, :]
            den = w_inter * q_n[rows, :] + rs_ref[u, rows, :]
            scale = 1.0 / jnp.maximum(jnp.abs(den), emt_ref[u, rows, :])
            hh = (spread(w_inter, ML_DHEAD) * q_c[rows, :] + s_v[rows, :]) * spread(scale, ML_DHEAD)
            hh_ref[u, rows, :] = hh
            wi_ref[u, rows, :] = jnp.broadcast_to(jnp.mean(hh, axis=-1, keepdims=True), (rb, LANES))
    for s, j, rows in units:
        u = slot(s, j)
        hc = hh_ref[u, rows, :] - spread(wi_ref[u, rows, :], ML_DHEAD)
        var = jnp.mean(hc * hc, axis=-1, keepdims=True)
        hn = hc * lax.rsqrt(var + LN_EPS) * ng_ref[:, head_cols[j]] * so_ref[s, rows, head_cols[j]].astype(F32)
        hm_ref[s, rows, head_cols[j]] = hn.astype(BF16)

    ones = jnp.ones((blk, LANES), BF16)
    for s, j in pairs:
        u = slot(s, j)
        kw = (kt_ref[s, head_cols[j], :].astype(F32) * w_s_t[s][j:j + 1, :]).astype(BF16)
        d_j = lane_of(decay[s], j)
        c_ref[s, j] = d_j * c_ref[s, j] + jnp.dot(kw, v_ref[s, :, head_cols[j]], preferred_element_type=F32)
        ncol_ref[u] = d_j * ncol_ref[u] + jnp.dot(kw, ones, preferred_element_type=F32)
    for s in streams:
        m_ref[s] = m_new[s]

    @pl.when(pl.program_id(1) == pl.num_programs(1) - 1)
    def _():
        for s, j in pairs:
            n_ref[s, j:j + 1, :] = ncol_ref[slot(s, j)].T[0:1, :]


def _mlstm(q, kt, v, so, gates, state, norm_g):
    b, t, _ = q.shape
    blk = min(256, t)
    nb = ML_STREAMS if b % ML_STREAMS == 0 else 1
    carried = [] if state is None else list(state)
    tok = pl.BlockSpec((nb, blk, ML_WIDTH), lambda i, j: (i, j, 0))
    c_spec = pl.BlockSpec((nb, ML_HEADS, ML_DHEAD, ML_DHEAD), lambda i, j: (i, 0, 0, 0))
    n_spec = pl.BlockSpec((nb, ML_HEADS, ML_DHEAD), lambda i, j: (i, 0, 0))
    m_spec = pl.BlockSpec((nb, 1, LANES), lambda i, j: (i, 0, 0))
    per_head = lambda width, dtype: pltpu.VMEM((nb * ML_HEADS, blk, width), dtype)
    return pl.pallas_call(
        functools.partial(_mlstm_kernel, blk=blk, fresh=state is None),
        grid=(b // nb, t // blk),
        in_specs=[tok, pl.BlockSpec((nb, ML_WIDTH, blk), lambda i, j: (i, 0, j)), tok, tok,
                  pl.BlockSpec((nb, blk, GATE_LANES), lambda i, j: (i, j, 0))]
                 + [c_spec, pl.BlockSpec((nb, ML_HEADS, ML_DHEAD, LANES), lambda i, j: (i, 0, 0, 0)),
                    m_spec][:len(carried)]
                 + [_const_spec(norm_g.shape)],
        out_specs=[tok, c_spec, n_spec, m_spec],
        out_shape=[jax.ShapeDtypeStruct((b, t, ML_WIDTH), BF16),
                   jax.ShapeDtypeStruct((b, ML_HEADS, ML_DHEAD, ML_DHEAD), F32),
                   jax.ShapeDtypeStruct((b, ML_HEADS, ML_DHEAD), F32),
                   jax.ShapeDtypeStruct((b, 1, LANES), F32)],
        scratch_shapes=[pltpu.VMEM((nb * ML_HEADS, ML_DHEAD, LANES), F32),
                        per_head(blk, F32), per_head(blk, BF16), per_head(ML_DHEAD, F32),
                        per_head(LANES, F32), per_head(LANES, F32), per_head(LANES, F32)],
        compiler_params=_params("parallel", "arbitrary"),
        name="mlstm",
    )(q, kt, v, so, gates, *carried, norm_g)


KEY_BLOCK = 256
N_KEY_BLOCKS = 3
N_STAGE = 4
PIPE_AHEAD = 1
ROW_BLOCK = 64


def _band_attn_kernel(q_ref, kt0_ref, kt1_ref, kt2_ref, v0_ref, v1_ref, v2_ref, bias_ref, o_ref,
                      s_ref, p_ref, mx_ref, linv_ref, even_ref, *, lead_blocks):
    kt_refs = (kt0_ref, kt1_ref, kt2_ref)
    v_refs = (v0_ref, v1_ref, v2_ref)
    qb = q_ref.shape[1]
    lane = lax.broadcasted_iota(jnp.int32, (qb, LANES), 1)
    low = lane < ATT_DHEAD

    def attend(first_slot):
        slots = range(first_slot, N_KEY_BLOCKS)
        k_lo = first_slot * KEY_BLOCK

        def scores(head):
            sl = slice(head // 2 * LANES, (head // 2 + 1) * LANES)
            q2 = q_ref[0, :, sl].astype(F32)
            qh = jnp.where(low if head % 2 == 0 else jnp.logical_not(low), q2, 0.0).astype(BF16)
            row_max = None
            for c in slots:
                cols = slice(c * KEY_BLOCK, (c + 1) * KEY_BLOCK)
                s = jnp.dot(qh, kt_refs[c][0, sl, :], preferred_element_type=F32) + bias_ref[head, :, cols]
                s_ref[head % N_STAGE, :, cols] = s
                part = jnp.maximum(s[:, :LANES], s[:, LANES:])
                row_max = part if row_max is None else jnp.maximum(row_max, part)
            mx_ref[head % N_STAGE] = row_max

        def weighted_values(head):
            sl = slice(head // 2 * LANES, (head // 2 + 1) * LANES)
            buf = head % N_STAGE
            rb = min(ROW_BLOCK, qb)
            for r in range(qb // rb):
                rows = slice(r * rb, (r + 1) * rb)
                m = jnp.max(mx_ref[buf, rows, :], axis=-1, keepdims=True)
                row_sum = None
                for c in slots:
                    cols = slice(c * KEY_BLOCK, (c + 1) * KEY_BLOCK)
                    p = jnp.exp2(s_ref[buf, rows, cols] - m)
                    p_ref[buf, rows, cols] = p.astype(BF16)
                    part = p[:, :LANES] + p[:, LANES:]
                    row_sum = part if row_sum is None else row_sum + part
                l = jnp.sum(row_sum, axis=-1, keepdims=True)
                linv_ref[buf, rows, :] = jnp.broadcast_to(1.0 / l, (rb, LANES))
            v2 = jnp.concatenate([v_refs[c][0, :, sl] for c in slots], axis=0)
            return jnp.dot(p_ref[buf, :, k_lo:], v2, preferred_element_type=F32) * linv_ref[buf]

        for head in range(PIPE_AHEAD):
            scores(head)
        for head in range(ATT_HEADS):
            if head + PIPE_AHEAD < ATT_HEADS:
                scores(head + PIPE_AHEAD)
            out = weighted_values(head)
            if head % 2 == 0:
                even_ref[...] = out
            else:
                sl = slice(head // 2 * LANES, (head // 2 + 1) * LANES)
                o_ref[0, :, sl] = jnp.where(low, even_ref[...], out).astype(BF16)

    j = pl.program_id(1)
    for missing in range(lead_blocks, 0, -1):
        pl.when(j == lead_blocks - missing)(functools.partial(attend, missing))
    pl.when(j >= lead_blocks)(functools.partial(attend, 0))


BIAS_PERIOD = 1024


BIAS_HEADS_PER_STEP = 4


def _band_bias_kernel(diag_ref, o_ref, *, qb):
    nkeys = N_KEY_BLOCKS * KEY_BLOCK
    row = lax.broadcasted_iota(jnp.int32, (qb, nkeys), 0)
    kpos = lax.broadcasted_iota(jnp.int32, (qb, nkeys), 1) - (nkeys - qb)
    chunk_start = (row // CHUNK) * CHUNK
    in_band = jnp.logical_and(kpos >= chunk_start - BAND, kpos < chunk_start + CHUNK)
    for h in range(diag_ref.shape[0]):
        toep = pltpu.roll(jnp.broadcast_to(diag_ref[h], (qb, BIAS_PERIOD)), 0, 1, stride=1, stride_axis=0)
        o_ref[h] = jnp.where(in_band, toep[:, :nkeys] * LOG2_E, NEG_INF)


def _band_bias(table, qb):
    nkeys = N_KEY_BLOCKS * KEY_BLOCK
    heads = table.shape[0]
    shift = np.arange(BIAS_PERIOD)
    shift = np.where(shift < nkeys, shift, shift - BIAS_PERIOD)
    rel_idx = np.clip(nkeys - qb - shift, -MAX_REL, MAX_REL) + MAX_REL
    diag = table[:, rel_idx].astype(F32).reshape(heads, 1, BIAS_PERIOD)
    return pl.pallas_call(
        functools.partial(_band_bias_kernel, qb=qb),
        grid=(heads // BIAS_HEADS_PER_STEP,),
        in_specs=[pl.BlockSpec((BIAS_HEADS_PER_STEP, 1, BIAS_PERIOD), lambda h: (h, 0, 0))],
        out_specs=pl.BlockSpec((BIAS_HEADS_PER_STEP, qb, nkeys), lambda h: (h, 0, 0)),
        out_shape=jax.ShapeDtypeStruct((heads, qb, nkeys), F32),
        compiler_params=_params("parallel"),
        name="band_bias",
    )(diag)


def _cached_attn_kernel(q_ref, ktn_ref, vn_ref, kp_ref, vp_ref, bias_ref, o_ref):
    t, n_past = q_ref.shape[1], kp_ref.shape[2]
    low = lax.broadcasted_iota(jnp.int32, (t, LANES), 1) < ATT_DHEAD
    for g in range(ATT_HEADS // 2):
        sl = slice(g * LANES, (g + 1) * LANES)
        q2 = q_ref[0, :, sl].astype(F32)
        kpt, vpt = kp_ref[0, sl, :].astype(BF16), vp_ref[0, sl, :].astype(BF16)
        qs = jnp.concatenate([jnp.where(low, q2, 0.0), jnp.where(low, 0.0, q2)], axis=0).astype(BF16)
        bias = jnp.concatenate([bias_ref[2 * g], bias_ref[2 * g + 1]], axis=0)
        s_past = jnp.dot(qs, kpt, preferred_element_type=F32) + bias[:, :n_past]
        s_new = jnp.dot(qs, ktn_ref[0, sl, :], preferred_element_type=F32) + bias[:, n_past:]
        m = jnp.maximum(jnp.max(s_past, axis=-1, keepdims=True), jnp.max(s_new, axis=-1, keepdims=True))
        p_past, p_new = jnp.exp2(s_past - m), jnp.exp2(s_new - m)
        l = jnp.sum(p_past, axis=-1, keepdims=True) + jnp.sum(p_new, axis=-1, keepdims=True)
        pv = (lax.dot_general(p_past.astype(BF16), vpt, (((1,), (1,)), ((), ())), preferred_element_type=F32)
              + jnp.dot(p_new.astype(BF16), vn_ref[0, :, sl], preferred_element_type=F32)) / l
        o_ref[0, :, sl] = jnp.where(low, pv[:t, :], pv[t:, :]).astype(BF16)


def _cached_attn(q, kt_new, v_new, k_past, v_past, bias):
    b, t, _ = q.shape
    n_past = k_past.shape[2]
    tok = pl.BlockSpec((1, t, ATT_WIDTH), lambda i: (i, 0, 0))
    past = pl.BlockSpec((1, ATT_WIDTH, n_past), lambda i: (i, 0, 0))
    return pl.pallas_call(
        _cached_attn_kernel,
        grid=(b,),
        in_specs=[tok, pl.BlockSpec((1, ATT_WIDTH, t), lambda i: (i, 0, 0)), tok, past, past,
                  _const_spec(bias.shape)],
        out_specs=tok,
        out_shape=jax.ShapeDtypeStruct((b, t, ATT_WIDTH), BF16),
        compiler_params=_params("parallel"),
        name="cached_attn",
    )(q, kt_new, v_new, k_past, v_past, bias)


def _attention(q, kt, v, att_past, bias_tile):
    if att_past is None:
        return _band_attn(q, kt, v, bias_tile)
    k_past, v_past = att_past
    b, t, _ = q.shape
    n_past = k_past.shape[1]
    assert n_past == BAND and t <= CHUNK
    bias = bias_tile[:, :t, :n_past + t]
    feature_major = lambda a: jnp.swapaxes(a.reshape(b, n_past, ATT_WIDTH), 1, 2)
    return _cached_attn(q, kt, v, feature_major(k_past), feature_major(v_past), bias)


def _band_attn(q, kt, v, bias):
    b, tq, _ = q.shape
    qb = bias.shape[1]
    lead = N_KEY_BLOCKS - qb // KEY_BLOCK

    def kidx(s):
        return lambda i, j: jnp.maximum(j - lead + s, 0)

    kt_specs = [pl.BlockSpec((1, ATT_WIDTH, KEY_BLOCK), (lambda f: lambda i, j: (i, 0, f(i, j)))(kidx(s)))
                for s in range(N_KEY_BLOCKS)]
    v_specs = [pl.BlockSpec((1, KEY_BLOCK, ATT_WIDTH), (lambda f: lambda i, j: (i, f(i, j), 0))(kidx(s)))
               for s in range(N_KEY_BLOCKS)]
    tok = pl.BlockSpec((1, qb, ATT_WIDTH), lambda i, j: (i, j, 0))
    return pl.pallas_call(
        functools.partial(_band_attn_kernel, lead_blocks=lead),
        grid=(b, tq // qb),
        in_specs=[tok] + kt_specs + v_specs + [_const_spec(bias.shape)],
        out_specs=tok,
        out_shape=jax.ShapeDtypeStruct((b, tq, ATT_WIDTH), BF16),
        scratch_shapes=[pltpu.VMEM((N_STAGE, qb, N_KEY_BLOCKS * KEY_BLOCK), F32),
                        pltpu.VMEM((N_STAGE, qb, N_KEY_BLOCKS * KEY_BLOCK), BF16),
                        pltpu.VMEM((N_STAGE, qb, LANES), F32),
                        pltpu.VMEM((N_STAGE, qb, LANES), F32),
                        pltpu.VMEM((qb, LANES), F32)],
        compiler_params=_params("parallel", "arbitrary"),
        name="band_attn",
    )(q, kt, kt, kt, v, v, v, bias)


def _mix_ln_kernel(h_ref, hm_ref, oa_ref, sgm_ref, sga_ref, wm_ref, wa_ref, wout_ref, g_ref, b_ref, o_ref):
    pending = None
    for rows in (slice(r, r + FFN_SUB_ROWS) for r in range(0, h_ref.shape[0], FFN_SUB_ROWS)):
        y_ml = jnp.dot(hm_ref[rows, :], wm_ref[...], preferred_element_type=F32)
        y_att = jnp.dot(oa_ref[rows, :], wa_ref[...], preferred_element_type=F32)
        merged = sgm_ref[rows, :].astype(F32) * y_ml + sga_ref[rows, :].astype(F32) * y_att
        mix = jnp.dot(merged.astype(BF16), wout_ref[...], preferred_element_type=F32)
        if pending is not None:
            p_rows, p_y = pending
            o_ref[p_rows, :] = _layer_norm(p_y, g_ref[...], b_ref[...])
        pending = (rows, ALPHA * h_ref[rows, :] + mix)
    p_rows, p_y = pending
    o_ref[p_rows, :] = _layer_norm(p_y, g_ref[...], b_ref[...])


def _mix_ln(h, hm, oa, sgm, sga, wm, wa, wout, g, b):
    n = h.shape[0]
    tm = min(FFN_TILE_ROWS, n)
    row = pl.BlockSpec((tm, D_MODEL), lambda i: (i, 0))
    consts = [wm, wa, wout, g, b]
    return pl.pallas_call(
        _mix_ln_kernel,
        grid=(n // tm,),
        in_specs=[row] * 5 + [_const_spec(a.shape) for a in consts],
        out_specs=row,
        out_shape=jax.ShapeDtypeStruct((n, D_MODEL), F32),
        compiler_params=_params("parallel"),
        name="mix_ln",
    )(h, hm, oa, sgm, sga, *consts)


W_PREP_ROWS = 512


def _w_prep_kernel(a_ref, b_ref, g_ref, main_ref, gate_ref):
    c = pl.program_id(1)
    gate_rows = g_ref.shape[0]

    @pl.when(c < _W_AQ)
    def _():
        main_ref[...] = a_ref[...].T.astype(BF16)

    @pl.when(c >= _W_AQ)
    def _():
        moved = jnp.concatenate([a_ref[gate_rows:, :], b_ref[...]], axis=0)
        scale = jnp.where(c == _W_AQ, ATT_DHEAD ** -0.5 * LOG2_E, 1.0)
        main_ref[...] = (moved * scale).T.astype(BF16)

    @pl.when(c == _W_AQ)
    def _():
        padded = jnp.concatenate([g_ref[...], jnp.zeros((GATE_LANES - gate_rows, g_ref.shape[1]), F32)], axis=0)
        gate_ref[...] = padded.T.astype(BF16)


def _w_prep(w_in):
    w_t = jnp.swapaxes(w_in, 0, 1)
    n_blocks = _W_GA + 1
    gate_rows = _O_AQ - _O_MLI
    per_block = D_MODEL // gate_rows
    return pl.pallas_call(
        _w_prep_kernel,
        grid=(D_MODEL // W_PREP_ROWS, n_blocks),
        in_specs=[pl.BlockSpec((D_MODEL, W_PREP_ROWS), lambda r, c: (c, r)),
                  pl.BlockSpec((gate_rows, W_PREP_ROWS),
                               lambda r, c: (jnp.maximum(c + 1, _W_AQ + 1) * per_block, r)),
                  pl.BlockSpec((gate_rows, W_PREP_ROWS), lambda r, c: (_W_AQ * per_block, r))],
        out_specs=[pl.BlockSpec((W_PREP_ROWS, D_MODEL), lambda r, c: (r, c)),
                   pl.BlockSpec((W_PREP_ROWS, GATE_LANES), lambda r, c: (r, 0))],
        out_shape=[jax.ShapeDtypeStruct((D_MODEL, n_blocks * D_MODEL), BF16),
                   jax.ShapeDtypeStruct((D_MODEL, GATE_LANES), BF16)],
        compiler_params=_params("parallel", "arbitrary"),
        name="w_prep",
    )(w_t, w_t, w_t)


def _prep_weights(p):
    w_main, w_gate = _w_prep(p["w_in"])
    bif = jnp.concatenate([p["b_ml_i"], p["b_ml_f"], jnp.zeros((GATE_LANES - 2 * ML_HEADS,), F32)])
    row = lambda a: a.reshape(1, -1).astype(F32)
    return {
        "w_in": w_main,
        "w_gate": w_gate,
        "bif": row(bif),
        "conv_w": jnp.broadcast_to(p["ml_conv_w"].astype(F32)[:, None, :], (CONV_W, CONV_ROWS, 2 * ML_WIDTH)),
        "conv_b": jnp.broadcast_to(row(p["ml_conv_b"]), (CONV_ROWS, 2 * ML_WIDTH)),
        "norm_g": row(p["ml_norm_g"]),
        "wm": p["w_ml_proj"].astype(BF16),
        "wa": p["w_att_proj"].astype(BF16),
        "wout": p["w_out"].astype(BF16),
        "ffn1": (p["ffn1_w_gu"].astype(BF16), p["ffn1_w_down"].astype(BF16)),
        "ffn2": (p["ffn2_w_gu"].astype(BF16), p["ffn2_w_down"].astype(BF16)),
        "ln1": (row(p["ln1_g"]), row(p["ln1_b"])),
        "ln2": (row(p["ln2_g"]), row(p["ln2_b"])),
        "ln3": (row(p["ln3_g"]), row(p["ln3_b"])),
        "bias_tile": _band_bias(p["att_rel_bias"], KEY_BLOCK),
    }


def _encoder_layer(x, w, conv_prev, ml_state, att_past):
    b, t, _ = x.shape
    n = b * t
    flat = lambda a: a.reshape(n, a.shape[-1])
    h1, h1_bf16 = _ffn_ln(flat(x), *w["ffn1"], *w["ln1"], also_bf16=True)
    (q, k, v, so, gates, new_conv, aq, akt, av, k_rows, v_rows, sgm, sga) = _in_proj(
        h1_bf16.reshape(b, t, D_MODEL), conv_prev, w)

    if ml_state is not None:
        c0, n0, m0 = ml_state
        m0p = jnp.pad(m0.astype(F32), ((0, 0), (0, LANES - ML_HEADS))).reshape(b, 1, LANES)
        n0_rep = jnp.broadcast_to(n0.astype(F32)[..., None], n0.shape + (LANES,))
        ml_state = (c0.astype(F32), n0_rep, m0p)
    hm, c1, n1, m1p = _mlstm(q, k, v, so, gates, ml_state, w["norm_g"])
    m1 = m1p[:, 0, :ML_HEADS]

    oa = _attention(aq, akt, av, att_past, w["bias_tile"])
    h2 = _mix_ln(h1, flat(hm), flat(oa), flat(sgm), flat(sga), w["wm"], w["wa"], w["wout"], *w["ln2"])
    y = _ffn_ln(h2, *w["ffn2"], *w["ln3"]).reshape(b, t, D_MODEL)
    keep = k_rows.shape[1]
    state = (new_conv, c1, n1, m1,
             k_rows.reshape(b, keep, ATT_HEADS, ATT_DHEAD), v_rows.reshape(b, keep, ATT_HEADS, ATT_DHEAD))
    return y, state


def kernel(x_prompt, x_sample, state_ml_conv, state_ml_C, state_ml_n, state_ml_m, cache_att_k, cache_att_v,
           w_in, b_ml_i, b_ml_f, ml_conv_w, ml_conv_b, ml_norm_g, att_rel_bias, w_ml_proj, w_att_proj, w_out,
           ffn1_w_gu, ffn1_w_down, ffn2_w_gu, ffn2_w_down, ln1_g, ln1_b, ln2_g, ln2_b, ln3_g, ln3_b):
    depth = w_in.shape[0]
    y_p, y_s = x_prompt, x_sample
    p_states, s_states = [], []
    for l in range(depth):
        w = _prep_weights({
            "w_in": w_in[l], "b_ml_i": b_ml_i[l], "b_ml_f": b_ml_f[l], "ml_conv_w": ml_conv_w[l],
            "ml_conv_b": ml_conv_b[l], "ml_norm_g": ml_norm_g[l], "att_rel_bias": att_rel_bias[l],
            "w_ml_proj": w_ml_proj[l], "w_att_proj": w_att_proj[l], "w_out": w_out[l],
            "ffn1_w_gu": ffn1_w_gu[l], "ffn1_w_down": ffn1_w_down[l], "ffn2_w_gu": ffn2_w_gu[l],
            "ffn2_w_down": ffn2_w_down[l], "ln1_g": ln1_g[l], "ln1_b": ln1_b[l], "ln2_g": ln2_g[l],
            "ln2_b": ln2_b[l], "ln3_g": ln3_g[l], "ln3_b": ln3_b[l]})
        y_p, st_p = _encoder_layer(y_p, w, None, None, None)
        y_s, st_s = _encoder_layer(y_s, w, state_ml_conv[l],
                                   (state_ml_C[l], state_ml_n[l], state_ml_m[l]),
                                   (cache_att_k[l], cache_att_v[l]))
        p_states.append(st_p)
        s_states.append(st_s)
    p_conv, p_c, p_n, p_m, p_k, p_v = [jnp.stack(s) for s in zip(*p_states)]
    s_conv, s_c, s_n, s_m, s_k, s_v = [jnp.stack(s) for s in zip(*s_states)]
    return (y_p, y_s, p_conv, s_conv, p_c, s_c, p_n, s_n, p_m, s_m, p_k, s_k, p_v, s_v)
```
